```python
import math
import jax, jax.numpy as jnp
from jax import lax
import numpy as np

D_MODEL = 1024
BATCH = 4
SEQ = 8192
DEPTH = 2

GRID_W = 64
CTX_LEN = 256
N_MIXERS = 2
FF_HIDDEN = 2816
RW_HEAD = 64
RW_HEADS = D_MODEL // RW_HEAD
DECAY_LORA = 64
AAA_LORA = 64
GATE_LORA = 160
HY_N_FILT = 2
HY_BANDS = 16
HY_EMB = 2 * HY_BANDS + 1
HY_HID = 64
HY_FAST_DECAY = 0.3
HY_SLOW_DECAY = 1.5
HY_TARGET = 1e-2
NORM_EPS = 1e-6
GN_EPS = 64e-5
N_RWKV = (DEPTH + N_MIXERS - 1) // N_MIXERS
N_HYENA = DEPTH // N_MIXERS

kernel_name = 'hybrid_rwkv7_hyena_diffusion_block'

F32 = jnp.float32


def _rmsnorm(x, g):
    x32 = x.astype(F32)
    y = x32 * lax.rsqrt(jnp.mean(x32 * x32, axis=-1, keepdims=True) + NORM_EPS) * g.astype(F32)
    return y.astype(x.dtype)


def _pre(x, g, shift, scale):
    return _rmsnorm(x, g) * (1 + scale) + shift


def _swiglu(h, w_gu, w_down):
    gate, up = jnp.split(h @ w_gu, 2, axis=-1)
    return (jax.nn.silu(gate) * up) @ w_down


def _grid_shift(h):
    B, T, D = h.shape
    rows = T // GRID_W
    q = D // 4
    g = jnp.pad(h.reshape(B, rows, GRID_W, D), ((0, 0), (1, 1), (1, 1), (0, 0)))
    out = jnp.concatenate([g[:, 1:-1, :-2, :q], g[:, 1:-1, 2:, q:2 * q],
                           g[:, :-2, 1:-1, 2 * q:3 * q], g[:, 2:, 1:-1, 3 * q:]], axis=-1)
    return out.reshape(B, T, D)


def _seq_shift(h):
    q = h.shape[-1] // 4
    p = jnp.pad(h, ((0, 0), (1, 1), (0, 0)))
    prev, nxt = p[:, :-2], p[:, 2:]
    return jnp.concatenate([prev[..., :q], nxt[..., q:2 * q], prev[..., 2 * q:3 * q], nxt[..., 3 * q:]], axis=-1)


def _both(a):
    return jnp.stack([a, jnp.flip(a, axis=1)])


def _fb(a):
    return jnp.stack([a[0], jnp.flip(a[1], axis=1)])


def _wkv_scan(S0, w, k, v, kk, ka, r=None):
    tm = lambda a: jnp.moveaxis(a, 2, 0)
    xs = (tm(w), tm(k), tm(v), tm(kk), tm(ka)) + (() if r is None else (tm(r),))

    def step(S, inp):
        w_t, k_t, v_t, kk_t, ka_t = inp[:5]
        sa = jnp.einsum('dbhvk,dbhk->dbhv', S, kk_t)
        S = S * w_t[..., None, :] - sa[..., None] * ka_t[..., None, :] + v_t[..., None] * k_t[..., None, :]
        if r is None:
            return S, None
        return S, jnp.einsum('dbhvk,dbhk->dbhv', S, inp[5])

    S, y = lax.scan(step, S0, xs)
    return S, (None if y is None else jnp.moveaxis(y, 0, 2))


def _rwkv7_sequence(h, shifted, S0, p, with_out):
    (mix, w_rkv, w_o, w0, w1, w2, a0, a1, a2, g1, g2, k_k, k_a, r_k, ln_g, ln_b) = p
    B, T, D = h.shape
    heads = lambda a: a.reshape(a.shape[:-1] + (RW_HEADS, RW_HEAD)).astype(F32)
    xm = h[None] + (shifted - h)[None] * mix[:, None, None, :]
    xr, xw, xk, xv, xa, xg = xm
    k = xk @ w_rkv[1]
    vh = heads(xv @ w_rkv[2])
    w_pre = w0[:, None, None, :] + jnp.einsum('nbtr,nrd->nbtd', jnp.tanh(jnp.einsum('btd,ndr->nbtr', xw, w1)), w2)
    decay = jnp.exp(-jnp.exp(-jax.nn.softplus(-w_pre.astype(F32)) - 0.5))
    a = jax.nn.sigmoid((a0[:, None, None, :] + jnp.einsum('nbtr,nrd->nbtd', jnp.einsum('btd,ndr->nbtr', xa, a1), a2)).astype(F32))
    ah = heads(a)
    kk = heads(k * k_k)
    kk = kk / jnp.maximum(jnp.sqrt(jnp.sum(kk * kk, axis=-1, keepdims=True)), 1e-12)
    k_dir = heads(k)[None] * (1 + (ah - 1) * heads(k_a))
    ka = kk[None] * ah
    rh = heads(xr @ w_rkv[0]) if with_out else None
    S, y = _wkv_scan(S0, _fb(heads(decay)), _fb(k_dir), _both(vh), _both(kk), _fb(ka),
                     None if rh is None else _both(rh))
    if not with_out:
        return None, S
    y = y[0] + jnp.flip(y[1], axis=1)
    mu = jnp.mean(y, axis=-1, keepdims=True)
    var = jnp.mean(jnp.square(y - mu), axis=-1, keepdims=True)
    y = (y - mu) * lax.rsqrt(var + GN_EPS) * heads(ln_g) + heads(ln_b)
    k_bonus = 0.5 * (k_dir[0] + k_dir[1])
    y = y + jnp.sum(rh * k_bonus * r_k.astype(F32), axis=-1, keepdims=True) * vh
    g = jax.nn.sigmoid(xg @ g1) @ g2
    return (y.reshape(B, T, D).astype(h.dtype) * g) @ w_o, S


def _conv3(u, w, b):
    up = jnp.pad(u, ((0, 0), (1, 1), (0, 0)))
    return up[:, :-2] * w[0] + up[:, 1:-1] * w[1] + up[:, 2:] * w[2] + b


def _hyena_filters(L, f_w1, f_b1, f_freq, f_w2, f_b2, f_w3, deltas):
    f32 = lambda a: a.astype(F32)
    t = jnp.linspace(0.0, 1.0, L, dtype=F32)[:, None]
    ang = (2 * math.pi / L) * jnp.arange(L, dtype=F32)[:, None] * jnp.linspace(1e-4, HY_BANDS - 1, HY_BANDS, dtype=F32)[None]
    z = jnp.concatenate([t, jnp.cos(ang), -jnp.sin(ang)], axis=-1)
    hid = jnp.sin(f32(f_freq[0]) * (z @ f32(f_w1) + f32(f_b1)))
    hid = jnp.sin(f32(f_freq[1]) * (hid @ f32(f_w2) + f32(f_b2)))
    filt = (hid @ f32(f_w3)).reshape(L, HY_N_FILT, -1)
    return filt * jnp.exp(-t[:, :, None] * jnp.abs(f32(deltas)))


def _fftconv(u, filt, bias):
    L = u.shape[1]
    n = 2 * L
    y = jnp.fft.irfft(jnp.fft.rfft(u, n=n, axis=1) * jnp.fft.rfft(filt, n=n, axis=0)[None], n=n, axis=1)[:, :L]
    return y + u * bias.astype(F32)


def _hyena_sequence(h, p):
    (w_in, conv_w, conv_b, f_w1, f_b1, f_freq, f_w2, f_b2, f_w3, deltas, bias, w_out) = p
    L = h.shape[1]
    u = _conv3(h @ w_in, conv_w, conv_b).astype(F32)
    x1, x2, v = jnp.split(u, 3, axis=-1)
    filt = _hyena_filters(L, f_w1, f_b1, f_freq, f_w2, f_b2, f_w3, deltas)
    z = x1 * _fftconv(v, filt[:, 0], bias[0])
    z = x2 * _fftconv(z, filt[:, 1], bias[1])
    return z.astype(h.dtype) @ w_out


def setup_inputs(seed: int = 0) -> dict:
    key = jax.random.key(seed)
    ks = iter(jax.random.split(key, 64))
    nrm = lambda shape, s: jax.random.normal(next(ks), shape, F32) * s
    uni = lambda shape, lo, hi: jax.random.uniform(next(ks), shape, F32, lo, hi)
    D, F, NR, NH = D_MODEL, FF_HIDDEN, N_RWKV, N_HYENA
    max_decay = math.log(HY_TARGET) / HY_FAST_DECAY
    min_decay = math.log(HY_TARGET) / HY_SLOW_DECAY
    deltas = jnp.broadcast_to(jnp.linspace(min_decay, max_decay, D, dtype=F32), (NH, HY_N_FILT, D)) + nrm((NH, HY_N_FILT, D), 0.05)
    return {
        'x': nrm((BATCH, SEQ, D), 1.0),
        'c': nrm((BATCH, D), 1.0),
        'ctx': nrm((BATCH, CTX_LEN, D), 1.0),
        'c_ctx': nrm((D,), 1.0),
        'mod_w': nrm((DEPTH, D, 9 * D), 0.5 * D ** -0.5),
        'mod_b': nrm((DEPTH, 9 * D), 0.02),
        'norm_g': 1.0 + nrm((DEPTH, 3, D), 0.02),
        'ffn_w_gu': nrm((DEPTH, 2, D, 2 * F), D ** -0.5),
        'ffn_w_down': nrm((DEPTH, 2, F, D), F ** -0.5),
        'rw_mix': uni((NR, 6, D), 0.0, 1.0),
        'rw_w_rkv': nrm((NR, 3, D, D), D ** -0.5),
        'rw_w_o': nrm((NR, D, D), D ** -0.5),
        'rw_w0': uni((NR, 2, D), -6.5, -1.0),
        'rw_w1': nrm((NR, 2, D, DECAY_LORA), D ** -0.5),
        'rw_w2': nrm((NR, 2, DECAY_LORA, D), 0.5 * DECAY_LORA ** -0.5),
        'rw_a0': nrm((NR, 2, D), 0.1),
        'rw_a1': nrm((NR, 2, D, AAA_LORA), D ** -0.5),
        'rw_a2': nrm((NR, 2, AAA_LORA, D), 0.5 * AAA_LORA ** -0.5),
        'rw_g1': nrm((NR, D, GATE_LORA), D ** -0.5),
        'rw_g2': nrm((NR, GATE_LORA, D), GATE_LORA ** -0.5),
        'rw_k_k': 0.85 + nrm((NR, D), 0.02),
        'rw_k_a': 1.0 + nrm((NR, D), 0.02),
        'rw_r_k': nrm((NR, RW_HEADS, RW_HEAD), 0.1),
        'rw_ln_g': 1.0 + nrm((NR, D), 0.02),
        'rw_ln_b': nrm((NR, D), 0.02),
        'hy_w_in': nrm((NH, D, 3 * D), D ** -0.5),
        'hy_conv_w': nrm((NH, 3, 3 * D), 3 ** -0.5),
        'hy_conv_b': nrm((NH, 3 * D), 0.02),
        'hy_f_w1': nrm((NH, HY_EMB, HY_HID), HY_EMB ** -0.5),
        'hy_f_b1': nrm((NH, HY_HID), 0.1),
        'hy_f_freq': 1.0 + nrm((NH, 2, HY_HID), 0.02),
        'hy_f_w2': nrm((NH, HY_HID, HY_HID), HY_HID ** -0.5),
        'hy_f_b2': nrm((NH, HY_HID), 0.1),
        'hy_f_w3': nrm((NH, HY_HID, HY_N_FILT * D), 0.003),
        'hy_deltas': deltas,
        'hy_bias': nrm((NH, HY_N_FILT, D), 0.1),
        'hy_w_out': nrm((NH, D, D), D ** -0.5),
        'final_g': 1.0 + nrm((D,), 0.02),
    }


def reference(x, c, ctx, c_ctx, mod_w, mod_b, norm_g, ffn_w_gu, ffn_w_down,
              rw_mix, rw_w_rkv, rw_w_o, rw_w0, rw_w1, rw_w2, rw_a0, rw_a1, rw_a2, rw_g1, rw_g2,
              rw_k_k, rw_k_a, rw_r_k, rw_ln_g, rw_ln_b,
              hy_w_in, hy_conv_w, hy_conv_b, hy_f_w1, hy_f_b1, hy_f_freq, hy_f_w2, hy_f_b2, hy_f_w3,
              hy_deltas, hy_bias, hy_w_out, final_g):
    B = x.shape[0]
    rw = (rw_mix, rw_w_rkv, rw_w_o, rw_w0, rw_w1, rw_w2, rw_a0, rw_a1, rw_a2, rw_g1, rw_g2,
          rw_k_k, rw_k_a, rw_r_k, rw_ln_g, rw_ln_b)
    hy = (hy_w_in, hy_conv_w, hy_conv_b, hy_f_w1, hy_f_b1, hy_f_freq, hy_f_w2, hy_f_b2, hy_f_w3,
          hy_deltas, hy_bias, hy_w_out)
    last_rec = ((DEPTH - 1) // N_MIXERS) * N_MIXERS
    xl, xc = x, ctx
    for i in range(DEPTH):
        ctx_live, ctx_full = i <= last_rec, i < last_rec
        ml = [m[:, None, :] for m in jnp.split(jax.nn.silu(c) @ mod_w[i] + mod_b[i], 9, axis=-1)]
        mc = jnp.split(jax.nn.silu(c_ctx) @ mod_w[i] + mod_b[i], 9) if ctx_live else None
        xl = xl + 0.5 * ml[2] * _swiglu(_pre(xl, norm_g[i, 0], ml[0], ml[1]), ffn_w_gu[i, 0], ffn_w_down[i, 0])
        if ctx_live:
            xc = xc + 0.5 * mc[2] * _swiglu(_pre(xc, norm_g[i, 0], mc[0], mc[1]), ffn_w_gu[i, 0], ffn_w_down[i, 0])
        hl = _pre(xl, norm_g[i, 1], ml[3], ml[4])
        hc = _pre(xc, norm_g[i, 1], mc[3], mc[4]) if ctx_live else None
        j = i // N_MIXERS
        if i % N_MIXERS == 0:
            p = tuple(a[j] for a in rw)
            S0 = jnp.zeros((2, B, RW_HEADS, RW_HEAD, RW_HEAD), F32)
            yc, S_ctx = _rwkv7_sequence(hc, _seq_shift(hc), S0, p, ctx_full)
            yl, _ = _rwkv7_sequence(hl, _grid_shift(hl), S_ctx, p, True)
        else:
            p = tuple(a[j] for a in hy)
            yl = _hyena_sequence(hl, p)
            yc = _hyena_sequence(hc, p) if ctx_full else None
        xl = xl + ml[5] * yl
        if ctx_full:
            xc = xc + mc[5] * yc
            xc = xc + 0.5 * mc[8] * _swiglu(_pre(xc, norm_g[i, 2], mc[6], mc[7]), ffn_w_gu[i, 1], ffn_w_down[i, 1])
        xl = xl + 0.5 * ml[8] * _swiglu(_pre(xl, norm_g[i, 2], ml[6], ml[7]), ffn_w_gu[i, 1], ffn_w_down[i, 1])
    return _rmsnorm(xl, final_g)
```

```python
import functools
import math

import jax
import jax.numpy as jnp
import numpy as np
from jax import lax
from jax.experimental import pallas as pl
from jax.experimental.pallas import tpu as pltpu

F32 = jnp.float32
BF16 = jnp.bfloat16

GRID_W = 64
RW_HEAD = 64
NORM_EPS = 1e-6
GN_EPS = 64e-5
HY_BANDS = 16

LANES = 128
CHUNK = 64
VMEM_LIMIT = 56 * 1024 * 1024


def _cparams(*sem):
    return pltpu.CompilerParams(dimension_semantics=sem, vmem_limit_bytes=VMEM_LIMIT)


def _bdot(a, b):
    return jnp.dot(a.astype(BF16), b.astype(BF16), preferred_element_type=F32)


def _bdot_nt(a, b):
    return lax.dot_general(a.astype(BF16), b.astype(BF16), (((1,), (1,)), ((), ())),
                           preferred_element_type=F32)


def _bdot_tn(a, b):
    return lax.dot_general(a.astype(BF16), b.astype(BF16), (((0,), (0,)), ((), ())),
                           preferred_element_type=F32)


def _rms_mod(x, g, shift, scale):
    ms = jnp.mean(x * x, axis=-1, keepdims=True)
    return (x * lax.rsqrt(ms + NORM_EPS) * g) * (1.0 + scale) + shift


def _mod_kernel(c_ref, w_ref, b_ref, o_ref):
    c = c_ref[...]
    o_ref[...] = _bdot(c * jax.nn.sigmoid(c), w_ref[...]) + b_ref[...]


def _modulation(cc, w, b):
    m, d = cc.shape
    n = w.shape[1]
    tn = 1152
    out = pl.pallas_call(
        _mod_kernel,
        grid=(n // tn,),
        in_specs=[pl.BlockSpec((m, d), lambda j: (0, 0)),
                  pl.BlockSpec((d, tn), lambda j: (0, j)),
                  pl.BlockSpec((1, tn), lambda j: (0, j))],
        out_specs=pl.BlockSpec((m, tn), lambda j: (0, j)),
        out_shape=jax.ShapeDtypeStruct((m, n), F32),
        compiler_params=_cparams("parallel"),
        name="modulation",
    )(cc, w, b.reshape(1, n))
    return out.reshape(m, 9, d)


def _ffn_kernel(x_ref, mod_ref, g_ref, wgu_ref, wd_ref, *out_refs, mod_off, f_chunk, next_off):
    x = x_ref[...]
    mod = mod_ref[0]
    g = g_ref[...]
    h = _rms_mod(x, g[0:1], mod[mod_off:mod_off + 1], mod[mod_off + 1:mod_off + 2]).astype(BF16)
    f = wd_ref.shape[0]
    acc = jnp.zeros(x.shape, F32)
    for c0 in range(0, f, f_chunk):
        gate = jnp.dot(h, wgu_ref[:, c0:c0 + f_chunk], preferred_element_type=F32)
        up = jnp.dot(h, wgu_ref[:, f + c0:f + c0 + f_chunk], preferred_element_type=F32)
        a = (gate * jax.nn.sigmoid(gate) * up).astype(BF16)
        acc = acc + jnp.dot(a, wd_ref[c0:c0 + f_chunk, :], preferred_element_type=F32)
    xn = x + (0.5 * mod[mod_off + 2:mod_off + 3]) * acc
    if next_off == "final":
        ms = jnp.mean(xn * xn, axis=-1, keepdims=True)
        out_refs[0][...] = xn * lax.rsqrt(ms + NORM_EPS) * g[1:2]
        return
    out_refs[0][...] = xn
    if next_off is not None:
        out_refs[1][...] = _rms_mod(xn, g[1:2], mod[next_off:next_off + 1], mod[next_off + 1:next_off + 2])


def _ffn(x, mods, mod_row0, rows_per_mod, norm_g2, wgu, wd, mod_off, next_off, tm):
    m, d = x.shape
    f = wd.shape[0]
    f_chunk = f // 2 if (f // 2) % LANES == 0 else f
    n_out = 2 if isinstance(next_off, int) else 1
    kern = functools.partial(_ffn_kernel, mod_off=mod_off, f_chunk=f_chunk, next_off=next_off)
    tok = pl.BlockSpec((tm, d), lambda i: (i, 0))
    outs = pl.pallas_call(
        kern,
        grid=(m // tm,),
        in_specs=[tok,
                  pl.BlockSpec((1, 9, d), lambda i: (mod_row0 + (i * tm) // rows_per_mod, 0, 0)),
                  pl.BlockSpec((2, d), lambda i: (0, 0)),
                  pl.BlockSpec(wgu.shape, lambda i: (0, 0), pipeline_mode=pl.Buffered(1)),
                  pl.BlockSpec(wd.shape, lambda i: (0, 0), pipeline_mode=pl.Buffered(1))],
        out_specs=[tok] * n_out,
        out_shape=[jax.ShapeDtypeStruct((m, d), F32)] * n_out,
        compiler_params=_cparams("parallel"),
        name="ffn_halfstep",
    )(x, mods, norm_g2, wgu, wd)
    return outs if n_out == 2 else (outs[0], None)


def _head_sum(x):
    row = lax.broadcasted_iota(jnp.int32, (LANES, LANES), 0) // RW_HEAD
    col = lax.broadcasted_iota(jnp.int32, (LANES, LANES), 1) // RW_HEAD
    ones_bd = jnp.where(row == col, 1.0, 0.0).astype(BF16)
    hi = x.astype(BF16)
    lo = (x - hi.astype(F32)).astype(BF16)
    outs = []
    for j in range(x.shape[1] // LANES):
        sl = slice(j * LANES, (j + 1) * LANES)
        outs.append(jnp.dot(hi[:, sl], ones_bd, preferred_element_type=F32)
                    + jnp.dot(lo[:, sl], ones_bd, preferred_element_type=F32))
    return jnp.concatenate(outs, axis=1)


def _rwkv_prep_kernel(h_ref, hp_ref, hn_ref, mix_ref, wrkv_ref, w1_ref, w2_ref, a1_ref, a2_ref,
                      g1_ref, g2_ref, vec_ref,
                      k_ref, v_ref, kk_ref, r_ref, lw0_ref, lw1_ref, a0_ref, a1o_ref, bonus_ref, g_ref,
                      *, seq_len, grid_w, halo):
    tm, d = h_ref.shape
    q = d // 4
    h = h_ref[...]
    hext = jnp.concatenate([hp_ref[...], h, hn_ref[...]], axis=0)
    t = (pl.program_id(0) * tm + lax.broadcasted_iota(jnp.int32, (tm, 1), 0)) % seq_len
    if grid_w is None:
        offs = (-1, 1, -1, 1)
    else:
        offs = (-1, 1, -grid_w, grid_w)
    parts = []
    for qi, o in enumerate(offs):
        piece = hext[halo + o:halo + o + tm, qi * q:(qi + 1) * q]
        ok = (t + o >= 0) & (t + o < seq_len)
        if grid_w is not None and abs(o) == 1:
            colp = t % grid_w + o
            ok = ok & (colp >= 0) & (colp < grid_w)
        parts.append(jnp.where(ok, piece, 0.0))
    dx = jnp.concatenate(parts, axis=1) - h
    mix = mix_ref[...]
    vec = vec_ref[...]
    w0c, a0c = vec[0:1], vec[1:2]
    k_k, k_a, r_k = vec[2:3, :d], vec[3:4, :d], vec[4:5, :d]
    xr, xw, xk, xv, xa, xg = [(h + dx * mix[j:j + 1]).astype(BF16) for j in range(6)]
    r = jnp.dot(xr, wrkv_ref[0], preferred_element_type=F32)
    k = jnp.dot(xk, wrkv_ref[1], preferred_element_type=F32)
    v = jnp.dot(xv, wrkv_ref[2], preferred_element_type=F32)
    wl = jnp.tanh(jnp.dot(xw, w1_ref[...], preferred_element_type=F32))
    w_pre = w0c + _bdot(wl, w2_ref[...])
    lw = (-math.exp(-0.5)) * jax.nn.sigmoid(w_pre)
    al = jnp.dot(xa, a1_ref[...], preferred_element_type=F32)
    a = jax.nn.sigmoid(a0c + _bdot(al, a2_ref[...]))
    gl = jax.nn.sigmoid(jnp.dot(xg, g1_ref[...], preferred_element_type=F32))
    g = _bdot(gl, g2_ref[...])
    kk = k * k_k
    ss = _head_sum(kk * kk)
    kk = kk * lax.rsqrt(jnp.maximum(ss, 1e-24))
    a_f, a_b = a[:, :d], a[:, d:]
    k_bonus = k * (1.0 + (0.5 * (a_f + a_b) - 1.0) * k_a)
    bonus = _head_sum(r * k_bonus * r_k) * v
    k_ref[...] = k
    v_ref[...] = v
    kk_ref[...] = kk
    r_ref[...] = r
    lw0_ref[...] = lw[:, :d]
    lw1_ref[...] = lw[:, d:]
    a0_ref[...] = a_f
    a1o_ref[...] = a_b
    bonus_ref[...] = bonus
    g_ref[...] = g


def _rwkv_prep(h, seq_len, grid_w, p, tm):
    (mix, w_rkv, w1c, w2bd, a1c, a2bd, g1p, g2p, vec) = p
    m, d = h.shape
    halo = GRID_W
    nb = tm // halo
    last = m // halo - 1
    full = lambda a: pl.BlockSpec(a.shape, lambda i: (0,) * a.ndim, pipeline_mode=pl.Buffered(1))
    tok = pl.BlockSpec((tm, d), lambda i: (i, 0))
    kern = functools.partial(_rwkv_prep_kernel, seq_len=seq_len, grid_w=grid_w, halo=halo)
    return pl.pallas_call(
        kern,
        grid=(m // tm,),
        in_specs=[tok,
                  pl.BlockSpec((halo, d), lambda i: (jnp.maximum(i * nb - 1, 0), 0)),
                  pl.BlockSpec((halo, d), lambda i: (jnp.minimum((i + 1) * nb, last), 0)),
                  full(mix), full(w_rkv), full(w1c), full(w2bd), full(a1c), full(a2bd),
                  full(g1p), full(g2p), full(vec)],
        out_specs=[tok] * 10,
        out_shape=[jax.ShapeDtypeStruct((m, d), F32)] * 10,
        compiler_params=_cparams("parallel"),
        name="rwkv_prep",
    )(h, h, h, mix, w_rkv, w1c, w2bd, a1c, a2bd, g1p, g2p, vec)


def _stack_heads(x):
    lane = lax.broadcasted_iota(jnp.int32, x.shape, 1)
    first = lane < RW_HEAD
    return jnp.concatenate([jnp.where(first, x, 0.0), jnp.where(first, 0.0, x)], axis=0)


def _unit_tri_inverse(l_mat, ri, ci):
    b16 = (ri // 16) == (ci // 16)
    b32 = (ri // 32) == (ci // 32)
    l16 = jnp.where(b16, l_mat, 0.0)
    l32 = jnp.where(b32 & (~b16), l_mat, 0.0).astype(BF16)
    l64 = jnp.where(b32, 0.0, l_mat).astype(BF16)
    l16b = l16.astype(BF16)
    x = jnp.where(ri == ci, 1.0, 0.0) - l16
    p = jnp.dot(l16b, l16b, preferred_element_type=F32)
    for it in range(3):
        pb = p.astype(BF16)
        x = x + jnp.dot(x.astype(BF16), pb, preferred_element_type=F32)
        if it < 2:
            p = jnp.dot(pb, pb, preferred_element_type=F32)
    for off in (l32, l64):
        xb = x.astype(BF16)
        t = jnp.dot(off, xb, preferred_element_type=F32)
        x = x - jnp.dot(xb, t.astype(BF16), preferred_element_type=F32)
    return x


def _scan_chunk(s, lw, k, v, kk, r, a, k_a, reverse):
    c = lw.shape[0]
    n = 2 * c
    kd = k * (1.0 + (a - 1.0) * k_a)
    ka = kk * a
    ti = lax.broadcasted_iota(jnp.int32, (c, c), 0)
    tj = lax.broadcasted_iota(jnp.int32, (c, c), 1)
    tri = jnp.where((tj >= ti) if reverse else (tj <= ti), 1.0, 0.0)
    cum = jnp.dot(tri, lw, preferred_element_type=F32, precision=lax.Precision.HIGHEST)
    tot = cum[0:1] if reverse else cum[c - 1:c]
    e_neg = jnp.exp(-cum)
    e_end = jnp.exp(tot - cum)
    alpha = _stack_heads(kk * jnp.exp(cum - lw)).astype(BF16)
    beta = _stack_heads(ka * e_neg).astype(BF16)
    kappa = _stack_heads(kd * e_neg).astype(BF16)
    rho = _stack_heads(r * jnp.exp(cum))
    kappa_e = _stack_heads(kd * e_end).astype(BF16)
    beta_e = _stack_heads(ka * e_end).astype(BF16)
    v_st = _stack_heads(v).astype(BF16)
    w_end = jnp.exp(tot)

    nt = (((1,), (1,)), ((), ()))
    tn = (((0,), (0,)), ((), ()))
    sc = lax.dot_general(jnp.concatenate([alpha, rho.astype(BF16)], axis=0),
                         jnp.concatenate([beta, kappa], axis=0), nt, preferred_element_type=F32)
    ri = lax.broadcasted_iota(jnp.int32, (n, n), 0)
    ci = lax.broadcasted_iota(jnp.int32, (n, n), 1)
    strict = (ci > ri) if reverse else (ci < ri)
    incl = (ci >= ri) if reverse else (ci <= ri)
    l_mat = jnp.where(strict, sc[:n, :n], 0.0)
    ak = jnp.where(strict, sc[:n, n:], 0.0).astype(BF16)
    pb = jnp.where(incl, sc[n:, :n], 0.0).astype(BF16)
    pk = jnp.where(incl, sc[n:, n:], 0.0).astype(BF16)

    x = _unit_tri_inverse(l_mat, ri, ci).astype(BF16)
    akv = jnp.dot(ak, v_st, preferred_element_type=F32)
    uw = jnp.dot(x, jnp.concatenate([akv.astype(BF16), alpha], axis=1), preferred_element_type=F32)
    uwb = uw.astype(BF16)
    yr = (jnp.concatenate([jnp.dot(pk, v_st, preferred_element_type=F32), rho], axis=1)
          - jnp.dot(pb, uwb, preferred_element_type=F32))
    sb = s.astype(BF16)
    y_st = yr[:, :n] + lax.dot_general(yr[:, n:].astype(BF16), sb, nt, preferred_element_type=F32)
    y = y_st[:c] + y_st[c:]
    n_t = (lax.dot_general(v_st, kappa_e, tn, preferred_element_type=F32)
           - lax.dot_general(uwb[:, :n], beta_e, tn, preferred_element_type=F32))
    q = lax.dot_general(uwb[:, n:], beta_e, tn, preferred_element_type=F32)
    s_new = s * w_end - jnp.dot(sb, q.astype(BF16), preferred_element_type=F32) + n_t
    return s_new, y


def _scan_kernel(k_ref, v_ref, kk_ref, r_ref, lw_ref, a_ref, ka_ref, s0_ref, y_ref, sout_ref, s_scr,
                 *, reverse, chunk):
    tb = k_ref.shape[1]
    nchunk = tb // chunk
    step = pl.program_id(2)

    @pl.when(step == 0)
    def _():
        s_scr[...] = s0_ref[0, 0]

    k_a = ka_ref[...]

    def body(i, s):
        ci = (nchunk - 1 - i) if reverse else i
        sl = pl.ds(pl.multiple_of(ci * chunk, chunk), chunk)
        s, y = _scan_chunk(s, lw_ref[0, sl, :], k_ref[0, sl, :], v_ref[0, sl, :], kk_ref[0, sl, :],
                           r_ref[0, sl, :], a_ref[0, sl, :], k_a, reverse)
        y_ref[0, sl, :] = y
        return s

    s = lax.fori_loop(0, nchunk, body, s_scr[...])
    s_scr[...] = s

    @pl.when(step == pl.num_programs(2) - 1)
    def _():
        sout_ref[0, 0] = s


def _rwkv_scan(k, v, kk, r, lw, a, k_a, s0, reverse, tb):
    b, t, d = k.shape
    ns = d // LANES
    nt = t // tb
    tmap = (lambda bi, j, c: (bi, nt - 1 - c, j)) if reverse else (lambda bi, j, c: (bi, c, j))
    tok = pl.BlockSpec((1, tb, LANES), tmap)
    st = pl.BlockSpec((1, 1, LANES, LANES), lambda bi, j, c: (bi, j, 0, 0))
    kern = functools.partial(_scan_kernel, reverse=reverse, chunk=CHUNK)
    return pl.pallas_call(
        kern,
        grid=(b, ns, nt),
        in_specs=[tok] * 6 + [pl.BlockSpec((1, LANES), lambda bi, j, c: (0, j)), st],
        out_specs=[tok, st],
        out_shape=[jax.ShapeDtypeStruct((b, t, d), F32),
                   jax.ShapeDtypeStruct((b, ns, LANES, LANES), F32)],
        scratch_shapes=[pltpu.VMEM((LANES, LANES), F32)],
        compiler_params=_cparams("parallel", "parallel", "arbitrary"),
        name="rwkv_scan_bwd" if reverse else "rwkv_scan_fwd",
    )(k, v, kk, r, lw, a, k_a, s0)


def _rwkv_out_kernel(x_ref, yf_ref, yb_ref, bonus_ref, g_ref, mod_ref, ln_ref, wo_ref, o_ref):
    y = yf_ref[...] + yb_ref[...]
    ln = ln_ref[...]
    inv_n = 1.0 / RW_HEAD
    mu = _head_sum(y) * inv_n
    yc = y - mu
    var = _head_sum(yc * yc) * inv_n
    yn = yc * lax.rsqrt(var + GN_EPS) * ln[0:1] + ln[1:2]
    z = ((yn + bonus_ref[...]) * g_ref[...]).astype(BF16)
    out = jnp.dot(z, wo_ref[...], preferred_element_type=F32)
    o_ref[...] = x_ref[...] + mod_ref[0][5:6] * out


def _rwkv_out(x, yf, yb, bonus, g, mods, rows_per_mod, ln, w_o, tm):
    m, d = x.shape
    tok = pl.BlockSpec((tm, d), lambda i: (i, 0))
    return pl.pallas_call(
        _rwkv_out_kernel,
        grid=(m // tm,),
        in_specs=[tok] * 5 + [pl.BlockSpec((1, 9, d), lambda i: ((i * tm) // rows_per_mod, 0, 0)),
                              pl.BlockSpec((2, d), lambda i: (0, 0)),
                              pl.BlockSpec(w_o.shape, lambda i: (0, 0), pipeline_mode=pl.Buffered(1))],
        out_specs=tok,
        out_shape=jax.ShapeDtypeStruct((m, d), F32),
        compiler_params=_cparams("parallel"),
        name="rwkv_out",
    )(x, yf, yb, bonus, g, mods, ln, w_o)


def _hy_in_kernel(h_ref, hp_ref, hn_ref, w_ref, cw_ref, cb_ref, x1_ref, x2_ref, v_ref, *, seq_len):
    tm, d = h_ref.shape
    halo = hp_ref.shape[0]
    hext = jnp.concatenate([hp_ref[...], h_ref[...], hn_ref[...]], axis=0).astype(BF16)
    pe = jnp.dot(hext, w_ref[...], preferred_element_type=F32)
    t = (pl.program_id(0) * tm + lax.broadcasted_iota(jnp.int32, (tm, 1), 0)) % seq_len
    cw = cw_ref[...]
    prev = jnp.where(t >= 1, pe[halo - 1:halo - 1 + tm], 0.0)
    nxt = jnp.where(t + 1 < seq_len, pe[halo + 1:halo + 1 + tm], 0.0)
    u = prev * cw[0:1] + pe[halo:halo + tm] * cw[1:2] + nxt * cw[2:3] + cb_ref[...]
    x1_ref[...] = u[:, :d]
    x2_ref[...] = u[:, d:2 * d]
    v_ref[...] = u[:, 2 * d:]


def _hy_in(h, seq_len, w_in, conv_w, conv_b, tm):
    m, d = h.shape
    halo = 8
    nb = tm // halo
    last = m // halo - 1
    tok = pl.BlockSpec((tm, d), lambda i: (i, 0))
    full = lambda a: pl.BlockSpec(a.shape, lambda i: (0,) * a.ndim, pipeline_mode=pl.Buffered(1))
    return pl.pallas_call(
        functools.partial(_hy_in_kernel, seq_len=seq_len),
        grid=(m // tm,),
        in_specs=[tok,
                  pl.BlockSpec((halo, d), lambda i: (jnp.maximum(i * nb - 1, 0), 0)),
                  pl.BlockSpec((halo, d), lambda i: (jnp.minimum((i + 1) * nb, last), 0)),
                  full(w_in), full(conv_w), full(conv_b)],
        out_specs=[tok] * 3,
        out_shape=[jax.ShapeDtypeStruct((m, d), F32)] * 3,
        compiler_params=_cparams("parallel"),
        name="hyena_in",
    )(h, h, h, w_in, conv_w, conv_b)


def _hy_filter_kernel(z_ref, w1_ref, b1_ref, fr_ref, w2_ref, b2_ref, w3_ref, dl_ref, o_ref):
    hp = lax.Precision.HIGHEST
    z = z_ref[...]
    fr = fr_ref[...]
    hid = jnp.sin(fr[0:1] * (jnp.dot(z, w1_ref[...], precision=hp, preferred_element_type=F32) + b1_ref[...]))
    hid = jnp.sin(fr[1:2] * (jnp.dot(hid, w2_ref[...], precision=hp, preferred_element_type=F32) + b2_ref[...]))
    filt = jnp.dot(hid, w3_ref[...], precision=hp, preferred_element_type=F32)
    o_ref[...] = filt * jnp.exp(-z[:, 0:1] * jnp.abs(dl_ref[...]))


def _hy_filters(seq_len, f_w1, f_b1, f_freq, f_w2, f_b2, f_w3, deltas):
    t = jnp.linspace(0.0, 1.0, seq_len, dtype=F32)[:, None]
    ang = ((2 * math.pi / seq_len) * jnp.arange(seq_len, dtype=F32)[:, None]
           * jnp.linspace(1e-4, HY_BANDS - 1, HY_BANDS, dtype=F32)[None])
    z = jnp.concatenate([t, jnp.cos(ang), -jnp.sin(ang)], axis=-1)
    emb, hid = f_w1.shape
    z = jnp.pad(z, ((0, 0), (0, LANES - emb)))
    w1 = jnp.pad(f_w1, ((0, LANES - emb), (0, 0)))
    n_out = f_w3.shape[1]
    tl = min(seq_len, 512)
    full = lambda a: pl.BlockSpec(a.shape, lambda i: (0,) * a.ndim)
    args = (w1, f_b1.reshape(1, hid), f_freq, f_w2, f_b2.reshape(1, hid), f_w3, deltas.reshape(1, n_out))
    return pl.pallas_call(
        _hy_filter_kernel,
        grid=(seq_len // tl,),
        in_specs=[pl.BlockSpec((tl, LANES), lambda i: (i, 0))] + [full(a) for a in args],
        out_specs=pl.BlockSpec((tl, n_out), lambda i: (i, 0)),
        out_shape=jax.ShapeDtypeStruct((seq_len, n_out), F32),
        compiler_params=_cparams("parallel"),
        name="hyena_filters",
    )(z, *args)


def _dft_tables(seq_len):
    big = 2 * seq_len
    n = int(round(math.sqrt(big)))
    assert n * n == big and n % 32 == 0, "sequence length must give a square, tile-aligned DFT"
    k1 = np.arange(n)[None, :, None]
    n1 = np.arange(n // 2)[None, None, :]
    n2 = np.arange(n)[:, None, None]
    ang = -2.0 * np.pi * ((k1 * (n * n1 + n2)) % big) / big
    e_in = np.concatenate([np.cos(ang), np.sin(ang)], axis=1)
    e_out = np.transpose(e_in, (0, 2, 1))
    a2 = -2.0 * np.pi * ((np.arange(n)[:, None] * np.arange(n)[None, :]) % n) / n
    fr, fi = np.cos(a2), np.sin(a2)
    f_fwd = np.block([[fr, -fi], [fi, fr]])
    f_inv = np.block([[fr, fi], [-fi, fr]])
    cast = lambda a: jnp.asarray(a, dtype=F32).astype(BF16)
    return n, cast(e_in), cast(e_out), cast(f_fwd), cast(f_inv)


DFT_GROUP = 8


def _dft_in_kernel(x_ref, e_ref, o_ref, *, dx):
    for j in range(DFT_GROUP):
        sl = slice(j * dx, (j + 1) * dx)
        o_ref[0, :, sl] = jnp.dot(e_ref[j], x_ref[0, :, sl].astype(BF16),
                                  preferred_element_type=F32).astype(o_ref.dtype)


def _dft_in(x, n, e_in):
    b, l, dx = x.shape
    gw = DFT_GROUP * dx
    out = pl.pallas_call(
        functools.partial(_dft_in_kernel, dx=dx),
        grid=(b, n // DFT_GROUP),
        in_specs=[pl.BlockSpec((1, n // 2, gw), lambda bi, j: (bi, 0, j)),
                  pl.BlockSpec((DFT_GROUP, 2 * n, n // 2), lambda bi, j: (j, 0, 0))],
        out_specs=pl.BlockSpec((1, 2 * n, gw), lambda bi, j: (bi, 0, j)),
        out_shape=jax.ShapeDtypeStruct((b, 2 * n, n * dx), BF16),
        compiler_params=_cparams("parallel", "parallel"),
        name="dft_stage1",
    )(x.reshape(b, n // 2, n * dx), e_in)
    return out.reshape(b, 2, n, n, dx)


def _dft_mid_fwd_kernel(a_ref, f_ref, o_ref):
    two, n, td = a_ref.shape[1], a_ref.shape[2], a_ref.shape[3]
    a = a_ref[0].reshape(two * n, td)
    o_ref[...] = jnp.dot(f_ref[...], a, preferred_element_type=F32).reshape(two, n, td)


def _dft_spectrum(a, n, f_fwd):
    dx = a.shape[-1]
    td = min(dx, 1024)
    return pl.pallas_call(
        _dft_mid_fwd_kernel,
        grid=(n, dx // td),
        in_specs=[pl.BlockSpec((1, 2, None, n, td), lambda k, c: (0, 0, k, 0, c)),
                  pl.BlockSpec((2 * n, 2 * n), lambda k, c: (0, 0))],
        out_specs=pl.BlockSpec((2, None, n, td), lambda k, c: (0, k, 0, c)),
        out_shape=jax.ShapeDtypeStruct((2, n, n, dx), F32),
        compiler_params=_cparams("parallel", "parallel"),
        name="dft_filter_spectrum",
    )(a, f_fwd)


def _dft_mid_kernel(a_ref, h_ref, ff_ref, fi_ref, o_ref):
    two, n, td = a_ref.shape[1], a_ref.shape[2], a_ref.shape[3]
    x = jnp.dot(ff_ref[...], a_ref[0].reshape(two * n, td), preferred_element_type=F32)
    xr, xi = x[:n], x[n:]
    hr, hi = h_ref[0], h_ref[1]
    y = jnp.concatenate([xr * hr - xi * hi, xr * hi + xi * hr], axis=0).astype(BF16)
    o_ref[0] = jnp.dot(fi_ref[...], y, preferred_element_type=F32).reshape(two, n, td).astype(o_ref.dtype)


def _dft_mid(a, spec, f_idx, n, f_fwd, f_inv):
    b, dx = a.shape[0], a.shape[-1]
    return pl.pallas_call(
        _dft_mid_kernel,
        grid=(n, b),
        in_specs=[pl.BlockSpec((1, 2, None, n, dx), lambda k, bi: (bi, 0, k, 0, 0)),
                  pl.BlockSpec((2, None, n, dx), lambda k, bi: (0, k, 0, f_idx)),
                  pl.BlockSpec((2 * n, 2 * n), lambda k, bi: (0, 0)),
                  pl.BlockSpec((2 * n, 2 * n), lambda k, bi: (0, 0))],
        out_specs=pl.BlockSpec((1, 2, None, n, dx), lambda k, bi: (bi, 0, k, 0, 0)),
        out_shape=jax.ShapeDtypeStruct(a.shape, BF16),
        compiler_params=_cparams("parallel", "parallel"),
        name="dft_stage2_filter",
    )(a, spec, f_fwd, f_inv)


def _dft_out_kernel(z_ref, e_ref, u_ref, gate_ref, bias_ref, o_ref, *, inv_n, d):
    bias = bias_ref[...]
    for j in range(DFT_GROUP):
        sl = slice(j * d, (j + 1) * d)
        y = jnp.dot(e_ref[j], z_ref[0, :, sl], preferred_element_type=F32) * inv_n
        o_ref[0, :, sl] = gate_ref[0, :, sl] * (y + u_ref[0, :, sl] * bias)


def _dft_out(z, n, e_out, u, gate, bias):
    b, l, d = u.shape
    gw = DFT_GROUP * d
    row = pl.BlockSpec((1, n // 2, gw), lambda bi, j: (bi, 0, j))
    out = pl.pallas_call(
        functools.partial(_dft_out_kernel, inv_n=1.0 / (n * n), d=d),
        grid=(b, n // DFT_GROUP),
        in_specs=[pl.BlockSpec((1, 2 * n, gw), lambda bi, j: (bi, 0, j)),
                  pl.BlockSpec((DFT_GROUP, n // 2, 2 * n), lambda bi, j: (j, 0, 0)),
                  row, row, pl.BlockSpec((1, d), lambda bi, j: (0, 0))],
        out_specs=row,
        out_shape=jax.ShapeDtypeStruct((b, n // 2, n * d), F32),
        compiler_params=_cparams("parallel", "parallel"),
        name="dft_inverse_stage1",
    )(z.reshape(b, 2 * n, n * d), e_out, u.reshape(b, n // 2, n * d), gate.reshape(b, n // 2, n * d),
      bias.reshape(1, d))
    return out.reshape(b, l, d)


def _hyena_conv(u, gate, spec, f_idx, bias, tables):
    n, e_in, e_out, f_fwd, f_inv = tables
    a = _dft_in(u, n, e_in)
    z = _dft_mid(a, spec, f_idx, n, f_fwd, f_inv)
    return _dft_out(z, n, e_out, u, gate, bias)


def _mix_out_kernel(x_ref, z_ref, mod_ref, w_ref, o_ref):
    out = jnp.dot(z_ref[...].astype(BF16), w_ref[...], preferred_element_type=F32)
    o_ref[...] = x_ref[...] + mod_ref[0][5:6] * out


def _mix_out(x, z, mods, rows_per_mod, w_out, tm):
    m, d = x.shape
    tok = pl.BlockSpec((tm, d), lambda i: (i, 0))
    return pl.pallas_call(
        _mix_out_kernel,
        grid=(m // tm,),
        in_specs=[tok, tok, pl.BlockSpec((1, 9, d), lambda i: ((i * tm) // rows_per_mod, 0, 0)),
                  pl.BlockSpec(w_out.shape, lambda i: (0, 0), pipeline_mode=pl.Buffered(1))],
        out_specs=tok,
        out_shape=jax.ShapeDtypeStruct((m, d), F32),
        compiler_params=_cparams("parallel"),
        name="hyena_out",
    )(x, z, mods, w_out)


def _hyena_mixer(hl, batch, hp, tm):
    (w_in, conv_w, conv_b, f_w1, f_b1, f_freq, f_w2, f_b2, f_w3, deltas, bias) = hp
    m, d = hl.shape
    seq_len = m // batch
    x1, x2, v = _hy_in(hl, seq_len, w_in.astype(BF16), conv_w, conv_b.reshape(1, -1), tm)
    three = lambda a: a.reshape(batch, seq_len, d)
    x1, x2, v = three(x1), three(x2), three(v)
    tables = _dft_tables(seq_len)
    n, e_in, _, f_fwd, _ = tables
    filt = _hy_filters(seq_len, f_w1, f_b1, f_freq, f_w2, f_b2, f_w3, deltas)
    spec = _dft_spectrum(_dft_in(filt[None], n, e_in), n, f_fwd)
    z = _hyena_conv(v, x1, spec, 0, bias[0], tables)
    z = _hyena_conv(z, x2, spec, 1, bias[1], tables)
    return z.reshape(m, d)


def _rwkv_weights(mix, w_rkv, w0, w1, w2, a0, a1, a2, g1, g2, k_k, k_a, r_k):
    d = mix.shape[-1]
    lora = w1.shape[-1]
    zero = jnp.zeros((lora, d), F32)
    blockdiag = lambda m: jnp.concatenate([jnp.concatenate([m[0], zero], axis=1),
                                           jnp.concatenate([zero, m[1]], axis=1)], axis=0)
    gl = g1.shape[-1]
    glp = -(-gl // LANES) * LANES
    vec = jnp.zeros((8, 2 * d), F32)
    vec = vec.at[0].set(w0.reshape(-1)).at[1].set(a0.reshape(-1))
    vec = vec.at[2, :d].set(k_k).at[3, :d].set(k_a).at[4, :d].set(r_k.reshape(-1))
    return (mix, w_rkv.astype(BF16),
            jnp.concatenate([w1[0], w1[1]], axis=1).astype(BF16), blockdiag(w2).astype(BF16),
            jnp.concatenate([a1[0], a1[1]], axis=1).astype(BF16), blockdiag(a2).astype(BF16),
            jnp.pad(g1, ((0, 0), (0, glp - gl))).astype(BF16),
            jnp.pad(g2, ((0, glp - gl), (0, 0))).astype(BF16), vec)


def _rwkv_mixer(hl, hc, batch, wts, k_a, tm, tb):
    d = hl.shape[1]
    t_lat, t_ctx = hl.shape[0] // batch, hc.shape[0] // batch
    k_a = k_a.reshape(1, d)
    s0 = jnp.zeros((batch, d // LANES, LANES, LANES), F32)
    three = lambda arrs, t: [a.reshape(batch, t, d) for a in arrs]
    kc, vc, kkc, rc, lw0c, lw1c, a0c, a1c, _, _ = _rwkv_prep(hc, t_ctx, None, wts, min(tm, t_ctx))
    kc, vc, kkc, rc, lw0c, lw1c, a0c, a1c = three((kc, vc, kkc, rc, lw0c, lw1c, a0c, a1c), t_ctx)
    tbc = min(tb, t_ctx)
    _, s_f = _rwkv_scan(kc, vc, kkc, rc, lw0c, a0c, k_a, s0, False, tbc)
    _, s_b = _rwkv_scan(kc, vc, kkc, rc, lw1c, a1c, k_a, s0, True, tbc)
    k, v, kk, r, lw0, lw1, a0, a1, bonus, g = _rwkv_prep(hl, t_lat, GRID_W, wts, tm)
    k, v, kk, r, lw0, lw1, a0, a1 = three((k, v, kk, r, lw0, lw1, a0, a1), t_lat)
    yf, _ = _rwkv_scan(k, v, kk, r, lw0, a0, k_a, s_f, False, tb)
    yb, _ = _rwkv_scan(k, v, kk, r, lw1, a1, k_a, s_b, True, tb)
    return yf.reshape(-1, d), yb.reshape(-1, d), bonus, g


TOKEN_TILE = 512
PREP_TILE = 256
SCAN_BLOCK = 512


def kernel(x, c, ctx, c_ctx, mod_w, mod_b, norm_g, ffn_w_gu, ffn_w_down, rw_mix, rw_w_rkv, rw_w_o, rw_w0, rw_w1, rw_w2, rw_a0, rw_a1, rw_a2, rw_g1, rw_g2, rw_k_k, rw_k_a, rw_r_k, rw_ln_g, rw_ln_b, hy_w_in, hy_conv_w, hy_conv_b, hy_f_w1, hy_f_b1, hy_f_freq, hy_f_w2, hy_f_b2, hy_f_w3, hy_deltas, hy_bias, hy_w_out, final_g):
    batch, seq, d = x.shape
    t_ctx = ctx.shape[1]
    depth = mod_w.shape[0]
    assert depth == 2 and rw_mix.shape[0] == 1 and hy_w_in.shape[0] == 1
    assert seq % TOKEN_TILE == 0 and seq % GRID_W == 0 and t_ctx % CHUNK == 0 and batch + 1 <= 8

    cc = jnp.concatenate([c, c_ctx[None], jnp.zeros((8 - batch - 1, d), F32)], axis=0)
    mods0 = _modulation(cc, mod_w[0], mod_b[0])
    mods1 = _modulation(cc, mod_w[1], mod_b[1])
    wgu = ffn_w_gu.astype(BF16)
    wd = ffn_w_down.astype(BF16)
    xl = x.reshape(batch * seq, d)
    xc = ctx.reshape(batch * t_ctx, d)
    tm = TOKEN_TILE
    tmc = min(tm, t_ctx)

    xl, hl = _ffn(xl, mods0, 0, seq, norm_g[0, 0:2], wgu[0, 0], wd[0, 0], 0, 3, tm)
    _, hc = _ffn(xc, mods0, batch, batch * t_ctx, norm_g[0, 0:2], wgu[0, 0], wd[0, 0], 0, 3, tmc)
    wts = _rwkv_weights(rw_mix[0], rw_w_rkv[0], rw_w0[0], rw_w1[0], rw_w2[0], rw_a0[0], rw_a1[0], rw_a2[0],
                        rw_g1[0], rw_g2[0], rw_k_k[0], rw_k_a[0], rw_r_k[0])
    yf, yb, bonus, g = _rwkv_mixer(hl, hc, batch, wts, rw_k_a[0], PREP_TILE, SCAN_BLOCK)
    xl = _rwkv_out(xl, yf, yb, bonus, g, mods0, seq, jnp.stack([rw_ln_g[0], rw_ln_b[0]]),
                   rw_w_o[0].astype(BF16), tm)
    xl, _ = _ffn(xl, mods0, 0, seq, jnp.stack([norm_g[0, 2], final_g]), wgu[0, 1], wd[0, 1], 6, None, tm)

    xl, hl = _ffn(xl, mods1, 0, seq, norm_g[1, 0:2], wgu[1, 0], wd[1, 0], 0, 3, tm)
    hp = (hy_w_in[0], hy_conv_w[0], hy_conv_b[0], hy_f_w1[0], hy_f_b1[0], hy_f_freq[0], hy_f_w2[0],
          hy_f_b2[0], hy_f_w3[0], hy_deltas[0], hy_bias[0])
    z = _hyena_mixer(hl, batch, hp, tm)
    xl = _mix_out(xl, z, mods1, seq, hy_w_out[0].astype(BF16), tm)
    out, _ = _ffn(xl, mods1, 0, seq, jnp.stack([norm_g[1, 2], final_g]), wgu[1, 1], wd[1, 1], 6, "final", tm)
    return out.reshape(batch, seq, d)
```

```python
import functools
import math

import jax
import jax.numpy as jnp
import numpy as np
from jax import lax
from jax.experimental import pallas as pl
from jax.experimental.pallas import tpu as pltpu

F32 = jnp.float32
BF16 = jnp.bfloat16

GRID_W = 64
RW_HEAD = 64
NORM_EPS = 1e-6
GN_EPS = 64e-5
HY_BANDS = 16

LANES = 128
CHUNK = 64
VMEM_LIMIT = 56 * 1024 * 1024


def _cparams(*sem):
    return pltpu.CompilerParams(dimension_semantics=sem, vmem_limit_bytes=VMEM_LIMIT)


def _bdot(a, b):
    return jnp.dot(a.astype(BF16), b.astype(BF16), preferred_element_type=F32)


def _bdot_nt(a, b):
    return lax.dot_general(a.astype(BF16), b.astype(BF16), (((1,), (1,)), ((), ())),
                           preferred_element_type=F32)


def _bdot_tn(a, b):
    return lax.dot_general(a.astype(BF16), b.astype(BF16), (((0,), (0,)), ((), ())),
                           preferred_element_type=F32)


def _rms_mod(x, g, shift, scale):
    ms = jnp.mean(x * x, axis=-1, keepdims=True)
    return (x * lax.rsqrt(ms + NORM_EPS) * g) * (1.0 + scale) + shift


def _mod_kernel(c_ref, w_ref, b_ref, o_ref):
    c = c_ref[...]
    o_ref[...] = _bdot(c * jax.nn.sigmoid(c), w_ref[...]) + b_ref[...]


def _modulation(cc, w, b):
    m, d = cc.shape
    n = w.shape[1]
    tn = 1152
    out = pl.pallas_call(
        _mod_kernel,
        grid=(n // tn,),
        in_specs=[pl.BlockSpec((m, d), lambda j: (0, 0)),
                  pl.BlockSpec((d, tn), lambda j: (0, j)),
                  pl.BlockSpec((1, tn), lambda j: (0, j))],
        out_specs=pl.BlockSpec((m, tn), lambda j: (0, j)),
        out_shape=jax.ShapeDtypeStruct((m, n), F32),
        compiler_params=_cparams("parallel"),
        name="modulation",
    )(cc, w, b.reshape(1, n))
    return out.reshape(m, 9, d)


def _ffn_kernel(x_ref, mod_ref, g_ref, wgu_ref, wd_ref, *out_refs, mod_off, f_chunk, next_off):
    x = x_ref[...]
    mod = mod_ref[0]
    g = g_ref[...]
    h = _rms_mod(x, g[0:1], mod[mod_off:mod_off + 1], mod[mod_off + 1:mod_off + 2]).astype(BF16)
    f = wd_ref.shape[0]
    acc = jnp.zeros(x.shape, F32)
    for c0 in range(0, f, f_chunk):
        gate = jnp.dot(h, wgu_ref[:, c0:c0 + f_chunk], preferred_element_type=F32)
        up = jnp.dot(h, wgu_ref[:, f + c0:f + c0 + f_chunk], preferred_element_type=F32)
        a = (gate * jax.nn.sigmoid(gate) * up).astype(BF16)
        acc = acc + jnp.dot(a, wd_ref[c0:c0 + f_chunk, :], preferred_element_type=F32)
    xn = x + (0.5 * mod[mod_off + 2:mod_off + 3]) * acc
    if next_off == "final":
        ms = jnp.mean(xn * xn, axis=-1, keepdims=True)
        out_refs[0][...] = xn * lax.rsqrt(ms + NORM_EPS) * g[1:2]
        return
    out_refs[0][...] = xn
    if next_off is not None:
        out_refs[1][...] = _rms_mod(xn, g[1:2], mod[next_off:next_off + 1], mod[next_off + 1:next_off + 2])


def _ffn(x, mods, mod_row0, rows_per_mod, norm_g2, wgu, wd, mod_off, next_off, tm):
    m, d = x.shape
    f = wd.shape[0]
    f_chunk = f // 2 if (f // 2) % LANES == 0 else f
    n_out = 2 if isinstance(next_off, int) else 1
    kern = functools.partial(_ffn_kernel, mod_off=mod_off, f_chunk=f_chunk, next_off=next_off)
    tok = pl.BlockSpec((tm, d), lambda i: (i, 0))
    outs = pl.pallas_call(
        kern,
        grid=(m // tm,),
        in_specs=[tok,
                  pl.BlockSpec((1, 9, d), lambda i: (mod_row0 + (i * tm) // rows_per_mod, 0, 0)),
                  pl.BlockSpec((2, d), lambda i: (0, 0)),
                  pl.BlockSpec(wgu.shape, lambda i: (0, 0), pipeline_mode=pl.Buffered(1)),
                  pl.BlockSpec(wd.shape, lambda i: (0, 0), pipeline_mode=pl.Buffered(1))],
        out_specs=[tok] * n_out,
        out_shape=[jax.ShapeDtypeStruct((m, d), F32)] * n_out,
        compiler_params=_cparams("parallel"),
        name="ffn_halfstep",
    )(x, mods, norm_g2, wgu, wd)
    return outs if n_out == 2 else (outs[0], None)


def _head_sum(x):
    row = lax.broadcasted_iota(jnp.int32, (LANES, LANES), 0) // RW_HEAD
    col = lax.broadcasted_iota(jnp.int32, (LANES, LANES), 1) // RW_HEAD
    ones_bd = jnp.where(row == col, 1.0, 0.0).astype(BF16)
    hi = x.astype(BF16)
    lo = (x - hi.astype(F32)).astype(BF16)
    outs = []
    for j in range(x.shape[1] // LANES):
        sl = slice(j * LANES, (j + 1) * LANES)
        outs.append(jnp.dot(hi[:, sl], ones_bd, preferred_element_type=F32)
                    + jnp.dot(lo[:, sl], ones_bd, preferred_element_type=F32))
    return jnp.concatenate(outs, axis=1)


def _rwkv_prep_kernel(h_ref, hp_ref, hn_ref, mix_ref, wrkv_ref, w1_ref, w2_ref, a1_ref, a2_ref,
                      g1_ref, g2_ref, vec_ref,
                      k_ref, v_ref, kk_ref, r_ref, lw0_ref, lw1_ref, a0_ref, a1o_ref, bonus_ref, g_ref,
                      *, seq_len, grid_w, halo):
    tm, d = h_ref.shape
    q = d // 4
    h = h_ref[...]
    hext = jnp.concatenate([hp_ref[...], h, hn_ref[...]], axis=0)
    t = (pl.program_id(0) * tm + lax.broadcasted_iota(jnp.int32, (tm, 1), 0)) % seq_len
    if grid_w is None:
        offs = (-1, 1, -1, 1)
    else:
        offs = (-1, 1, -grid_w, grid_w)
    parts = []
    for qi, o in enumerate(offs):
        piece = hext[halo + o:halo + o + tm, qi * q:(qi + 1) * q]
        ok = (t + o >= 0) & (t + o < seq_len)
        if grid_w is not None and abs(o) == 1:
            colp = t % grid_w + o
            ok = ok & (colp >= 0) & (colp < grid_w)
        parts.append(jnp.where(ok, piece, 0.0))
    dx = jnp.concatenate(parts, axis=1) - h
    mix = mix_ref[...]
    vec = vec_ref[...]
    w0c, a0c = vec[0:1], vec[1:2]
    k_k, k_a, r_k = vec[2:3, :d], vec[3:4, :d], vec[4:5, :d]
    xr, xw, xk, xv, xa, xg = [(h + dx * mix[j:j + 1]).astype(BF16) for j in range(6)]
    r = jnp.dot(xr, wrkv_ref[0], preferred_element_type=F32)
    k = jnp.dot(xk, wrkv_ref[1], preferred_element_type=F32)
    v = jnp.dot(xv, wrkv_ref[2], preferred_element_type=F32)
    wl = jnp.tanh(jnp.dot(xw, w1_ref[...], preferred_element_type=F32))
    w_pre = w0c + _bdot(wl, w2_ref[...])
    lw = (-math.exp(-0.5)) * jax.nn.sigmoid(w_pre)
    al = jnp.dot(xa, a1_ref[...], preferred_element_type=F32)
    a = jax.nn.sigmoid(a0c + _bdot(al, a2_ref[...]))
    gl = jax.nn.sigmoid(jnp.dot(xg, g1_ref[...], preferred_element_type=F32))
    g = _bdot(gl, g2_ref[...])
    kk = k * k_k
    ss = _head_sum(kk * kk)
    kk = kk * lax.rsqrt(jnp.maximum(ss, 1e-24))
    a_f, a_b = a[:, :d], a[:, d:]
    k_bonus = k * (1.0 + (0.5 * (a_f + a_b) - 1.0) * k_a)
    bonus = _head_sum(r * k_bonus * r_k) * v
    k_ref[...] = k
    v_ref[...] = v
    kk_ref[...] = kk
    r_ref[...] = r
    lw0_ref[...] = lw[:, :d]
    lw1_ref[...] = lw[:, d:]
    a0_ref[...] = a_f
    a1o_ref[...] = a_b
    bonus_ref[...] = bonus
    g_ref[...] = g


def _rwkv_prep(h, seq_len, grid_w, p, tm):
    (mix, w_rkv, w1c, w2bd, a1c, a2bd, g1p, g2p, vec) = p
    m, d = h.shape
    halo = GRID_W
    nb = tm // halo
    last = m // halo - 1
    full = lambda a: pl.BlockSpec(a.shape, lambda i: (0,) * a.ndim, pipeline_mode=pl.Buffered(1))
    tok = pl.BlockSpec((tm, d), lambda i: (i, 0))
    kern = functools.partial(_rwkv_prep_kernel, seq_len=seq_len, grid_w=grid_w, halo=halo)
    return pl.pallas_call(
        kern,
        grid=(m // tm,),
        in_specs=[tok,
                  pl.BlockSpec((halo, d), lambda i: (jnp.maximum(i * nb - 1, 0), 0)),
                  pl.BlockSpec((halo, d), lambda i: (jnp.minimum((i + 1) * nb, last), 0)),
                  full(mix), full(w_rkv), full(w1c), full(w2bd), full(a1c), full(a2bd),
                  full(g1p), full(g2p), full(vec)],
        out_specs=[tok] * 10,
        out_shape=[jax.ShapeDtypeStruct((m, d), F32)] * 10,
        compiler_params=_cparams("parallel"),
        name="rwkv_prep",
    )(h, h, h, mix, w_rkv, w1c, w2bd, a1c, a2bd, g1p, g2p, vec)


def _stack_heads(x):
    lane = lax.broadcasted_iota(jnp.int32, x.shape, 1)
    first = lane < RW_HEAD
    return jnp.concatenate([jnp.where(first, x, 0.0), jnp.where(first, 0.0, x)], axis=0)


def _each(f, *lists):
    return [f(*args) for args in zip(*lists)]


def _mm(a, b):
    return jnp.dot(a, b, preferred_element_type=F32)


def _mm_nt(a, b):
    return lax.dot_general(a, b, (((1,), (1,)), ((), ())), preferred_element_type=F32)


def _mm_tn(a, b):
    return lax.dot_general(a, b, (((0,), (0,)), ((), ())), preferred_element_type=F32)


def _to_bf16(x):
    return x.astype(BF16)


def _unit_tri_inverse(l_mats, ri, ci):
    b16 = (ri // 16) == (ci // 16)
    b32 = (ri // 32) == (ci // 32)
    eye = jnp.where(ri == ci, 1.0, 0.0)
    l16 = _each(lambda l: jnp.where(b16, l, 0.0), l_mats)
    l16b = _each(_to_bf16, l16)
    x = _each(lambda l: eye - l, l16)
    p = _each(_mm, l16b, l16b)
    for it in range(3):
        pb = _each(_to_bf16, p)
        x = _each(lambda xx, pp: xx + _mm(xx.astype(BF16), pp), x, pb)
        if it < 2:
            p = _each(_mm, pb, pb)
    for sel in (lambda l: jnp.where(b32 & (~b16), l, 0.0), lambda l: jnp.where(b32, 0.0, l)):
        off = _each(lambda l: sel(l).astype(BF16), l_mats)
        xb = _each(_to_bf16, x)
        t = _each(_mm, off, xb)
        x = _each(lambda xx, xxb, tt: xx - _mm(xxb, tt.astype(BF16)), x, xb, t)
    return x


def _scan_chunks(s, lw, k, v, kk, r, a, k_a, reverse):
    c = lw[0].shape[0]
    n = 2 * c
    kd = _each(lambda k_, a_, ka_: k_ * (1.0 + (a_ - 1.0) * ka_), k, a, k_a)
    ka = _each(lambda kk_, a_: kk_ * a_, kk, a)
    ti = lax.broadcasted_iota(jnp.int32, (c, c), 0)
    tj = lax.broadcasted_iota(jnp.int32, (c, c), 1)
    tri = jnp.where((tj >= ti) if reverse else (tj <= ti), 1.0, 0.0)
    cum = _each(lambda x: jnp.dot(tri, x, preferred_element_type=F32, precision=lax.Precision.HIGHEST), lw)
    tot = _each(lambda x: x[0:1] if reverse else x[c - 1:c], cum)
    e_neg = _each(lambda x: jnp.exp(-x), cum)
    e_end = _each(lambda t_, x: jnp.exp(t_ - x), tot, cum)
    stack_b = lambda x: _stack_heads(x).astype(BF16)
    alpha = _each(lambda kk_, c_, l_: stack_b(kk_ * jnp.exp(c_ - l_)), kk, cum, lw)
    beta = _each(lambda x, e: stack_b(x * e), ka, e_neg)
    kappa = _each(lambda x, e: stack_b(x * e), kd, e_neg)
    rho = _each(lambda r_, c_: _stack_heads(r_ * jnp.exp(c_)), r, cum)
    kappa_e = _each(lambda x, e: stack_b(x * e), kd, e_end)
    beta_e = _each(lambda x, e: stack_b(x * e), ka, e_end)
    v_st = _each(stack_b, v)
    w_end = _each(jnp.exp, tot)

    sc = _each(lambda al, rh, be, kp: _mm_nt(jnp.concatenate([al, rh.astype(BF16)], axis=0),
                                             jnp.concatenate([be, kp], axis=0)), alpha, rho, beta, kappa)
    ri = lax.broadcasted_iota(jnp.int32, (n, n), 0)
    ci = lax.broadcasted_iota(jnp.int32, (n, n), 1)
    strict = (ci > ri) if reverse else (ci < ri)
    incl = (ci >= ri) if reverse else (ci <= ri)
    l_mat = _each(lambda x: jnp.where(strict, x[:n, :n], 0.0), sc)
    ak = _each(lambda x: jnp.where(strict, x[:n, n:], 0.0).astype(BF16), sc)
    pb = _each(lambda x: jnp.where(incl, x[n:, :n], 0.0).astype(BF16), sc)
    pk = _each(lambda x: jnp.where(incl, x[n:, n:], 0.0).astype(BF16), sc)

    akv = _each(_mm, ak, v_st)
    pkv = _each(_mm, pk, v_st)
    x = _each(_to_bf16, _unit_tri_inverse(l_mat, ri, ci))
    uwb = _each(lambda x_, akv_, al: _mm(x_, jnp.concatenate([akv_.astype(BF16), al], axis=1)).astype(BF16),
                x, akv, alpha)
    yr = _each(lambda pkv_, rh, pb_, uw: jnp.concatenate([pkv_, rh], axis=1) - _mm(pb_, uw), pkv, rho, pb, uwb)
    sb = _each(_to_bf16, s)
    y_st = _each(lambda yr_, sb_: yr_[:, :n] + _mm_nt(yr_[:, n:].astype(BF16), sb_), yr, sb)
    y = _each(lambda x_: x_[:c] + x_[c:], y_st)
    n_t = _each(lambda v_, ke, uw, be: _mm_tn(v_, ke) - _mm_tn(uw[:, :n], be), v_st, kappa_e, uwb, beta_e)
    q = _each(lambda uw, be: _mm_tn(uw[:, n:], be).astype(BF16), uwb, beta_e)
    s_new = _each(lambda s_, we, sb_, q_, nt_: s_ * we - _mm(sb_, q_) + nt_, s, w_end, sb, q, n_t)
    return s_new, y


SCAN_SLABS = 8


def _scan_kernel(k_ref, v_ref, kk_ref, r_ref, lw_ref, a_ref, ka_ref, s0_ref, y_ref, sout_ref, s_scr,
                 *, reverse, chunk):
    tb = k_ref.shape[1]
    nslab = s_scr.shape[0]
    nchunk = tb // chunk
    step = pl.program_id(2)

    @pl.when(step == 0)
    def _():
        s_scr[...] = s0_ref[0]

    def body(i, carry):
        ci = (nchunk - 1 - i) if reverse else i
        sl = pl.ds(pl.multiple_of(ci * chunk, chunk), chunk)
        lanes = [slice(g * LANES, (g + 1) * LANES) for g in range(nslab)]
        get = lambda ref: [ref[0, sl, ln] for ln in lanes]
        s_new, y = _scan_chunks([s_scr[g] for g in range(nslab)], get(lw_ref), get(k_ref), get(v_ref),
                                get(kk_ref), get(r_ref), get(a_ref), [ka_ref[:, ln] for ln in lanes],
                                reverse)
        for g in range(nslab):
            y_ref[0, sl, lanes[g]] = y[g]
            s_scr[g] = s_new[g]
        return carry

    lax.fori_loop(0, nchunk, body, 0)

    @pl.when(step == pl.num_programs(2) - 1)
    def _():
        sout_ref[0] = s_scr[...]


def _rwkv_scan(k, v, kk, r, lw, a, k_a, s0, reverse, tb):
    b, t, d = k.shape
    g = SCAN_SLABS
    gw = g * LANES
    nt = t // tb
    tmap = (lambda bi, j, c: (bi, nt - 1 - c, j)) if reverse else (lambda bi, j, c: (bi, c, j))
    tok = pl.BlockSpec((1, tb, gw), tmap)
    st = pl.BlockSpec((1, g, LANES, LANES), lambda bi, j, c: (bi, j, 0, 0))
    kern = functools.partial(_scan_kernel, reverse=reverse, chunk=CHUNK)
    return pl.pallas_call(
        kern,
        grid=(b, d // gw, nt),
        in_specs=[tok] * 6 + [pl.BlockSpec((1, gw), lambda bi, j, c: (0, j)), st],
        out_specs=[tok, st],
        out_shape=[jax.ShapeDtypeStruct((b, t, d), F32),
                   jax.ShapeDtypeStruct((b, d // LANES, LANES, LANES), F32)],
        scratch_shapes=[pltpu.VMEM((g, LANES, LANES), F32)],
        compiler_params=_cparams("parallel", "parallel", "arbitrary"),
        name="rwkv_scan_bwd" if reverse else "rwkv_scan_fwd",
    )(k, v, kk, r, lw, a, k_a, s0)


def _rwkv_out_kernel(x_ref, yf_ref, yb_ref, bonus_ref, g_ref, mod_ref, ln_ref, wo_ref, o_ref):
    y = yf_ref[...] + yb_ref[...]
    ln = ln_ref[...]
    inv_n = 1.0 / RW_HEAD
    mu = _head_sum(y) * inv_n
    yc = y - mu
    var = _head_sum(yc * yc) * inv_n
    yn = yc * lax.rsqrt(var + GN_EPS) * ln[0:1] + ln[1:2]
    z = ((yn + bonus_ref[...]) * g_ref[...]).astype(BF16)
    out = jnp.dot(z, wo_ref[...], preferred_element_type=F32)
    o_ref[...] = x_ref[...] + mod_ref[0][5:6] * out


def _rwkv_out(x, yf, yb, bonus, g, mods, rows_per_mod, ln, w_o, tm):
    m, d = x.shape
    tok = pl.BlockSpec((tm, d), lambda i: (i, 0))
    return pl.pallas_call(
        _rwkv_out_kernel,
        grid=(m // tm,),
        in_specs=[tok] * 5 + [pl.BlockSpec((1, 9, d), lambda i: ((i * tm) // rows_per_mod, 0, 0)),
                              pl.BlockSpec((2, d), lambda i: (0, 0)),
                              pl.BlockSpec(w_o.shape, lambda i: (0, 0), pipeline_mode=pl.Buffered(1))],
        out_specs=tok,
        out_shape=jax.ShapeDtypeStruct((m, d), F32),
        compiler_params=_cparams("parallel"),
        name="rwkv_out",
    )(x, yf, yb, bonus, g, mods, ln, w_o)


def _hy_in_kernel(h_ref, hp_ref, hn_ref, w_ref, cw_ref, cb_ref, x1_ref, x2_ref, v_ref, *, seq_len):
    tm, d = h_ref.shape
    halo = hp_ref.shape[0]
    hext = jnp.concatenate([hp_ref[...], h_ref[...], hn_ref[...]], axis=0).astype(BF16)
    pe = jnp.dot(hext, w_ref[...], preferred_element_type=F32)
    t = (pl.program_id(0) * tm + lax.broadcasted_iota(jnp.int32, (tm, 1), 0)) % seq_len
    cw = cw_ref[...]
    prev = jnp.where(t >= 1, pe[halo - 1:halo - 1 + tm], 0.0)
    nxt = jnp.where(t + 1 < seq_len, pe[halo + 1:halo + 1 + tm], 0.0)
    u = prev * cw[0:1] + pe[halo:halo + tm] * cw[1:2] + nxt * cw[2:3] + cb_ref[...]
    x1_ref[...] = u[:, :d]
    x2_ref[...] = u[:, d:2 * d]
    v_ref[...] = u[:, 2 * d:]


def _hy_in(h, seq_len, w_in, conv_w, conv_b, tm):
    m, d = h.shape
    halo = 8
    nb = tm // halo
    last = m // halo - 1
    tok = pl.BlockSpec((tm, d), lambda i: (i, 0))
    full = lambda a: pl.BlockSpec(a.shape, lambda i: (0,) * a.ndim, pipeline_mode=pl.Buffered(1))
    return pl.pallas_call(
        functools.partial(_hy_in_kernel, seq_len=seq_len),
        grid=(m // tm,),
        in_specs=[tok,
                  pl.BlockSpec((halo, d), lambda i: (jnp.maximum(i * nb - 1, 0), 0)),
                  pl.BlockSpec((halo, d), lambda i: (jnp.minimum((i + 1) * nb, last), 0)),
                  full(w_in), full(conv_w), full(conv_b)],
        out_specs=[tok] * 3,
        out_shape=[jax.ShapeDtypeStruct((m, d), F32)] * 3,
        compiler_params=_cparams("parallel"),
        name="hyena_in",
    )(h, h, h, w_in, conv_w, conv_b)


def _hy_filter_kernel(z_ref, w1_ref, b1_ref, fr_ref, w2_ref, b2_ref, w3_ref, dl_ref, o_ref):
    hp = lax.Precision.HIGHEST
    z = z_ref[...]
    fr = fr_ref[...]
    hid = jnp.sin(fr[0:1] * (jnp.dot(z, w1_ref[...], precision=hp, preferred_element_type=F32) + b1_ref[...]))
    hid = jnp.sin(fr[1:2] * (jnp.dot(hid, w2_ref[...], precision=hp, preferred_element_type=F32) + b2_ref[...]))
    filt = jnp.dot(hid, w3_ref[...], precision=hp, preferred_element_type=F32)
    o_ref[...] = filt * jnp.exp(-z[:, 0:1] * jnp.abs(dl_ref[...]))


def _hy_filters(seq_len, f_w1, f_b1, f_freq, f_w2, f_b2, f_w3, deltas):
    t = jnp.linspace(0.0, 1.0, seq_len, dtype=F32)[:, None]
    ang = ((2 * math.pi / seq_len) * jnp.arange(seq_len, dtype=F32)[:, None]
           * jnp.linspace(1e-4, HY_BANDS - 1, HY_BANDS, dtype=F32)[None])
    z = jnp.concatenate([t, jnp.cos(ang), -jnp.sin(ang)], axis=-1)
    emb, hid = f_w1.shape
    z = jnp.pad(z, ((0, 0), (0, LANES - emb)))
    w1 = jnp.pad(f_w1, ((0, LANES - emb), (0, 0)))
    n_out = f_w3.shape[1]
    tl = min(seq_len, 512)
    full = lambda a: pl.BlockSpec(a.shape, lambda i: (0,) * a.ndim)
    args = (w1, f_b1.reshape(1, hid), f_freq, f_w2, f_b2.reshape(1, hid), f_w3, deltas.reshape(1, n_out))
    return pl.pallas_call(
        _hy_filter_kernel,
        grid=(seq_len // tl,),
        in_specs=[pl.BlockSpec((tl, LANES), lambda i: (i, 0))] + [full(a) for a in args],
        out_specs=pl.BlockSpec((tl, n_out), lambda i: (i, 0)),
        out_shape=jax.ShapeDtypeStruct((seq_len, n_out), F32),
        compiler_params=_cparams("parallel"),
        name="hyena_filters",
    )(z, *args)


def _dft_tables(seq_len):
    big = 2 * seq_len
    n = int(round(math.sqrt(big)))
    assert n * n == big and n % 32 == 0, "sequence length must give a square, tile-aligned DFT"
    k1 = np.arange(n)[None, :, None]
    n1 = np.arange(n // 2)[None, None, :]
    n2 = np.arange(n)[:, None, None]
    ang = -2.0 * np.pi * ((k1 * (n * n1 + n2)) % big) / big
    e_in = np.concatenate([np.cos(ang), np.sin(ang)], axis=1)
    e_out = np.transpose(e_in, (0, 2, 1))
    a2 = -2.0 * np.pi * ((np.arange(n)[:, None] * np.arange(n)[None, :]) % n) / n
    fr, fi = np.cos(a2), np.sin(a2)
    f_fwd = np.block([[fr, -fi], [fi, fr]])
    f_inv = np.block([[fr, fi], [-fi, fr]])
    cast = lambda a: jnp.asarray(a, dtype=F32).astype(BF16)
    return n, cast(e_in), cast(e_out), cast(f_fwd), cast(f_inv)


DFT_GROUP = 8


def _dft_in_kernel(x_ref, e_ref, o_ref, *, dx):
    for j in range(DFT_GROUP):
        sl = slice(j * dx, (j + 1) * dx)
        o_ref[0, :, sl] = jnp.dot(e_ref[j], x_ref[0, :, sl].astype(BF16),
                                  preferred_element_type=F32).astype(o_ref.dtype)


def _dft_in(x, n, e_in):
    b, l, dx = x.shape
    gw = DFT_GROUP * dx
    out = pl.pallas_call(
        functools.partial(_dft_in_kernel, dx=dx),
        grid=(b, n // DFT_GROUP),
        in_specs=[pl.BlockSpec((1, n // 2, gw), lambda bi, j: (bi, 0, j)),
                  pl.BlockSpec((DFT_GROUP, 2 * n, n // 2), lambda bi, j: (j, 0, 0))],
        out_specs=pl.BlockSpec((1, 2 * n, gw), lambda bi, j: (bi, 0, j)),
        out_shape=jax.ShapeDtypeStruct((b, 2 * n, n * dx), BF16),
        compiler_params=_cparams("parallel", "parallel"),
        name="dft_stage1",
    )(x.reshape(b, n // 2, n * dx), e_in)
    return out.reshape(b, 2, n, n, dx)


def _dft_mid_fwd_kernel(a_ref, f_ref, o_ref):
    two, n, td = a_ref.shape[1], a_ref.shape[2], a_ref.shape[3]
    a = a_ref[0].reshape(two * n, td)
    o_ref[...] = jnp.dot(f_ref[...], a, preferred_element_type=F32).reshape(two, n, td)


def _dft_spectrum(a, n, f_fwd):
    dx = a.shape[-1]
    td = min(dx, 1024)
    return pl.pallas_call(
        _dft_mid_fwd_kernel,
        grid=(n, dx // td),
        in_specs=[pl.BlockSpec((1, 2, None, n, td), lambda k, c: (0, 0, k, 0, c)),
                  pl.BlockSpec((2 * n, 2 * n), lambda k, c: (0, 0))],
        out_specs=pl.BlockSpec((2, None, n, td), lambda k, c: (0, k, 0, c)),
        out_shape=jax.ShapeDtypeStruct((2, n, n, dx), F32),
        compiler_params=_cparams("parallel", "parallel"),
        name="dft_filter_spectrum",
    )(a, f_fwd)


def _dft_mid_kernel(a_ref, h_ref, ff_ref, fi_ref, o_ref):
    two, n, td = a_ref.shape[1], a_ref.shape[2], a_ref.shape[3]
    x = jnp.dot(ff_ref[...], a_ref[0].reshape(two * n, td), preferred_element_type=F32)
    xr, xi = x[:n], x[n:]
    hr, hi = h_ref[0], h_ref[1]
    y = jnp.concatenate([xr * hr - xi * hi, xr * hi + xi * hr], axis=0).astype(BF16)
    o_ref[0] = jnp.dot(fi_ref[...], y, preferred_element_type=F32).reshape(two, n, td).astype(o_ref.dtype)


def _dft_mid(a, spec, f_idx, n, f_fwd, f_inv):
    b, dx = a.shape[0], a.shape[-1]
    return pl.pallas_call(
        _dft_mid_kernel,
        grid=(n, b),
        in_specs=[pl.BlockSpec((1, 2, None, n, dx), lambda k, bi: (bi, 0, k, 0, 0)),
                  pl.BlockSpec((2, None, n, dx), lambda k, bi: (0, k, 0, f_idx)),
                  pl.BlockSpec((2 * n, 2 * n), lambda k, bi: (0, 0)),
                  pl.BlockSpec((2 * n, 2 * n), lambda k, bi: (0, 0))],
        out_specs=pl.BlockSpec((1, 2, None, n, dx), lambda k, bi: (bi, 0, k, 0, 0)),
        out_shape=jax.ShapeDtypeStruct(a.shape, BF16),
        compiler_params=_cparams("parallel", "parallel"),
        name="dft_stage2_filter",
    )(a, spec, f_fwd, f_inv)


def _dft_out_kernel(z_ref, e_ref, u_ref, gate_ref, bias_ref, o_ref, *, inv_n, d):
    bias = bias_ref[...]
    for j in range(DFT_GROUP):
        sl = slice(j * d, (j + 1) * d)
        y = jnp.dot(e_ref[j], z_ref[0, :, sl], preferred_element_type=F32) * inv_n
        o_ref[0, :, sl] = gate_ref[0, :, sl] * (y + u_ref[0, :, sl] * bias)


def _dft_out(z, n, e_out, u, gate, bias):
    b, l, d = u.shape
    gw = DFT_GROUP * d
    row = pl.BlockSpec((1, n // 2, gw), lambda bi, j: (bi, 0, j))
    out = pl.pallas_call(
        functools.partial(_dft_out_kernel, inv_n=1.0 / (n * n), d=d),
        grid=(b, n // DFT_GROUP),
        in_specs=[pl.BlockSpec((1, 2 * n, gw), lambda bi, j: (bi, 0, j)),
                  pl.BlockSpec((DFT_GROUP, n // 2, 2 * n), lambda bi, j: (j, 0, 0)),
                  row, row, pl.BlockSpec((1, d), lambda bi, j: (0, 0))],
        out_specs=row,
        out_shape=jax.ShapeDtypeStruct((b, n // 2, n * d), F32),
        compiler_params=_cparams("parallel", "parallel"),
        name="dft_inverse_stage1",
    )(z.reshape(b, 2 * n, n * d), e_out, u.reshape(b, n // 2, n * d), gate.reshape(b, n // 2, n * d),
      bias.reshape(1, d))
    return out.reshape(b, l, d)


def _hyena_conv(u, gate, spec, f_idx, bias, tables):
    n, e_in, e_out, f_fwd, f_inv = tables
    a = _dft_in(u, n, e_in)
    z = _dft_mid(a, spec, f_idx, n, f_fwd, f_inv)
    return _dft_out(z, n, e_out, u, gate, bias)


def _mix_out_kernel(x_ref, z_ref, mod_ref, w_ref, o_ref):
    out = jnp.dot(z_ref[...].astype(BF16), w_ref[...], preferred_element_type=F32)
    o_ref[...] = x_ref[...] + mod_ref[0][5:6] * out


def _mix_out(x, z, mods, rows_per_mod, w_out, tm):
    m, d = x.shape
    tok = pl.BlockSpec((tm, d), lambda i: (i, 0))
    return pl.pallas_call(
        _mix_out_kernel,
        grid=(m // tm,),
        in_specs=[tok, tok, pl.BlockSpec((1, 9, d), lambda i: ((i * tm) // rows_per_mod, 0, 0)),
                  pl.BlockSpec(w_out.shape, lambda i: (0, 0), pipeline_mode=pl.Buffered(1))],
        out_specs=tok,
        out_shape=jax.ShapeDtypeStruct((m, d), F32),
        compiler_params=_cparams("parallel"),
        name="hyena_out",
    )(x, z, mods, w_out)


def _hyena_mixer(hl, batch, hp, tm):
    (w_in, conv_w, conv_b, f_w1, f_b1, f_freq, f_w2, f_b2, f_w3, deltas, bias) = hp
    m, d = hl.shape
    seq_len = m // batch
    x1, x2, v = _hy_in(hl, seq_len, w_in.astype(BF16), conv_w, conv_b.reshape(1, -1), tm)
    three = lambda a: a.reshape(batch, seq_len, d)
    x1, x2, v = three(x1), three(x2), three(v)
    tables = _dft_tables(seq_len)
    n, e_in, _, f_fwd, _ = tables
    filt = _hy_filters(seq_len, f_w1, f_b1, f_freq, f_w2, f_b2, f_w3, deltas)
    spec = _dft_spectrum(_dft_in(filt[None], n, e_in), n, f_fwd)
    z = _hyena_conv(v, x1, spec, 0, bias[0], tables)
    z = _hyena_conv(z, x2, spec, 1, bias[1], tables)
    return z.reshape(m, d)


def _rwkv_weights(mix, w_rkv, w0, w1, w2, a0, a1, a2, g1, g2, k_k, k_a, r_k):
    d = mix.shape[-1]
    lora = w1.shape[-1]
    zero = jnp.zeros((lora, d), F32)
    blockdiag = lambda m: jnp.concatenate([jnp.concatenate([m[0], zero], axis=1),
                                           jnp.concatenate([zero, m[1]], axis=1)], axis=0)
    gl = g1.shape[-1]
    glp = -(-gl // LANES) * LANES
    vec = jnp.zeros((8, 2 * d), F32)
    vec = vec.at[0].set(w0.reshape(-1)).at[1].set(a0.reshape(-1))
    vec = vec.at[2, :d].set(k_k).at[3, :d].set(k_a).at[4, :d].set(r_k.reshape(-1))
    return (mix, w_rkv.astype(BF16),
            jnp.concatenate([w1[0], w1[1]], axis=1).astype(BF16), blockdiag(w2).astype(BF16),
            jnp.concatenate([a1[0], a1[1]], axis=1).astype(BF16), blockdiag(a2).astype(BF16),
            jnp.pad(g1, ((0, 0), (0, glp - gl))).astype(BF16),
            jnp.pad(g2, ((0, glp - gl), (0, 0))).astype(BF16), vec)


def _rwkv_mixer(hl, hc, batch, wts, k_a, tm, tb):
    d = hl.shape[1]
    t_lat, t_ctx = hl.shape[0] // batch, hc.shape[0] // batch
    k_a = k_a.reshape(1, d)
    s0 = jnp.zeros((batch, d // LANES, LANES, LANES), F32)
    three = lambda arrs, t: [a.reshape(batch, t, d) for a in arrs]
    kc, vc, kkc, rc, lw0c, lw1c, a0c, a1c, _, _ = _rwkv_prep(hc, t_ctx, None, wts, min(tm, t_ctx))
    kc, vc, kkc, rc, lw0c, lw1c, a0c, a1c = three((kc, vc, kkc, rc, lw0c, lw1c, a0c, a1c), t_ctx)
    tbc = min(tb, t_ctx)
    _, s_f = _rwkv_scan(kc, vc, kkc, rc, lw0c, a0c, k_a, s0, False, tbc)
    _, s_b = _rwkv_scan(kc, vc, kkc, rc, lw1c, a1c, k_a, s0, True, tbc)
    k, v, kk, r, lw0, lw1, a0, a1, bonus, g = _rwkv_prep(hl, t_lat, GRID_W, wts, tm)
    k, v, kk, r, lw0, lw1, a0, a1 = three((k, v, kk, r, lw0, lw1, a0, a1), t_lat)
    yf, _ = _rwkv_scan(k, v, kk, r, lw0, a0, k_a, s_f, False, tb)
    yb, _ = _rwkv_scan(k, v, kk, r, lw1, a1, k_a, s_b, True, tb)
    return yf.reshape(-1, d), yb.reshape(-1, d), bonus, g


TOKEN_TILE = 512
PREP_TILE = 256
SCAN_BLOCK = 512


def kernel(x, c, ctx, c_ctx, mod_w, mod_b, norm_g, ffn_w_gu, ffn_w_down, rw_mix, rw_w_rkv, rw_w_o, rw_w0, rw_w1, rw_w2, rw_a0, rw_a1, rw_a2, rw_g1, rw_g2, rw_k_k, rw_k_a, rw_r_k, rw_ln_g, rw_ln_b, hy_w_in, hy_conv_w, hy_conv_b, hy_f_w1, hy_f_b1, hy_f_freq, hy_f_w2, hy_f_b2, hy_f_w3, hy_deltas, hy_bias, hy_w_out, final_g):
    batch, seq, d = x.shape
    t_ctx = ctx.shape[1]
    depth = mod_w.shape[0]
    assert depth == 2 and rw_mix.shape[0] == 1 and hy_w_in.shape[0] == 1
    assert seq % TOKEN_TILE == 0 and seq % GRID_W == 0 and t_ctx % CHUNK == 0 and batch + 1 <= 8

    cc = jnp.concatenate([c, c_ctx[None], jnp.zeros((8 - batch - 1, d), F32)], axis=0)
    mods0 = _modulation(cc, mod_w[0], mod_b[0])
    mods1 = _modulation(cc, mod_w[1], mod_b[1])
    wgu = ffn_w_gu.astype(BF16)
    wd = ffn_w_down.astype(BF16)
    xl = x.reshape(batch * seq, d)
    xc = ctx.reshape(batch * t_ctx, d)
    tm = TOKEN_TILE
    tmc = min(tm, t_ctx)

    xl, hl = _ffn(xl, mods0, 0, seq, norm_g[0, 0:2], wgu[0, 0], wd[0, 0], 0, 3, tm)
    _, hc = _ffn(xc, mods0, batch, batch * t_ctx, norm_g[0, 0:2], wgu[0, 0], wd[0, 0], 0, 3, tmc)
    wts = _rwkv_weights(rw_mix[0], rw_w_rkv[0], rw_w0[0], rw_w1[0], rw_w2[0], rw_a0[0], rw_a1[0], rw_a2[0],
                        rw_g1[0], rw_g2[0], rw_k_k[0], rw_k_a[0], rw_r_k[0])
    yf, yb, bonus, g = _rwkv_mixer(hl, hc, batch, wts, rw_k_a[0], PREP_TILE, SCAN_BLOCK)
    xl = _rwkv_out(xl, yf, yb, bonus, g, mods0, seq, jnp.stack([rw_ln_g[0], rw_ln_b[0]]),
                   rw_w_o[0].astype(BF16), tm)
    xl, _ = _ffn(xl, mods0, 0, seq, jnp.stack([norm_g[0, 2], final_g]), wgu[0, 1], wd[0, 1], 6, None, tm)

    xl, hl = _ffn(xl, mods1, 0, seq, norm_g[1, 0:2], wgu[1, 0], wd[1, 0], 0, 3, tm)
    hp = (hy_w_in[0], hy_conv_w[0], hy_conv_b[0], hy_f_w1[0], hy_f_b1[0], hy_f_freq[0], hy_f_w2[0],
          hy_f_b2[0], hy_f_w3[0], hy_deltas[0], hy_bias[0])
    z = _hyena_mixer(hl, batch, hp, tm)
    xl = _mix_out(xl, z, mods1, seq, hy_w_out[0].astype(BF16), tm)
    out, _ = _ffn(xl, mods1, 0, seq, jnp.stack([norm_g[1, 2], final_g]), wgu[1, 1], wd[1, 1], 6, "final", tm)
    return out.reshape(batch, seq, d)
```

```python
import functools
import math

import jax
import jax.numpy as jnp
import numpy as np
from jax import lax
from jax.experimental import pallas as pl
from jax.experimental.pallas import tpu as pltpu

F32 = jnp.float32
BF16 = jnp.bfloat16

GRID_W = 64
RW_HEAD = 64
NORM_EPS = 1e-6
GN_EPS = 64e-5
HY_BANDS = 16

LANES = 128
CHUNK = 64
VMEM_LIMIT = 56 * 1024 * 1024


def _cparams(*sem):
    return pltpu.CompilerParams(dimension_semantics=sem, vmem_limit_bytes=VMEM_LIMIT)


def _bdot(a, b):
    return jnp.dot(a.astype(BF16), b.astype(BF16), preferred_element_type=F32)


def _bdot_nt(a, b):
    return lax.dot_general(a.astype(BF16), b.astype(BF16), (((1,), (1,)), ((), ())),
                           preferred_element_type=F32)


def _bdot_tn(a, b):
    return lax.dot_general(a.astype(BF16), b.astype(BF16), (((0,), (0,)), ((), ())),
                           preferred_element_type=F32)


def _rms_mod(x, g, shift, scale):
    ms = jnp.mean(x * x, axis=-1, keepdims=True)
    return (x * lax.rsqrt(ms + NORM_EPS) * g) * (1.0 + scale) + shift


def _mod_kernel(c_ref, w_ref, b_ref, o_ref):
    c = c_ref[...]
    o_ref[...] = _bdot(c * jax.nn.sigmoid(c), w_ref[...]) + b_ref[...]


def _modulation(cc, w, b):
    m, d = cc.shape
    n = w.shape[1]
    tn = 1152
    out = pl.pallas_call(
        _mod_kernel,
        grid=(n // tn,),
        in_specs=[pl.BlockSpec((m, d), lambda j: (0, 0)),
                  pl.BlockSpec((d, tn), lambda j: (0, j)),
                  pl.BlockSpec((1, tn), lambda j: (0, j))],
        out_specs=pl.BlockSpec((m, tn), lambda j: (0, j)),
        out_shape=jax.ShapeDtypeStruct((m, n), F32),
        compiler_params=_cparams("parallel"),
        name="modulation",
    )(cc, w, b.reshape(1, n))
    return out.reshape(m, 9, d)


def _ffn_kernel(x_ref, mod_ref, g_ref, wgu_ref, wd_ref, *out_refs, mod_off, f_chunk, next_off):
    x = x_ref[...]
    mod = mod_ref[0]
    g = g_ref[...]
    h = _rms_mod(x, g[0:1], mod[mod_off:mod_off + 1], mod[mod_off + 1:mod_off + 2]).astype(BF16)
    f = wd_ref.shape[0]
    acc = jnp.zeros(x.shape, F32)
    for c0 in range(0, f, f_chunk):
        gate = jnp.dot(h, wgu_ref[:, c0:c0 + f_chunk], preferred_element_type=F32)
        up = jnp.dot(h, wgu_ref[:, f + c0:f + c0 + f_chunk], preferred_element_type=F32)
        a = (gate * jax.nn.sigmoid(gate) * up).astype(BF16)
        acc = acc + jnp.dot(a, wd_ref[c0:c0 + f_chunk, :], preferred_element_type=F32)
    xn = x + (0.5 * mod[mod_off + 2:mod_off + 3]) * acc
    if next_off == "final":
        ms = jnp.mean(xn * xn, axis=-1, keepdims=True)
        out_refs[0][...] = xn * lax.rsqrt(ms + NORM_EPS) * g[1:2]
        return
    out_refs[0][...] = xn
    if next_off is not None:
        out_refs[1][...] = _rms_mod(xn, g[1:2], mod[next_off:next_off + 1], mod[next_off + 1:next_off + 2])


def _ffn(x, mods, mod_row0, rows_per_mod, norm_g2, wgu, wd, mod_off, next_off, tm):
    m, d = x.shape
    f = wd.shape[0]
    f_chunk = f // 2 if (f // 2) % LANES == 0 else f
    n_out = 2 if isinstance(next_off, int) else 1
    kern = functools.partial(_ffn_kernel, mod_off=mod_off, f_chunk=f_chunk, next_off=next_off)
    tok = pl.BlockSpec((tm, d), lambda i: (i, 0))
    outs = pl.pallas_call(
        kern,
        grid=(m // tm,),
        in_specs=[tok,
                  pl.BlockSpec((1, 9, d), lambda i: (mod_row0 + (i * tm) // rows_per_mod, 0, 0)),
                  pl.BlockSpec((2, d), lambda i: (0, 0)),
                  pl.BlockSpec(wgu.shape, lambda i: (0, 0), pipeline_mode=pl.Buffered(1)),
                  pl.BlockSpec(wd.shape, lambda i: (0, 0), pipeline_mode=pl.Buffered(1))],
        out_specs=[tok] * n_out,
        out_shape=[jax.ShapeDtypeStruct((m, d), F32)] * n_out,
        compiler_params=_cparams("parallel"),
        name="ffn_halfstep",
    )(x, mods, norm_g2, wgu, wd)
    return outs if n_out == 2 else (outs[0], None)


def _head_sum(x):
    row = lax.broadcasted_iota(jnp.int32, (LANES, LANES), 0) // RW_HEAD
    col = lax.broadcasted_iota(jnp.int32, (LANES, LANES), 1) // RW_HEAD
    ones_bd = jnp.where(row == col, 1.0, 0.0).astype(BF16)
    hi = x.astype(BF16)
    lo = (x - hi.astype(F32)).astype(BF16)
    outs = []
    for j in range(x.shape[1] // LANES):
        sl = slice(j * LANES, (j + 1) * LANES)
        outs.append(jnp.dot(hi[:, sl], ones_bd, preferred_element_type=F32)
                    + jnp.dot(lo[:, sl], ones_bd, preferred_element_type=F32))
    return jnp.concatenate(outs, axis=1)


def _rwkv_prep_kernel(h_ref, hp_ref, hn_ref, mix_ref, wrkv_ref, w1_ref, w2_ref, a1_ref, a2_ref,
                      g1_ref, g2_ref, vec_ref,
                      k_ref, v_ref, kk_ref, r_ref, lw0_ref, lw1_ref, a0_ref, a1o_ref, bonus_ref, g_ref,
                      *, seq_len, grid_w, halo):
    tm, d = h_ref.shape
    q = d // 4
    h = h_ref[...]
    hext = jnp.concatenate([hp_ref[...], h, hn_ref[...]], axis=0)
    t = (pl.program_id(0) * tm + lax.broadcasted_iota(jnp.int32, (tm, 1), 0)) % seq_len
    if grid_w is None:
        offs = (-1, 1, -1, 1)
    else:
        offs = (-1, 1, -grid_w, grid_w)
    parts = []
    for qi, o in enumerate(offs):
        piece = hext[halo + o:halo + o + tm, qi * q:(qi + 1) * q]
        ok = (t + o >= 0) & (t + o < seq_len)
        if grid_w is not None and abs(o) == 1:
            colp = t % grid_w + o
            ok = ok & (colp >= 0) & (colp < grid_w)
        parts.append(jnp.where(ok, piece, 0.0))
    dx = jnp.concatenate(parts, axis=1) - h
    mix = mix_ref[...]
    vec = vec_ref[...]
    w0c, a0c = vec[0:1], vec[1:2]
    k_k, k_a, r_k = vec[2:3, :d], vec[3:4, :d], vec[4:5, :d]
    xr, xw, xk, xv, xa, xg = [(h + dx * mix[j:j + 1]).astype(BF16) for j in range(6)]
    r = jnp.dot(xr, wrkv_ref[0], preferred_element_type=F32)
    k = jnp.dot(xk, wrkv_ref[1], preferred_element_type=F32)
    v = jnp.dot(xv, wrkv_ref[2], preferred_element_type=F32)
    wl = jnp.tanh(jnp.dot(xw, w1_ref[...], preferred_element_type=F32))
    w_pre = w0c + _bdot(wl, w2_ref[...])
    lw = (-math.exp(-0.5)) * jax.nn.sigmoid(w_pre)
    al = jnp.dot(xa, a1_ref[...], preferred_element_type=F32)
    a = jax.nn.sigmoid(a0c + _bdot(al, a2_ref[...]))
    gl = jax.nn.sigmoid(jnp.dot(xg, g1_ref[...], preferred_element_type=F32))
    g = _bdot(gl, g2_ref[...])
    kk = k * k_k
    ss = _head_sum(kk * kk)
    kk = kk * lax.rsqrt(jnp.maximum(ss, 1e-24))
    a_f, a_b = a[:, :d], a[:, d:]
    k_bonus = k * (1.0 + (0.5 * (a_f + a_b) - 1.0) * k_a)
    bonus = _head_sum(r * k_bonus * r_k) * v
    k_ref[...] = k
    v_ref[...] = v
    kk_ref[...] = kk
    r_ref[...] = r
    lw0_ref[...] = lw[:, :d]
    lw1_ref[...] = lw[:, d:]
    a0_ref[...] = a_f
    a1o_ref[...] = a_b
    bonus_ref[...] = bonus
    g_ref[...] = g


def _rwkv_prep(h, seq_len, grid_w, p, tm):
    (mix, w_rkv, w1c, w2bd, a1c, a2bd, g1p, g2p, vec) = p
    m, d = h.shape
    halo = GRID_W
    nb = tm // halo
    last = m // halo - 1
    full = lambda a: pl.BlockSpec(a.shape, lambda i: (0,) * a.ndim, pipeline_mode=pl.Buffered(1))
    tok = pl.BlockSpec((tm, d), lambda i: (i, 0))
    kern = functools.partial(_rwkv_prep_kernel, seq_len=seq_len, grid_w=grid_w, halo=halo)
    return pl.pallas_call(
        kern,
        grid=(m // tm,),
        in_specs=[tok,
                  pl.BlockSpec((halo, d), lambda i: (jnp.maximum(i * nb - 1, 0), 0)),
                  pl.BlockSpec((halo, d), lambda i: (jnp.minimum((i + 1) * nb, last), 0)),
                  full(mix), full(w_rkv), full(w1c), full(w2bd), full(a1c), full(a2bd),
                  full(g1p), full(g2p), full(vec)],
        out_specs=[tok] * 10,
        out_shape=[jax.ShapeDtypeStruct((m, d), F32)] * 10,
        compiler_params=_cparams("parallel"),
        name="rwkv_prep",
    )(h, h, h, mix, w_rkv, w1c, w2bd, a1c, a2bd, g1p, g2p, vec)


def _stack_heads(x):
    lane = lax.broadcasted_iota(jnp.int32, x.shape, 1)
    first = lane < RW_HEAD
    return jnp.concatenate([jnp.where(first, x, 0.0), jnp.where(first, 0.0, x)], axis=0)


def _each(f, *lists):
    return [f(*args) for args in zip(*lists)]


def _mm(a, b):
    return jnp.dot(a, b, preferred_element_type=F32)


def _mm_nt(a, b):
    return lax.dot_general(a, b, (((1,), (1,)), ((), ())), preferred_element_type=F32)


def _mm_tn(a, b):
    return lax.dot_general(a, b, (((0,), (0,)), ((), ())), preferred_element_type=F32)


def _to_bf16(x):
    return x.astype(BF16)


def _unit_tri_inverse(l_mats, ri, ci):
    b16 = (ri // 16) == (ci // 16)
    b32 = (ri // 32) == (ci // 32)
    eye = jnp.where(ri == ci, 1.0, 0.0)
    l16 = _each(lambda l: jnp.where(b16, l, 0.0), l_mats)
    l16b = _each(_to_bf16, l16)
    x = _each(lambda l: eye - l, l16)
    p = _each(_mm, l16b, l16b)
    for it in range(3):
        pb = _each(_to_bf16, p)
        x = _each(lambda xx, pp: xx + _mm(xx.astype(BF16), pp), x, pb)
        if it < 2:
            p = _each(_mm, pb, pb)
    for sel in (lambda l: jnp.where(b32 & (~b16), l, 0.0), lambda l: jnp.where(b32, 0.0, l)):
        off = _each(lambda l: sel(l).astype(BF16), l_mats)
        xb = _each(_to_bf16, x)
        t = _each(_mm, off, xb)
        x = _each(lambda xx, xxb, tt: xx - _mm(xxb, tt.astype(BF16)), x, xb, t)
    return x


def _scan_chunks(s, lw, k, v, kk, r, a, k_a, reverse):
    c = lw[0].shape[0]
    n = 2 * c
    kd = _each(lambda k_, a_, ka_: k_ * (1.0 + (a_ - 1.0) * ka_), k, a, k_a)
    ka = _each(lambda kk_, a_: kk_ * a_, kk, a)
    ti = lax.broadcasted_iota(jnp.int32, (c, c), 0)
    tj = lax.broadcasted_iota(jnp.int32, (c, c), 1)
    tri = jnp.where((tj >= ti) if reverse else (tj <= ti), 1.0, 0.0)
    cum = _each(lambda x: jnp.dot(tri, x, preferred_element_type=F32, precision=lax.Precision.HIGHEST), lw)
    tot = _each(lambda x: x[0:1] if reverse else x[c - 1:c], cum)
    e_neg = _each(lambda x: jnp.exp(-x), cum)
    e_end = _each(lambda t_, x: jnp.exp(t_ - x), tot, cum)
    stack_b = lambda x: _stack_heads(x).astype(BF16)
    alpha = _each(lambda kk_, c_, l_: stack_b(kk_ * jnp.exp(c_ - l_)), kk, cum, lw)
    beta = _each(lambda x, e: stack_b(x * e), ka, e_neg)
    kappa = _each(lambda x, e: stack_b(x * e), kd, e_neg)
    rho = _each(lambda r_, c_: _stack_heads(r_ * jnp.exp(c_)), r, cum)
    kappa_e = _each(lambda x, e: stack_b(x * e), kd, e_end)
    beta_e = _each(lambda x, e: stack_b(x * e), ka, e_end)
    v_st = _each(stack_b, v)
    w_end = _each(jnp.exp, tot)

    sc = _each(lambda al, rh, be, kp: _mm_nt(jnp.concatenate([al, rh.astype(BF16)], axis=0),
                                             jnp.concatenate([be, kp], axis=0)), alpha, rho, beta, kappa)
    ri = lax.broadcasted_iota(jnp.int32, (n, n), 0)
    ci = lax.broadcasted_iota(jnp.int32, (n, n), 1)
    strict = (ci > ri) if reverse else (ci < ri)
    incl = (ci >= ri) if reverse else (ci <= ri)
    l_mat = _each(lambda x: jnp.where(strict, x[:n, :n], 0.0), sc)
    ak = _each(lambda x: jnp.where(strict, x[:n, n:], 0.0).astype(BF16), sc)
    pb = _each(lambda x: jnp.where(incl, x[n:, :n], 0.0).astype(BF16), sc)
    pk = _each(lambda x: jnp.where(incl, x[n:, n:], 0.0).astype(BF16), sc)

    akv = _each(_mm, ak, v_st)
    pkv = _each(_mm, pk, v_st)
    x = _each(_to_bf16, _unit_tri_inverse(l_mat, ri, ci))
    uwb = _each(lambda x_, akv_, al: _mm(x_, jnp.concatenate([akv_.astype(BF16), al], axis=1)).astype(BF16),
                x, akv, alpha)
    yr = _each(lambda pkv_, rh, pb_, uw: jnp.concatenate([pkv_, rh], axis=1) - _mm(pb_, uw), pkv, rho, pb, uwb)
    sb = _each(_to_bf16, s)
    y_st = _each(lambda yr_, sb_: yr_[:, :n] + _mm_nt(yr_[:, n:].astype(BF16), sb_), yr, sb)
    y = _each(lambda x_: x_[:c] + x_[c:], y_st)
    n_t = _each(lambda v_, ke, uw, be: _mm_tn(v_, ke) - _mm_tn(uw[:, :n], be), v_st, kappa_e, uwb, beta_e)
    q = _each(lambda uw, be: _mm_tn(uw[:, n:], be).astype(BF16), uwb, beta_e)
    s_new = _each(lambda s_, we, sb_, q_, nt_: s_ * we - _mm(sb_, q_) + nt_, s, w_end, sb, q, n_t)
    return s_new, y


SCAN_SLABS = 8


def _scan_kernel(k_ref, v_ref, kk_ref, r_ref, lw_ref, a_ref, ka_ref, s0_ref, y_ref, sout_ref, s_scr,
                 *, reverse, chunk):
    tb = k_ref.shape[1]
    nslab = s_scr.shape[0]
    nchunk = tb // chunk
    step = pl.program_id(2)

    @pl.when(step == 0)
    def _():
        s_scr[...] = s0_ref[0]

    def body(i, carry):
        ci = (nchunk - 1 - i) if reverse else i
        sl = pl.ds(pl.multiple_of(ci * chunk, chunk), chunk)
        lanes = [slice(g * LANES, (g + 1) * LANES) for g in range(nslab)]
        get = lambda ref: [ref[0, sl, ln] for ln in lanes]
        s_new, y = _scan_chunks([s_scr[g] for g in range(nslab)], get(lw_ref), get(k_ref), get(v_ref),
                                get(kk_ref), get(r_ref), get(a_ref), [ka_ref[:, ln] for ln in lanes],
                                reverse)
        for g in range(nslab):
            y_ref[0, sl, lanes[g]] = y[g]
            s_scr[g] = s_new[g]
        return carry

    lax.fori_loop(0, nchunk, body, 0)

    @pl.when(step == pl.num_programs(2) - 1)
    def _():
        sout_ref[0] = s_scr[...]


def _rwkv_scan(k, v, kk, r, lw, a, k_a, s0, reverse, tb):
    b, t, d = k.shape
    g = SCAN_SLABS
    gw = g * LANES
    nt = t // tb
    tmap = (lambda bi, j, c: (bi, nt - 1 - c, j)) if reverse else (lambda bi, j, c: (bi, c, j))
    tok = pl.BlockSpec((1, tb, gw), tmap)
    st = pl.BlockSpec((1, g, LANES, LANES), lambda bi, j, c: (bi, j, 0, 0))
    kern = functools.partial(_scan_kernel, reverse=reverse, chunk=CHUNK)
    return pl.pallas_call(
        kern,
        grid=(b, d // gw, nt),
        in_specs=[tok] * 6 + [pl.BlockSpec((1, gw), lambda bi, j, c: (0, j)), st],
        out_specs=[tok, st],
        out_shape=[jax.ShapeDtypeStruct((b, t, d), F32),
                   jax.ShapeDtypeStruct((b, d // LANES, LANES, LANES), F32)],
        scratch_shapes=[pltpu.VMEM((g, LANES, LANES), F32)],
        compiler_params=_cparams("parallel", "parallel", "arbitrary"),
        name="rwkv_scan_bwd" if reverse else "rwkv_scan_fwd",
    )(k, v, kk, r, lw, a, k_a, s0)


def _rwkv_out_kernel(x_ref, yf_ref, yb_ref, bonus_ref, g_ref, mod_ref, ln_ref, wo_ref, o_ref):
    y = yf_ref[...] + yb_ref[...]
    ln = ln_ref[...]
    inv_n = 1.0 / RW_HEAD
    mu = _head_sum(y) * inv_n
    yc = y - mu
    var = _head_sum(yc * yc) * inv_n
    yn = yc * lax.rsqrt(var + GN_EPS) * ln[0:1] + ln[1:2]
    z = ((yn + bonus_ref[...]) * g_ref[...]).astype(BF16)
    out = jnp.dot(z, wo_ref[...], preferred_element_type=F32)
    o_ref[...] = x_ref[...] + mod_ref[0][5:6] * out


def _rwkv_out(x, yf, yb, bonus, g, mods, rows_per_mod, ln, w_o, tm):
    m, d = x.shape
    tok = pl.BlockSpec((tm, d), lambda i: (i, 0))
    return pl.pallas_call(
        _rwkv_out_kernel,
        grid=(m // tm,),
        in_specs=[tok] * 5 + [pl.BlockSpec((1, 9, d), lambda i: ((i * tm) // rows_per_mod, 0, 0)),
                              pl.BlockSpec((2, d), lambda i: (0, 0)),
                              pl.BlockSpec(w_o.shape, lambda i: (0, 0), pipeline_mode=pl.Buffered(1))],
        out_specs=tok,
        out_shape=jax.ShapeDtypeStruct((m, d), F32),
        compiler_params=_cparams("parallel"),
        name="rwkv_out",
    )(x, yf, yb, bonus, g, mods, ln, w_o)


def _hy_in_kernel(h_ref, hp_ref, hn_ref, w_ref, cw_ref, cb_ref, x1_ref, x2_ref, v_ref, *, seq_len):
    tm, d = h_ref.shape
    halo = hp_ref.shape[0]
    hext = jnp.concatenate([hp_ref[...], h_ref[...], hn_ref[...]], axis=0).astype(BF16)
    pe = jnp.dot(hext, w_ref[...], preferred_element_type=F32)
    t = (pl.program_id(0) * tm + lax.broadcasted_iota(jnp.int32, (tm, 1), 0)) % seq_len
    cw = cw_ref[...]
    prev = jnp.where(t >= 1, pe[halo - 1:halo - 1 + tm], 0.0)
    nxt = jnp.where(t + 1 < seq_len, pe[halo + 1:halo + 1 + tm], 0.0)
    u = prev * cw[0:1] + pe[halo:halo + tm] * cw[1:2] + nxt * cw[2:3] + cb_ref[...]
    x1_ref[...] = u[:, :d]
    x2_ref[...] = u[:, d:2 * d]
    v_ref[...] = u[:, 2 * d:]


def _hy_in(h, seq_len, w_in, conv_w, conv_b, tm):
    m, d = h.shape
    halo = 8
    nb = tm // halo
    last = m // halo - 1
    tok = pl.BlockSpec((tm, d), lambda i: (i, 0))
    full = lambda a: pl.BlockSpec(a.shape, lambda i: (0,) * a.ndim, pipeline_mode=pl.Buffered(1))
    return pl.pallas_call(
        functools.partial(_hy_in_kernel, seq_len=seq_len),
        grid=(m // tm,),
        in_specs=[tok,
                  pl.BlockSpec((halo, d), lambda i: (jnp.maximum(i * nb - 1, 0), 0)),
                  pl.BlockSpec((halo, d), lambda i: (jnp.minimum((i + 1) * nb, last), 0)),
                  full(w_in), full(conv_w), full(conv_b)],
        out_specs=[tok] * 3,
        out_shape=[jax.ShapeDtypeStruct((m, d), F32)] * 3,
        compiler_params=_cparams("parallel"),
        name="hyena_in",
    )(h, h, h, w_in, conv_w, conv_b)


def _hy_filter_kernel(z_ref, w1_ref, b1_ref, fr_ref, w2_ref, b2_ref, w3_ref, dl_ref, o_ref):
    hp = lax.Precision.HIGHEST
    z = z_ref[...]
    fr = fr_ref[...]
    hid = jnp.sin(fr[0:1] * (jnp.dot(z, w1_ref[...], precision=hp, preferred_element_type=F32) + b1_ref[...]))
    hid = jnp.sin(fr[1:2] * (jnp.dot(hid, w2_ref[...], precision=hp, preferred_element_type=F32) + b2_ref[...]))
    filt = jnp.dot(hid, w3_ref[...], precision=hp, preferred_element_type=F32)
    o_ref[...] = filt * jnp.exp(-z[:, 0:1] * jnp.abs(dl_ref[...]))


def _hy_filters(seq_len, f_w1, f_b1, f_freq, f_w2, f_b2, f_w3, deltas):
    t = jnp.linspace(0.0, 1.0, seq_len, dtype=F32)[:, None]
    ang = ((2 * math.pi / seq_len) * jnp.arange(seq_len, dtype=F32)[:, None]
           * jnp.linspace(1e-4, HY_BANDS - 1, HY_BANDS, dtype=F32)[None])
    z = jnp.concatenate([t, jnp.cos(ang), -jnp.sin(ang)], axis=-1)
    emb, hid = f_w1.shape
    z = jnp.pad(z, ((0, 0), (0, LANES - emb)))
    w1 = jnp.pad(f_w1, ((0, LANES - emb), (0, 0)))
    n_out = f_w3.shape[1]
    tl = min(seq_len, 512)
    full = lambda a: pl.BlockSpec(a.shape, lambda i: (0,) * a.ndim)
    args = (w1, f_b1.reshape(1, hid), f_freq, f_w2, f_b2.reshape(1, hid), f_w3, deltas.reshape(1, n_out))
    return pl.pallas_call(
        _hy_filter_kernel,
        grid=(seq_len // tl,),
        in_specs=[pl.BlockSpec((tl, LANES), lambda i: (i, 0))] + [full(a) for a in args],
        out_specs=pl.BlockSpec((tl, n_out), lambda i: (i, 0)),
        out_shape=jax.ShapeDtypeStruct((seq_len, n_out), F32),
        compiler_params=_cparams("parallel"),
        name="hyena_filters",
    )(z, *args)


def _dft_tables(seq_len):
    big = 2 * seq_len
    n = int(round(math.sqrt(big)))
    assert n * n == big and n % 32 == 0, "sequence length must give a square, tile-aligned DFT"
    k1 = np.arange(n)[None, :, None]
    n1 = np.arange(n // 2)[None, None, :]
    n2 = np.arange(n)[:, None, None]
    ang = -2.0 * np.pi * ((k1 * (n * n1 + n2)) % big) / big
    e_in = np.concatenate([np.cos(ang), np.sin(ang)], axis=1)
    e_out = np.transpose(e_in, (0, 2, 1))
    a2 = -2.0 * np.pi * ((np.arange(n)[:, None] * np.arange(n)[None, :]) % n) / n
    fr, fi = np.cos(a2), np.sin(a2)
    f_fwd = np.block([[fr, -fi], [fi, fr]])
    f_inv = np.block([[fr, fi], [-fi, fr]])
    cast = lambda a: jnp.asarray(a, dtype=F32).astype(BF16)
    return n, cast(e_in), cast(e_out), cast(f_fwd), cast(f_inv)


DFT_GROUP = 8


def _dft_in_kernel(x_ref, e_ref, o_ref, *, dx):
    for j in range(DFT_GROUP):
        sl = slice(j * dx, (j + 1) * dx)
        o_ref[0, :, sl] = jnp.dot(e_ref[j], x_ref[0, :, sl].astype(BF16),
                                  preferred_element_type=F32).astype(o_ref.dtype)


def _dft_in(x, n, e_in):
    b, l, dx = x.shape
    gw = DFT_GROUP * dx
    out = pl.pallas_call(
        functools.partial(_dft_in_kernel, dx=dx),
        grid=(b, n // DFT_GROUP),
        in_specs=[pl.BlockSpec((1, n // 2, gw), lambda bi, j: (bi, 0, j)),
                  pl.BlockSpec((DFT_GROUP, 2 * n, n // 2), lambda bi, j: (j, 0, 0))],
        out_specs=pl.BlockSpec((1, 2 * n, gw), lambda bi, j: (bi, 0, j)),
        out_shape=jax.ShapeDtypeStruct((b, 2 * n, n * dx), BF16),
        compiler_params=_cparams("parallel", "parallel"),
        name="dft_stage1",
    )(x.reshape(b, n // 2, n * dx), e_in)
    return out.reshape(b, 2, n, n, dx)


def _dft_mid_fwd_kernel(a_ref, f_ref, o_ref):
    two, n, td = a_ref.shape[1], a_ref.shape[2], a_ref.shape[3]
    a = a_ref[0].reshape(two * n, td)
    o_ref[...] = jnp.dot(f_ref[...], a, preferred_element_type=F32).reshape(two, n, td)


def _dft_spectrum(a, n, f_fwd):
    dx = a.shape[-1]
    td = min(dx, 1024)
    return pl.pallas_call(
        _dft_mid_fwd_kernel,
        grid=(n, dx // td),
        in_specs=[pl.BlockSpec((1, 2, None, n, td), lambda k, c: (0, 0, k, 0, c)),
                  pl.BlockSpec((2 * n, 2 * n), lambda k, c: (0, 0))],
        out_specs=pl.BlockSpec((2, None, n, td), lambda k, c: (0, k, 0, c)),
        out_shape=jax.ShapeDtypeStruct((2, n, n, dx), F32),
        compiler_params=_cparams("parallel", "parallel"),
        name="dft_filter_spectrum",
    )(a, f_fwd)


def _dft_mid_kernel(a_ref, h_ref, ff_ref, fi_ref, o_ref):
    two, n, td = a_ref.shape[1], a_ref.shape[2], a_ref.shape[3]
    x = jnp.dot(ff_ref[...], a_ref[0].reshape(two * n, td), preferred_element_type=F32)
    xr, xi = x[:n], x[n:]
    hr, hi = h_ref[0], h_ref[1]
    y = jnp.concatenate([xr * hr - xi * hi, xr * hi + xi * hr], axis=0).astype(BF16)
    o_ref[0] = jnp.dot(fi_ref[...], y, preferred_element_type=F32).reshape(two, n, td).astype(o_ref.dtype)


def _dft_mid(a, spec, f_idx, n, f_fwd, f_inv):
    b, dx = a.shape[0], a.shape[-1]
    return pl.pallas_call(
        _dft_mid_kernel,
        grid=(n, b),
        in_specs=[pl.BlockSpec((1, 2, None, n, dx), lambda k, bi: (bi, 0, k, 0, 0)),
                  pl.BlockSpec((2, None, n, dx), lambda k, bi: (0, k, 0, f_idx)),
                  pl.BlockSpec((2 * n, 2 * n), lambda k, bi: (0, 0)),
                  pl.BlockSpec((2 * n, 2 * n), lambda k, bi: (0, 0))],
        out_specs=pl.BlockSpec((1, 2, None, n, dx), lambda k, bi: (bi, 0, k, 0, 0)),
        out_shape=jax.ShapeDtypeStruct(a.shape, BF16),
        compiler_params=_cparams("parallel", "parallel"),
        name="dft_stage2_filter",
    )(a, spec, f_fwd, f_inv)


def _dft_out_kernel(z_ref, e_ref, u_ref, gate_ref, bias_ref, o_ref, *, inv_n, d):
    bias = bias_ref[...]
    for j in range(DFT_GROUP):
        sl = slice(j * d, (j + 1) * d)
        y = jnp.dot(e_ref[j], z_ref[0, :, sl], preferred_element_type=F32) * inv_n
        o_ref[0, :, sl] = gate_ref[0, :, sl] * (y + u_ref[0, :, sl] * bias)


def _dft_out(z, n, e_out, u, gate, bias):
    b, l, d = u.shape
    gw = DFT_GROUP * d
    row = pl.BlockSpec((1, n // 2, gw), lambda bi, j: (bi, 0, j))
    out = pl.pallas_call(
        functools.partial(_dft_out_kernel, inv_n=1.0 / (n * n), d=d),
        grid=(b, n // DFT_GROUP),
        in_specs=[pl.BlockSpec((1, 2 * n, gw), lambda bi, j: (bi, 0, j)),
                  pl.BlockSpec((DFT_GROUP, n // 2, 2 * n), lambda bi, j: (j, 0, 0)),
                  row, row, pl.BlockSpec((1, d), lambda bi, j: (0, 0))],
        out_specs=row,
        out_shape=jax.ShapeDtypeStruct((b, n // 2, n * d), F32),
        compiler_params=_cparams("parallel", "parallel"),
        name="dft_inverse_stage1",
    )(z.reshape(b, 2 * n, n * d), e_out, u.reshape(b, n // 2, n * d), gate.reshape(b, n // 2, n * d),
      bias.reshape(1, d))
    return out.reshape(b, l, d)


def _hyena_conv(u, gate, spec, f_idx, bias, tables):
    n, e_in, e_out, f_fwd, f_inv = tables
    a = _dft_in(u, n, e_in)
    z = _dft_mid(a, spec, f_idx, n, f_fwd, f_inv)
    return _dft_out(z, n, e_out, u, gate, bias)


DFT_UNROLL = 8


def _rdft_tables(seq_len):
    big = 2 * seq_len
    n = int(round(math.sqrt(big)))
    assert n * n == big and n % 32 == 0, "sequence length must give a square, tile-aligned DFT"
    kh = n // 2 + 1
    kp = -(-kh // 8) * 8
    k1 = np.arange(kp)[None, :, None]
    n1 = np.arange(n // 2)[None, None, :]
    n2 = np.arange(n)[:, None, None]
    ang = -2.0 * np.pi * ((k1 * (n * n1 + n2)) % big) / big
    live = (k1 < kh).astype(np.float64)
    e_in = np.concatenate([np.cos(ang) * live, np.sin(ang) * live], axis=1)
    wgt = np.where((k1 == 0) | (k1 == n // 2), 1.0, 2.0) * live
    e_out = np.transpose(np.concatenate([np.cos(ang) * wgt, np.sin(ang) * wgt], axis=1), (0, 2, 1))
    a2 = -2.0 * np.pi * ((np.arange(n)[:, None] * np.arange(n)[None, :]) % n) / n
    fr, fi = np.cos(a2), np.sin(a2)
    f_fwd = np.block([[fr, -fi], [fi, fr]])
    f_inv = np.block([[fr, fi], [-fi, fr]])
    kg = max(g for g in range(1, 21) if kh % g == 0)
    cast = lambda a: jnp.asarray(a, dtype=F32).astype(BF16)
    return dict(n=n, kh=kh, kp=kp, kg=kg, e_in=cast(e_in), e_out=cast(e_out), f_fwd=cast(f_fwd), f_inv=cast(f_inv))


def _rdft_first_stage(u_ref, e_in_ref, a_scr, n, kp):
    def body(i, carry):
        n2s = [i * DFT_UNROLL + j for j in range(DFT_UNROLL)]
        xs = [u_ref[0, pl.ds(n2, n // 2, stride=n), :].astype(BF16) for n2 in n2s]
        acc = [_mm(e_in_ref[n2], x) for n2, x in zip(n2s, xs)]
        for n2, a in zip(n2s, acc):
            row = pl.multiple_of(n2 * kp, 8)
            a_scr[0, pl.ds(row, kp), :] = a[:kp]
            a_scr[1, pl.ds(row, kp), :] = a[kp:]
        return carry

    lax.fori_loop(0, n // DFT_UNROLL, body, 0)


def _rdft_second_stage(a_scr, k1s, n, kp, f_fwd):
    cols = [jnp.concatenate([a_scr[0, pl.ds(k1, n, stride=kp), :], a_scr[1, pl.ds(k1, n, stride=kp), :]],
                            axis=0).astype(BF16) for k1 in k1s]
    return [_mm(f_fwd, c) for c in cols]


def _hy_conv_kernel(u_ref, gate_ref, h_ref, ein_ref, eout_ref, ff_ref, fi_ref, bias_ref, o_ref, a_scr,
                    *, n, kp, kg):
    step = pl.program_id(2)

    @pl.when(step == 0)
    def _():
        _rdft_first_stage(u_ref, ein_ref, a_scr, n, kp)

    f_fwd, f_inv = ff_ref[...], fi_ref[...]
    for j0 in range(0, kg, DFT_UNROLL):
        js = list(range(j0, min(j0 + DFT_UNROLL, kg)))
        k1s = [step * kg + j for j in js]
        xs = _rdft_second_stage(a_scr, k1s, n, kp, f_fwd)
        ys = []
        for j, x in zip(js, xs):
            xr, xi = x[:n], x[n:]
            hr, hi = h_ref[0, j], h_ref[1, j]
            ys.append(jnp.concatenate([xr * hr - xi * hi, xr * hi + xi * hr], axis=0).astype(BF16))
        zs = [_mm(f_inv, y) for y in ys]
        for k1, z in zip(k1s, zs):
            a_scr[0, pl.ds(k1, n, stride=kp), :] = z[:n]
            a_scr[1, pl.ds(k1, n, stride=kp), :] = z[n:]

    @pl.when(step == pl.num_programs(2) - 1)
    def _():
        inv_n = 1.0 / (n * n)
        bias = bias_ref[...]

        def body(i, carry):
            n2s = [i * DFT_UNROLL + j for j in range(DFT_UNROLL)]
            zc = []
            for n2 in n2s:
                row = pl.multiple_of(n2 * kp, 8)
                zc.append(jnp.concatenate([a_scr[0, pl.ds(row, kp), :], a_scr[1, pl.ds(row, kp), :]],
                                          axis=0).astype(BF16))
            ys = [_mm(eout_ref[n2], z) * inv_n for n2, z in zip(n2s, zc)]
            for n2, y in zip(n2s, ys):
                rows = pl.ds(n2, n // 2, stride=n)
                o_ref[0, rows, :] = gate_ref[0, rows, :] * (y + u_ref[0, rows, :] * bias)
            return carry

        lax.fori_loop(0, n // DFT_UNROLL, body, 0)


def _hy_conv(u, gate, spec, f_idx, bias, tb):
    b, l, d = u.shape
    n, kh, kp, kg = tb["n"], tb["kh"], tb["kp"], tb["kg"]
    nd = d // LANES
    tok = lambda bufs: pl.BlockSpec((1, l, LANES), lambda c, bi, g: (bi, 0, c), **bufs)
    once = dict(pipeline_mode=pl.Buffered(1))
    full = lambda a: pl.BlockSpec(a.shape, lambda c, bi, g: (0,) * a.ndim, **once)
    return pl.pallas_call(
        functools.partial(_hy_conv_kernel, n=n, kp=kp, kg=kg),
        grid=(nd, b, kh // kg),
        in_specs=[tok(once), tok(once),
                  pl.BlockSpec((2, kg, n, LANES), lambda c, bi, g: (0, g, 0, f_idx * nd + c)),
                  full(tb["e_in"]), full(tb["e_out"]), full(tb["f_fwd"]), full(tb["f_inv"]),
                  pl.BlockSpec((1, LANES), lambda c, bi, g: (0, c))],
        out_specs=tok({}),
        out_shape=jax.ShapeDtypeStruct((b, l, d), F32),
        scratch_shapes=[pltpu.VMEM((2, n * kp, LANES), F32)],
        compiler_params=_cparams("parallel", "parallel", "arbitrary"),
        name="hyena_conv",
    )(u, gate, spec, tb["e_in"], tb["e_out"], tb["f_fwd"], tb["f_inv"], bias.reshape(1, d))


def _hy_spec_kernel(u_ref, ein_ref, ff_ref, h_ref, a_scr, *, n, kp, kg):
    step = pl.program_id(1)

    @pl.when(step == 0)
    def _():
        _rdft_first_stage(u_ref, ein_ref, a_scr, n, kp)

    f_fwd = ff_ref[...]
    for j0 in range(0, kg, DFT_UNROLL):
        js = list(range(j0, min(j0 + DFT_UNROLL, kg)))
        xs = _rdft_second_stage(a_scr, [step * kg + j for j in js], n, kp, f_fwd)
        for j, x in zip(js, xs):
            h_ref[0, j] = x[:n]
            h_ref[1, j] = x[n:]


def _hy_spectrum(filt, tb):
    _, l, dx = filt.shape
    n, kh, kp, kg = tb["n"], tb["kh"], tb["kp"], tb["kg"]
    full = lambda a: pl.BlockSpec(a.shape, lambda c, g: (0,) * a.ndim, pipeline_mode=pl.Buffered(1))
    return pl.pallas_call(
        functools.partial(_hy_spec_kernel, n=n, kp=kp, kg=kg),
        grid=(dx // LANES, kh // kg),
        in_specs=[pl.BlockSpec((1, l, LANES), lambda c, g: (0, 0, c)), full(tb["e_in"]), full(tb["f_fwd"])],
        out_specs=pl.BlockSpec((2, kg, n, LANES), lambda c, g: (0, g, 0, c)),
        out_shape=jax.ShapeDtypeStruct((2, kh, n, dx), F32),
        scratch_shapes=[pltpu.VMEM((2, n * kp, LANES), F32)],
        compiler_params=_cparams("parallel", "arbitrary"),
        name="hyena_filter_spectrum",
    )(filt, tb["e_in"], tb["f_fwd"])


def _mix_out_kernel(x_ref, z_ref, mod_ref, w_ref, o_ref):
    out = jnp.dot(z_ref[...].astype(BF16), w_ref[...], preferred_element_type=F32)
    o_ref[...] = x_ref[...] + mod_ref[0][5:6] * out


def _mix_out(x, z, mods, rows_per_mod, w_out, tm):
    m, d = x.shape
    tok = pl.BlockSpec((tm, d), lambda i: (i, 0))
    return pl.pallas_call(
        _mix_out_kernel,
        grid=(m // tm,),
        in_specs=[tok, tok, pl.BlockSpec((1, 9, d), lambda i: ((i * tm) // rows_per_mod, 0, 0)),
                  pl.BlockSpec(w_out.shape, lambda i: (0, 0), pipeline_mode=pl.Buffered(1))],
        out_specs=tok,
        out_shape=jax.ShapeDtypeStruct((m, d), F32),
        compiler_params=_cparams("parallel"),
        name="hyena_out",
    )(x, z, mods, w_out)


def _hyena_mixer(hl, batch, hp, tm):
    (w_in, conv_w, conv_b, f_w1, f_b1, f_freq, f_w2, f_b2, f_w3, deltas, bias) = hp
    m, d = hl.shape
    seq_len = m // batch
    x1, x2, v = _hy_in(hl, seq_len, w_in.astype(BF16), conv_w, conv_b.reshape(1, -1), tm)
    three = lambda a: a.reshape(batch, seq_len, d)
    x1, x2, v = three(x1), three(x2), three(v)
    tables = _rdft_tables(seq_len)
    filt = _hy_filters(seq_len, f_w1, f_b1, f_freq, f_w2, f_b2, f_w3, deltas)
    spec = _hy_spectrum(filt[None], tables)
    z = _hy_conv(v, x1, spec, 0, bias[0], tables)
    z = _hy_conv(z, x2, spec, 1, bias[1], tables)
    return z.reshape(m, d)


def _rwkv_weights(mix, w_rkv, w0, w1, w2, a0, a1, a2, g1, g2, k_k, k_a, r_k):
    d = mix.shape[-1]
    lora = w1.shape[-1]
    zero = jnp.zeros((lora, d), F32)
    blockdiag = lambda m: jnp.concatenate([jnp.concatenate([m[0], zero], axis=1),
                                           jnp.concatenate([zero, m[1]], axis=1)], axis=0)
    gl = g1.shape[-1]
    glp = -(-gl // LANES) * LANES
    vec = jnp.zeros((8, 2 * d), F32)
    vec = vec.at[0].set(w0.reshape(-1)).at[1].set(a0.reshape(-1))
    vec = vec.at[2, :d].set(k_k).at[3, :d].set(k_a).at[4, :d].set(r_k.reshape(-1))
    return (mix, w_rkv.astype(BF16),
            jnp.concatenate([w1[0], w1[1]], axis=1).astype(BF16), blockdiag(w2).astype(BF16),
            jnp.concatenate([a1[0], a1[1]], axis=1).astype(BF16), blockdiag(a2).astype(BF16),
            jnp.pad(g1, ((0, 0), (0, glp - gl))).astype(BF16),
            jnp.pad(g2, ((0, glp - gl), (0, 0))).astype(BF16), vec)


def _rwkv_mixer(hl, hc, batch, wts, k_a, tm, tb):
    d = hl.shape[1]
    t_lat, t_ctx = hl.shape[0] // batch, hc.shape[0] // batch
    k_a = k_a.reshape(1, d)
    s0 = jnp.zeros((batch, d // LANES, LANES, LANES), F32)
    three = lambda arrs, t: [a.reshape(batch, t, d) for a in arrs]
    kc, vc, kkc, rc, lw0c, lw1c, a0c, a1c, _, _ = _rwkv_prep(hc, t_ctx, None, wts, min(tm, t_ctx))
    kc, vc, kkc, rc, lw0c, lw1c, a0c, a1c = three((kc, vc, kkc, rc, lw0c, lw1c, a0c, a1c), t_ctx)
    tbc = min(tb, t_ctx)
    _, s_f = _rwkv_scan(kc, vc, kkc, rc, lw0c, a0c, k_a, s0, False, tbc)
    _, s_b = _rwkv_scan(kc, vc, kkc, rc, lw1c, a1c, k_a, s0, True, tbc)
    k, v, kk, r, lw0, lw1, a0, a1, bonus, g = _rwkv_prep(hl, t_lat, GRID_W, wts, tm)
    k, v, kk, r, lw0, lw1, a0, a1 = three((k, v, kk, r, lw0, lw1, a0, a1), t_lat)
    yf, _ = _rwkv_scan(k, v, kk, r, lw0, a0, k_a, s_f, False, tb)
    yb, _ = _rwkv_scan(k, v, kk, r, lw1, a1, k_a, s_b, True, tb)
    return yf.reshape(-1, d), yb.reshape(-1, d), bonus, g


TOKEN_TILE = 512
PREP_TILE = 256
SCAN_BLOCK = 512


def kernel(x, c, ctx, c_ctx, mod_w, mod_b, norm_g, ffn_w_gu, ffn_w_down, rw_mix, rw_w_rkv, rw_w_o, rw_w0, rw_w1, rw_w2, rw_a0, rw_a1, rw_a2, rw_g1, rw_g2, rw_k_k, rw_k_a, rw_r_k, rw_ln_g, rw_ln_b, hy_w_in, hy_conv_w, hy_conv_b, hy_f_w1, hy_f_b1, hy_f_freq, hy_f_w2, hy_f_b2, hy_f_w3, hy_deltas, hy_bias, hy_w_out, final_g):
    batch, seq, d = x.shape
    t_ctx = ctx.shape[1]
    depth = mod_w.shape[0]
    assert depth == 2 and rw_mix.shape[0] == 1 and hy_w_in.shape[0] == 1
    assert seq % TOKEN_TILE == 0 and seq % GRID_W == 0 and t_ctx % CHUNK == 0 and batch + 1 <= 8

    cc = jnp.concatenate([c, c_ctx[None], jnp.zeros((8 - batch - 1, d), F32)], axis=0)
    mods0 = _modulation(cc, mod_w[0], mod_b[0])
    mods1 = _modulation(cc, mod_w[1], mod_b[1])
    wgu = ffn_w_gu.astype(BF16)
    wd = ffn_w_down.astype(BF16)
    xl = x.reshape(batch * seq, d)
    xc = ctx.reshape(batch * t_ctx, d)
    tm = TOKEN_TILE
    tmc = min(tm, t_ctx)

    xl, hl = _ffn(xl, mods0, 0, seq, norm_g[0, 0:2], wgu[0, 0], wd[0, 0], 0, 3, tm)
    _, hc = _ffn(xc, mods0, batch, batch * t_ctx, norm_g[0, 0:2], wgu[0, 0], wd[0, 0], 0, 3, tmc)
    wts = _rwkv_weights(rw_mix[0], rw_w_rkv[0], rw_w0[0], rw_w1[0], rw_w2[0], rw_a0[0], rw_a1[0], rw_a2[0],
                        rw_g1[0], rw_g2[0], rw_k_k[0], rw_k_a[0], rw_r_k[0])
    yf, yb, bonus, g = _rwkv_mixer(hl, hc, batch, wts, rw_k_a[0], PREP_TILE, SCAN_BLOCK)
    xl = _rwkv_out(xl, yf, yb, bonus, g, mods0, seq, jnp.stack([rw_ln_g[0], rw_ln_b[0]]),
                   rw_w_o[0].astype(BF16), tm)
    xl, _ = _ffn(xl, mods0, 0, seq, jnp.stack([norm_g[0, 2], final_g]), wgu[0, 1], wd[0, 1], 6, None, tm)

    xl, hl = _ffn(xl, mods1, 0, seq, norm_g[1, 0:2], wgu[1, 0], wd[1, 0], 0, 3, tm)
    hp = (hy_w_in[0], hy_conv_w[0], hy_conv_b[0], hy_f_w1[0], hy_f_b1[0], hy_f_freq[0], hy_f_w2[0],
          hy_f_b2[0], hy_f_w3[0], hy_deltas[0], hy_bias[0])
    z = _hyena_mixer(hl, batch, hp, tm)
    xl = _mix_out(xl, z, mods1, seq, hy_w_out[0].astype(BF16), tm)
    out, _ = _ffn(xl, mods1, 0, seq, jnp.stack([norm_g[1, 2], final_g]), wgu[1, 1], wd[1, 1], 6, "final", tm)
    return out.reshape(batch, seq, d)
```

```python
import functools
import math

import jax
import jax.numpy as jnp
import numpy as np
from jax import lax
from jax.experimental import pallas as pl
from jax.experimental.pallas import tpu as pltpu

F32 = jnp.float32
BF16 = jnp.bfloat16

GRID_W = 64
RW_HEAD = 64
NORM_EPS = 1e-6
GN_EPS = 64e-5
HY_BANDS = 16

LANES = 128
CHUNK = 64
VMEM_LIMIT = 56 * 1024 * 1024


def _cparams(*sem):
    return pltpu.CompilerParams(dimension_semantics=sem, vmem_limit_bytes=VMEM_LIMIT)


def _bdot(a, b):
    return jnp.dot(a.astype(BF16), b.astype(BF16), preferred_element_type=F32)


def _bdot_nt(a, b):
    return lax.dot_general(a.astype(BF16), b.astype(BF16), (((1,), (1,)), ((), ())),
                           preferred_element_type=F32)


def _bdot_tn(a, b):
    return lax.dot_general(a.astype(BF16), b.astype(BF16), (((0,), (0,)), ((), ())),
                           preferred_element_type=F32)


def _rms_mod(x, g, shift, scale):
    ms = jnp.mean(x * x, axis=-1, keepdims=True)
    return (x * lax.rsqrt(ms + NORM_EPS) * g) * (1.0 + scale) + shift


def _mod_kernel(c_ref, w_ref, b_ref, o_ref):
    c = c_ref[...]
    o_ref[...] = _bdot(c * jax.nn.sigmoid(c), w_ref[...]) + b_ref[...]


def _modulation(cc, w, b):
    m, d = cc.shape
    n = w.shape[1]
    tn = 1152
    out = pl.pallas_call(
        _mod_kernel,
        grid=(n // tn,),
        in_specs=[pl.BlockSpec((m, d), lambda j: (0, 0)),
                  pl.BlockSpec((d, tn), lambda j: (0, j)),
                  pl.BlockSpec((1, tn), lambda j: (0, j))],
        out_specs=pl.BlockSpec((m, tn), lambda j: (0, j)),
        out_shape=jax.ShapeDtypeStruct((m, n), F32),
        compiler_params=_cparams("parallel"),
        name="modulation",
    )(cc, w, b.reshape(1, n))
    return out.reshape(m, 9, d)


def _ffn_kernel(x_ref, mod_ref, g_ref, wgu_ref, wd_ref, *out_refs, mod_off, f_chunk, next_off):
    x = x_ref[...]
    mod = mod_ref[0]
    g = g_ref[...]
    h = _rms_mod(x, g[0:1], mod[mod_off:mod_off + 1], mod[mod_off + 1:mod_off + 2]).astype(BF16)
    f = wd_ref.shape[0]
    acc = jnp.zeros(x.shape, F32)
    for c0 in range(0, f, f_chunk):
        gate = jnp.dot(h, wgu_ref[:, c0:c0 + f_chunk], preferred_element_type=F32)
        up = jnp.dot(h, wgu_ref[:, f + c0:f + c0 + f_chunk], preferred_element_type=F32)
        a = (gate * jax.nn.sigmoid(gate) * up).astype(BF16)
        acc = acc + jnp.dot(a, wd_ref[c0:c0 + f_chunk, :], preferred_element_type=F32)
    xn = x + (0.5 * mod[mod_off + 2:mod_off + 3]) * acc
    if next_off == "final":
        ms = jnp.mean(xn * xn, axis=-1, keepdims=True)
        out_refs[0][...] = xn * lax.rsqrt(ms + NORM_EPS) * g[1:2]
        return
    out_refs[0][...] = xn
    if next_off is not None:
        out_refs[1][...] = _rms_mod(xn, g[1:2], mod[next_off:next_off + 1], mod[next_off + 1:next_off + 2])


def _ffn(x, mods, mod_row0, rows_per_mod, norm_g2, wgu, wd, mod_off, next_off, tm):
    m, d = x.shape
    f = wd.shape[0]
    f_chunk = f // 2 if (f // 2) % LANES == 0 else f
    n_out = 2 if isinstance(next_off, int) else 1
    kern = functools.partial(_ffn_kernel, mod_off=mod_off, f_chunk=f_chunk, next_off=next_off)
    tok = pl.BlockSpec((tm, d), lambda i: (i, 0))
    outs = pl.pallas_call(
        kern,
        grid=(m // tm,),
        in_specs=[tok,
                  pl.BlockSpec((1, 9, d), lambda i: (mod_row0 + (i * tm) // rows_per_mod, 0, 0)),
                  pl.BlockSpec((2, d), lambda i: (0, 0)),
                  pl.BlockSpec(wgu.shape, lambda i: (0, 0), pipeline_mode=pl.Buffered(1)),
                  pl.BlockSpec(wd.shape, lambda i: (0, 0), pipeline_mode=pl.Buffered(1))],
        out_specs=[tok] * n_out,
        out_shape=[jax.ShapeDtypeStruct((m, d), F32)] * n_out,
        compiler_params=_cparams("parallel"),
        name="ffn_halfstep",
    )(x, mods, norm_g2, wgu, wd)
    return outs if n_out == 2 else (outs[0], None)


def _head_sum(x):
    row = lax.broadcasted_iota(jnp.int32, (LANES, LANES), 0) // RW_HEAD
    col = lax.broadcasted_iota(jnp.int32, (LANES, LANES), 1) // RW_HEAD
    ones_bd = jnp.where(row == col, 1.0, 0.0).astype(BF16)
    hi = x.astype(BF16)
    lo = (x - hi.astype(F32)).astype(BF16)
    outs = []
    for j in range(x.shape[1] // LANES):
        sl = slice(j * LANES, (j + 1) * LANES)
        outs.append(jnp.dot(hi[:, sl], ones_bd, preferred_element_type=F32)
                    + jnp.dot(lo[:, sl], ones_bd, preferred_element_type=F32))
    return jnp.concatenate(outs, axis=1)


def _rwkv_prep_kernel(h_ref, hp_ref, hn_ref, mix_ref, wrkv_ref, w1_ref, w2_ref, a1_ref, a2_ref,
                      g1_ref, g2_ref, vec_ref,
                      k_ref, v_ref, kk_ref, r_ref, lw0_ref, lw1_ref, a0_ref, a1o_ref, bonus_ref, g_ref,
                      *, seq_len, grid_w, halo):
    tm, d = h_ref.shape
    q = d // 4
    h = h_ref[...]
    hext = jnp.concatenate([hp_ref[...], h, hn_ref[...]], axis=0)
    t = (pl.program_id(0) * tm + lax.broadcasted_iota(jnp.int32, (tm, 1), 0)) % seq_len
    if grid_w is None:
        offs = (-1, 1, -1, 1)
    else:
        offs = (-1, 1, -grid_w, grid_w)
    parts = []
    for qi, o in enumerate(offs):
        piece = hext[halo + o:halo + o + tm, qi * q:(qi + 1) * q]
        ok = (t + o >= 0) & (t + o < seq_len)
        if grid_w is not None and abs(o) == 1:
            colp = t % grid_w + o
            ok = ok & (colp >= 0) & (colp < grid_w)
        parts.append(jnp.where(ok, piece, 0.0))
    dx = jnp.concatenate(parts, axis=1) - h
    mix = mix_ref[...]
    vec = vec_ref[...]
    w0c, a0c = vec[0:1], vec[1:2]
    k_k, k_a, r_k = vec[2:3, :d], vec[3:4, :d], vec[4:5, :d]
    xr, xw, xk, xv, xa, xg = [(h + dx * mix[j:j + 1]).astype(BF16) for j in range(6)]
    r = jnp.dot(xr, wrkv_ref[0], preferred_element_type=F32)
    k = jnp.dot(xk, wrkv_ref[1], preferred_element_type=F32)
    v = jnp.dot(xv, wrkv_ref[2], preferred_element_type=F32)
    wl = jnp.tanh(jnp.dot(xw, w1_ref[...], preferred_element_type=F32))
    w_pre = w0c + _bdot(wl, w2_ref[...])
    lw = (-math.exp(-0.5)) * jax.nn.sigmoid(w_pre)
    al = jnp.dot(xa, a1_ref[...], preferred_element_type=F32)
    a = jax.nn.sigmoid(a0c + _bdot(al, a2_ref[...]))
    gl = jax.nn.sigmoid(jnp.dot(xg, g1_ref[...], preferred_element_type=F32))
    g = _bdot(gl, g2_ref[...])
    kk = k * k_k
    ss = _head_sum(kk * kk)
    kk = kk * lax.rsqrt(jnp.maximum(ss, 1e-24))
    a_f, a_b = a[:, :d], a[:, d:]
    k_bonus = k * (1.0 + (0.5 * (a_f + a_b) - 1.0) * k_a)
    bonus = _head_sum(r * k_bonus * r_k) * v
    k_ref[...] = k
    v_ref[...] = v
    kk_ref[...] = kk
    r_ref[...] = r
    lw0_ref[...] = lw[:, :d]
    lw1_ref[...] = lw[:, d:]
    a0_ref[...] = a_f
    a1o_ref[...] = a_b
    bonus_ref[...] = bonus
    g_ref[...] = g


def _rwkv_prep(h, seq_len, grid_w, p, tm):
    (mix, w_rkv, w1c, w2bd, a1c, a2bd, g1p, g2p, vec) = p
    m, d = h.shape
    halo = GRID_W
    nb = tm // halo
    last = m // halo - 1
    full = lambda a: pl.BlockSpec(a.shape, lambda i: (0,) * a.ndim, pipeline_mode=pl.Buffered(1))
    tok = pl.BlockSpec((tm, d), lambda i: (i, 0))
    kern = functools.partial(_rwkv_prep_kernel, seq_len=seq_len, grid_w=grid_w, halo=halo)
    return pl.pallas_call(
        kern,
        grid=(m // tm,),
        in_specs=[tok,
                  pl.BlockSpec((halo, d), lambda i: (jnp.maximum(i * nb - 1, 0), 0)),
                  pl.BlockSpec((halo, d), lambda i: (jnp.minimum((i + 1) * nb, last), 0)),
                  full(mix), full(w_rkv), full(w1c), full(w2bd), full(a1c), full(a2bd),
                  full(g1p), full(g2p), full(vec)],
        out_specs=[tok] * 10,
        out_shape=[jax.ShapeDtypeStruct((m, d), F32)] * 10,
        compiler_params=_cparams("parallel"),
        name="rwkv_prep",
    )(h, h, h, mix, w_rkv, w1c, w2bd, a1c, a2bd, g1p, g2p, vec)


def _stack_heads(x):
    lane = lax.broadcasted_iota(jnp.int32, x.shape, 1)
    first = lane < RW_HEAD
    return jnp.concatenate([jnp.where(first, x, 0.0), jnp.where(first, 0.0, x)], axis=0)


def _each(f, *lists):
    return [f(*args) for args in zip(*lists)]


def _mm(a, b):
    return jnp.dot(a, b, preferred_element_type=F32)


def _mm_nt(a, b):
    return lax.dot_general(a, b, (((1,), (1,)), ((), ())), preferred_element_type=F32)


def _mm_tn(a, b):
    return lax.dot_general(a, b, (((0,), (0,)), ((), ())), preferred_element_type=F32)


def _to_bf16(x):
    return x.astype(BF16)


INV_BASE = 16


def _row_blocks(x, size, parity):
    return jnp.concatenate([x[b * size:(b + 1) * size] for b in range(x.shape[0] // size) if b % 2 == parity],
                           axis=0)


def _put_row_blocks(xc, size, parity, base):
    out, k = [], 0
    for b in range(2 * xc.shape[0] // size):
        if b % 2 == parity:
            out.append(xc[k * size:(k + 1) * size])
            k += 1
        else:
            out.append(jnp.zeros((size, xc.shape[1]), xc.dtype) if base is None else base[b * size:(b + 1) * size])
    return jnp.concatenate(out, axis=0)


def _unit_tri_inverse(l_mats, ri, ci, reverse):
    nb = INV_BASE
    n = l_mats[0].shape[0]
    b16 = (ri // nb) == (ci // nb)
    b32 = (ri // (2 * nb)) == (ci // (2 * nb))
    sr = lax.broadcasted_iota(jnp.int32, (nb, n), 0)
    sc = lax.broadcasted_iota(jnp.int32, (nb, n), 1)
    eye_s = jnp.where(sr == sc % nb, 1.0, 0.0)
    strip = lambda m: functools.reduce(lambda a, b: a + b, [m[b * nb:(b + 1) * nb] for b in range(n // nb)])
    block_diag = lambda s: jnp.where(b16, jnp.concatenate([s] * (n // nb), axis=0), 0.0)

    l16 = _each(lambda l: jnp.where(b16, l, 0.0), l_mats)
    l16s = _each(strip, l16)
    xs = _each(lambda s: eye_s - s, l16s)
    ps = _each(lambda s, m: _mm(s.astype(BF16), m.astype(BF16)), l16s, l16)
    for it in range(3):
        pbd = _each(lambda s: block_diag(s).astype(BF16), ps)
        if it < 2:
            both = _each(lambda x_, s, p_: _mm(jnp.concatenate([x_, s], axis=0).astype(BF16), p_), xs, ps, pbd)
            xs = _each(lambda x_, b_: x_ + b_[:nb], xs, both)
            ps = _each(lambda b_: b_[nb:], both)
        else:
            xs = _each(lambda x_, p_: x_ + _mm(x_.astype(BF16), p_), xs, pbd)
    x = _each(block_diag, xs)

    par = 0 if reverse else 1
    for size, off_mask in ((nb, b32 & (~b16)), (2 * nb, ~b32)):
        lc = _each(lambda l: _row_blocks(jnp.where(off_mask, l, 0.0), size, par).astype(BF16), l_mats)
        xb = _each(_to_bf16, x)
        t = _each(lambda l_, x_: _put_row_blocks(_mm(l_, x_).astype(BF16), size, par, None), lc, xb)
        xc = _each(lambda x_: _row_blocks(x_, size, par), x)
        r = _each(lambda x_, t_: x_ - _mm(x_.astype(BF16), t_), xc, t)
        x = _each(lambda r_, x_: _put_row_blocks(r_, size, par, x_), r, x)
    return x


def _chunk_cumsum(x, reverse):
    c = x.shape[0]
    row = lax.broadcasted_iota(jnp.int32, x.shape, 0)
    s = 1
    while s < c:
        if reverse:
            x = x + jnp.where(row < c - s, pltpu.roll(x, c - s, axis=0), 0.0)
        else:
            x = x + jnp.where(row >= s, pltpu.roll(x, s, axis=0), 0.0)
        s *= 2
    return x


def _chunk_local(lw, k, v, kk, r, a, k_a, reverse):
    c = lw[0].shape[0]
    n = 2 * c
    kd = _each(lambda k_, a_, ka_: k_ * (1.0 + (a_ - 1.0) * ka_), k, a, k_a)
    ka = _each(lambda kk_, a_: kk_ * a_, kk, a)
    cum = _each(lambda x: _chunk_cumsum(x, reverse), lw)
    tot = _each(lambda x: x[0:1] if reverse else x[c - 1:c], cum)
    e_neg = _each(lambda x: jnp.exp(-x), cum)
    e_end = _each(lambda t_, x: jnp.exp(t_ - x), tot, cum)
    stack_b = lambda x: _stack_heads(x).astype(BF16)
    alpha = _each(lambda kk_, c_, l_: stack_b(kk_ * jnp.exp(c_ - l_)), kk, cum, lw)
    beta = _each(lambda x, e: stack_b(x * e), ka, e_neg)
    kappa = _each(lambda x, e: stack_b(x * e), kd, e_neg)
    rho = _each(lambda r_, c_: _stack_heads(r_ * jnp.exp(c_)), r, cum)
    kappa_e = _each(lambda x, e: stack_b(x * e), kd, e_end)
    beta_e = _each(lambda x, e: stack_b(x * e), ka, e_end)
    v_st = _each(stack_b, v)
    w_end = _each(jnp.exp, tot)

    sc = _each(lambda al, rh, be, kp: _mm_nt(jnp.concatenate([al, rh.astype(BF16)], axis=0),
                                             jnp.concatenate([be, kp], axis=0)), alpha, rho, beta, kappa)
    ri = lax.broadcasted_iota(jnp.int32, (n, n), 0)
    ci = lax.broadcasted_iota(jnp.int32, (n, n), 1)
    strict = (ci > ri) if reverse else (ci < ri)
    incl = (ci >= ri) if reverse else (ci <= ri)
    l_mat = _each(lambda x: jnp.where(strict, x[:n, :n], 0.0), sc)
    apk = _each(lambda x: jnp.concatenate([jnp.where(strict, x[:n, n:], 0.0), jnp.where(incl, x[n:, n:], 0.0)],
                                          axis=0).astype(BF16), sc)
    pb = _each(lambda x: jnp.where(incl, x[n:, :n], 0.0).astype(BF16), sc)

    apkv = _each(_mm, apk, v_st)
    x = _each(_to_bf16, _unit_tri_inverse(l_mat, ri, ci, reverse))
    uwb = _each(lambda x_, kv, al: _mm(x_, jnp.concatenate([kv[:n].astype(BF16), al], axis=1)).astype(BF16),
                x, apkv, alpha)
    yr = _each(lambda kv, rh, pb_, uw: jnp.concatenate([kv[n:], rh], axis=1) - _mm(pb_, uw), apkv, rho, pb, uwb)
    uq = _each(_mm_tn, uwb, beta_e)
    n_t = _each(lambda v_, ke, uq_: _mm_tn(v_, ke) - uq_[:n], v_st, kappa_e, uq)
    q = _each(lambda uq_: uq_[n:].astype(BF16), uq)
    return list(zip(yr, n_t, q, w_end))


def _chunk_state(s, local):
    n = local[0][0].shape[0]
    c = n // 2
    sb = _each(_to_bf16, s)
    y_st = _each(lambda lc, sb_: lc[0][:, :n] + _mm_nt(lc[0][:, n:].astype(BF16), sb_), local, sb)
    y = _each(lambda x_: x_[:c] + x_[c:], y_st)
    s_new = _each(lambda s_, sb_, lc: s_ * lc[3] - _mm(sb_, lc[2]) + lc[1], s, sb, local)
    return s_new, y


SCAN_SLABS = 8
SCAN_PAIR = 2


def _scan_kernel(k_ref, v_ref, kk_ref, r_ref, lw_ref, a_ref, ka_ref, s0_ref, y_ref, sout_ref, s_scr,
                 *, reverse, chunk):
    tb = k_ref.shape[1]
    nslab = s_scr.shape[0]
    nchunk = tb // chunk
    step = pl.program_id(2)

    @pl.when(step == 0)
    def _():
        s_scr[...] = s0_ref[0]

    lanes = [slice(g * LANES, (g + 1) * LANES) for g in range(nslab)]

    def body(i, carry):
        cis = [(nchunk - 1 - (i * SCAN_PAIR + p)) if reverse else (i * SCAN_PAIR + p) for p in range(SCAN_PAIR)]
        sls = [pl.ds(pl.multiple_of(ci * chunk, chunk), chunk) for ci in cis]
        get = lambda ref: [ref[0, sl, ln] for sl in sls for ln in lanes]
        local = _chunk_local(get(lw_ref), get(k_ref), get(v_ref), get(kk_ref), get(r_ref), get(a_ref),
                             [ka_ref[:, ln] for _ in sls for ln in lanes], reverse)
        s = [s_scr[g] for g in range(nslab)]
        for p, sl in enumerate(sls):
            s, y = _chunk_state(s, local[p * nslab:(p + 1) * nslab])
            for g in range(nslab):
                y_ref[0, sl, lanes[g]] = y[g]
        for g in range(nslab):
            s_scr[g] = s[g]
        return carry

    lax.fori_loop(0, nchunk // SCAN_PAIR, body, 0)

    @pl.when(step == pl.num_programs(2) - 1)
    def _():
        sout_ref[0] = s_scr[...]


def _rwkv_scan(k, v, kk, r, lw, a, k_a, s0, reverse, tb):
    b, t, d = k.shape
    g = SCAN_SLABS
    gw = g * LANES
    nt = t // tb
    tmap = (lambda bi, j, c: (bi, nt - 1 - c, j)) if reverse else (lambda bi, j, c: (bi, c, j))
    tok = pl.BlockSpec((1, tb, gw), tmap)
    st = pl.BlockSpec((1, g, LANES, LANES), lambda bi, j, c: (bi, j, 0, 0))
    kern = functools.partial(_scan_kernel, reverse=reverse, chunk=CHUNK)
    return pl.pallas_call(
        kern,
        grid=(b, d // gw, nt),
        in_specs=[tok] * 6 + [pl.BlockSpec((1, gw), lambda bi, j, c: (0, j)), st],
        out_specs=[tok, st],
        out_shape=[jax.ShapeDtypeStruct((b, t, d), F32),
                   jax.ShapeDtypeStruct((b, d // LANES, LANES, LANES), F32)],
        scratch_shapes=[pltpu.VMEM((g, LANES, LANES), F32)],
        compiler_params=_cparams("parallel", "parallel", "arbitrary"),
        name="rwkv_scan_bwd" if reverse else "rwkv_scan_fwd",
    )(k, v, kk, r, lw, a, k_a, s0)


def _rwkv_out_kernel(x_ref, yf_ref, yb_ref, bonus_ref, g_ref, mod_ref, ln_ref, wo_ref, o_ref):
    y = yf_ref[...] + yb_ref[...]
    ln = ln_ref[...]
    inv_n = 1.0 / RW_HEAD
    mu = _head_sum(y) * inv_n
    yc = y - mu
    var = _head_sum(yc * yc) * inv_n
    yn = yc * lax.rsqrt(var + GN_EPS) * ln[0:1] + ln[1:2]
    z = ((yn + bonus_ref[...]) * g_ref[...]).astype(BF16)
    out = jnp.dot(z, wo_ref[...], preferred_element_type=F32)
    o_ref[...] = x_ref[...] + mod_ref[0][5:6] * out


def _rwkv_out(x, yf, yb, bonus, g, mods, rows_per_mod, ln, w_o, tm):
    m, d = x.shape
    tok = pl.BlockSpec((tm, d), lambda i: (i, 0))
    return pl.pallas_call(
        _rwkv_out_kernel,
        grid=(m // tm,),
        in_specs=[tok] * 5 + [pl.BlockSpec((1, 9, d), lambda i: ((i * tm) // rows_per_mod, 0, 0)),
                              pl.BlockSpec((2, d), lambda i: (0, 0)),
                              pl.BlockSpec(w_o.shape, lambda i: (0, 0), pipeline_mode=pl.Buffered(1))],
        out_specs=tok,
        out_shape=jax.ShapeDtypeStruct((m, d), F32),
        compiler_params=_cparams("parallel"),
        name="rwkv_out",
    )(x, yf, yb, bonus, g, mods, ln, w_o)


def _hy_in_kernel(h_ref, hp_ref, hn_ref, w_ref, cw_ref, cb_ref, x1_ref, x2_ref, v_ref, *, seq_len):
    tm, d = h_ref.shape
    halo = hp_ref.shape[0]
    hext = jnp.concatenate([hp_ref[...], h_ref[...], hn_ref[...]], axis=0).astype(BF16)
    pe = jnp.dot(hext, w_ref[...], preferred_element_type=F32)
    t = (pl.program_id(0) * tm + lax.broadcasted_iota(jnp.int32, (tm, 1), 0)) % seq_len
    cw = cw_ref[...]
    prev = jnp.where(t >= 1, pe[halo - 1:halo - 1 + tm], 0.0)
    nxt = jnp.where(t + 1 < seq_len, pe[halo + 1:halo + 1 + tm], 0.0)
    u = prev * cw[0:1] + pe[halo:halo + tm] * cw[1:2] + nxt * cw[2:3] + cb_ref[...]
    x1_ref[...] = u[:, :d]
    x2_ref[...] = u[:, d:2 * d]
    v_ref[...] = u[:, 2 * d:]


def _hy_in(h, seq_len, w_in, conv_w, conv_b, tm):
    m, d = h.shape
    halo = 8
    nb = tm // halo
    last = m // halo - 1
    tok = pl.BlockSpec((tm, d), lambda i: (i, 0))
    full = lambda a: pl.BlockSpec(a.shape, lambda i: (0,) * a.ndim, pipeline_mode=pl.Buffered(1))
    return pl.pallas_call(
        functools.partial(_hy_in_kernel, seq_len=seq_len),
        grid=(m // tm,),
        in_specs=[tok,
                  pl.BlockSpec((halo, d), lambda i: (jnp.maximum(i * nb - 1, 0), 0)),
                  pl.BlockSpec((halo, d), lambda i: (jnp.minimum((i + 1) * nb, last), 0)),
                  full(w_in), full(conv_w), full(conv_b)],
        out_specs=[tok] * 3,
        out_shape=[jax.ShapeDtypeStruct((m, d), F32)] * 3,
        compiler_params=_cparams("parallel"),
        name="hyena_in",
    )(h, h, h, w_in, conv_w, conv_b)


def _hy_filter_kernel(z_ref, w1_ref, b1_ref, fr_ref, w2_ref, b2_ref, w3_ref, dl_ref, o_ref):
    hp = lax.Precision.HIGHEST
    z = z_ref[...]
    fr = fr_ref[...]
    hid = jnp.sin(fr[0:1] * (jnp.dot(z, w1_ref[...], precision=hp, preferred_element_type=F32) + b1_ref[...]))
    hid = jnp.sin(fr[1:2] * (jnp.dot(hid, w2_ref[...], precision=hp, preferred_element_type=F32) + b2_ref[...]))
    filt = jnp.dot(hid, w3_ref[...], precision=hp, preferred_element_type=F32)
    o_ref[...] = filt * jnp.exp(-z[:, 0:1] * jnp.abs(dl_ref[...]))


def _hy_filters(seq_len, f_w1, f_b1, f_freq, f_w2, f_b2, f_w3, deltas):
    t = jnp.linspace(0.0, 1.0, seq_len, dtype=F32)[:, None]
    ang = ((2 * math.pi / seq_len) * jnp.arange(seq_len, dtype=F32)[:, None]
           * jnp.linspace(1e-4, HY_BANDS - 1, HY_BANDS, dtype=F32)[None])
    z = jnp.concatenate([t, jnp.cos(ang), -jnp.sin(ang)], axis=-1)
    emb, hid = f_w1.shape
    z = jnp.pad(z, ((0, 0), (0, LANES - emb)))
    w1 = jnp.pad(f_w1, ((0, LANES - emb), (0, 0)))
    n_out = f_w3.shape[1]
    tl = min(seq_len, 512)
    full = lambda a: pl.BlockSpec(a.shape, lambda i: (0,) * a.ndim)
    args = (w1, f_b1.reshape(1, hid), f_freq, f_w2, f_b2.reshape(1, hid), f_w3, deltas.reshape(1, n_out))
    return pl.pallas_call(
        _hy_filter_kernel,
        grid=(seq_len // tl,),
        in_specs=[pl.BlockSpec((tl, LANES), lambda i: (i, 0))] + [full(a) for a in args],
        out_specs=pl.BlockSpec((tl, n_out), lambda i: (i, 0)),
        out_shape=jax.ShapeDtypeStruct((seq_len, n_out), F32),
        compiler_params=_cparams("parallel"),
        name="hyena_filters",
    )(z, *args)


def _dft_tables(seq_len):
    big = 2 * seq_len
    n = int(round(math.sqrt(big)))
    assert n * n == big and n % 32 == 0, "sequence length must give a square, tile-aligned DFT"
    k1 = np.arange(n)[None, :, None]
    n1 = np.arange(n // 2)[None, None, :]
    n2 = np.arange(n)[:, None, None]
    ang = -2.0 * np.pi * ((k1 * (n * n1 + n2)) % big) / big
    e_in = np.concatenate([np.cos(ang), np.sin(ang)], axis=1)
    e_out = np.transpose(e_in, (0, 2, 1))
    a2 = -2.0 * np.pi * ((np.arange(n)[:, None] * np.arange(n)[None, :]) % n) / n
    fr, fi = np.cos(a2), np.sin(a2)
    f_fwd = np.block([[fr, -fi], [fi, fr]])
    f_inv = np.block([[fr, fi], [-fi, fr]])
    cast = lambda a: jnp.asarray(a, dtype=F32).astype(BF16)
    return n, cast(e_in), cast(e_out), cast(f_fwd), cast(f_inv)


DFT_GROUP = 8


def _dft_in_kernel(x_ref, e_ref, o_ref, *, dx):
    for j in range(DFT_GROUP):
        sl = slice(j * dx, (j + 1) * dx)
        o_ref[0, :, sl] = jnp.dot(e_ref[j], x_ref[0, :, sl].astype(BF16),
                                  preferred_element_type=F32).astype(o_ref.dtype)


def _dft_in(x, n, e_in):
    b, l, dx = x.shape
    gw = DFT_GROUP * dx
    out = pl.pallas_call(
        functools.partial(_dft_in_kernel, dx=dx),
        grid=(b, n // DFT_GROUP),
        in_specs=[pl.BlockSpec((1, n // 2, gw), lambda bi, j: (bi, 0, j)),
                  pl.BlockSpec((DFT_GROUP, 2 * n, n // 2), lambda bi, j: (j, 0, 0))],
        out_specs=pl.BlockSpec((1, 2 * n, gw), lambda bi, j: (bi, 0, j)),
        out_shape=jax.ShapeDtypeStruct((b, 2 * n, n * dx), BF16),
        compiler_params=_cparams("parallel", "parallel"),
        name="dft_stage1",
    )(x.reshape(b, n // 2, n * dx), e_in)
    return out.reshape(b, 2, n, n, dx)


def _dft_mid_fwd_kernel(a_ref, f_ref, o_ref):
    two, n, td = a_ref.shape[1], a_ref.shape[2], a_ref.shape[3]
    a = a_ref[0].reshape(two * n, td)
    o_ref[...] = jnp.dot(f_ref[...], a, preferred_element_type=F32).reshape(two, n, td)


def _dft_spectrum(a, n, f_fwd):
    dx = a.shape[-1]
    td = min(dx, 1024)
    return pl.pallas_call(
        _dft_mid_fwd_kernel,
        grid=(n, dx // td),
        in_specs=[pl.BlockSpec((1, 2, None, n, td), lambda k, c: (0, 0, k, 0, c)),
                  pl.BlockSpec((2 * n, 2 * n), lambda k, c: (0, 0))],
        out_specs=pl.BlockSpec((2, None, n, td), lambda k, c: (0, k, 0, c)),
        out_shape=jax.ShapeDtypeStruct((2, n, n, dx), F32),
        compiler_params=_cparams("parallel", "parallel"),
        name="dft_filter_spectrum",
    )(a, f_fwd)


def _dft_mid_kernel(a_ref, h_ref, ff_ref, fi_ref, o_ref):
    two, n, td = a_ref.shape[1], a_ref.shape[2], a_ref.shape[3]
    x = jnp.dot(ff_ref[...], a_ref[0].reshape(two * n, td), preferred_element_type=F32)
    xr, xi = x[:n], x[n:]
    hr, hi = h_ref[0], h_ref[1]
    y = jnp.concatenate([xr * hr - xi * hi, xr * hi + xi * hr], axis=0).astype(BF16)
    o_ref[0] = jnp.dot(fi_ref[...], y, preferred_element_type=F32).reshape(two, n, td).astype(o_ref.dtype)


def _dft_mid(a, spec, f_idx, n, f_fwd, f_inv):
    b, dx = a.shape[0], a.shape[-1]
    return pl.pallas_call(
        _dft_mid_kernel,
        grid=(n, b),
        in_specs=[pl.BlockSpec((1, 2, None, n, dx), lambda k, bi: (bi, 0, k, 0, 0)),
                  pl.BlockSpec((2, None, n, dx), lambda k, bi: (0, k, 0, f_idx)),
                  pl.BlockSpec((2 * n, 2 * n), lambda k, bi: (0, 0)),
                  pl.BlockSpec((2 * n, 2 * n), lambda k, bi: (0, 0))],
        out_specs=pl.BlockSpec((1, 2, None, n, dx), lambda k, bi: (bi, 0, k, 0, 0)),
        out_shape=jax.ShapeDtypeStruct(a.shape, BF16),
        compiler_params=_cparams("parallel", "parallel"),
        name="dft_stage2_filter",
    )(a, spec, f_fwd, f_inv)


def _dft_out_kernel(z_ref, e_ref, u_ref, gate_ref, bias_ref, o_ref, *, inv_n, d):
    bias = bias_ref[...]
    for j in range(DFT_GROUP):
        sl = slice(j * d, (j + 1) * d)
        y = jnp.dot(e_ref[j], z_ref[0, :, sl], preferred_element_type=F32) * inv_n
        o_ref[0, :, sl] = gate_ref[0, :, sl] * (y + u_ref[0, :, sl] * bias)


def _dft_out(z, n, e_out, u, gate, bias):
    b, l, d = u.shape
    gw = DFT_GROUP * d
    row = pl.BlockSpec((1, n // 2, gw), lambda bi, j: (bi, 0, j))
    out = pl.pallas_call(
        functools.partial(_dft_out_kernel, inv_n=1.0 / (n * n), d=d),
        grid=(b, n // DFT_GROUP),
        in_specs=[pl.BlockSpec((1, 2 * n, gw), lambda bi, j: (bi, 0, j)),
                  pl.BlockSpec((DFT_GROUP, n // 2, 2 * n), lambda bi, j: (j, 0, 0)),
                  row, row, pl.BlockSpec((1, d), lambda bi, j: (0, 0))],
        out_specs=row,
        out_shape=jax.ShapeDtypeStruct((b, n // 2, n * d), F32),
        compiler_params=_cparams("parallel", "parallel"),
        name="dft_inverse_stage1",
    )(z.reshape(b, 2 * n, n * d), e_out, u.reshape(b, n // 2, n * d), gate.reshape(b, n // 2, n * d),
      bias.reshape(1, d))
    return out.reshape(b, l, d)


def _hyena_conv(u, gate, spec, f_idx, bias, tables):
    n, e_in, e_out, f_fwd, f_inv = tables
    a = _dft_in(u, n, e_in)
    z = _dft_mid(a, spec, f_idx, n, f_fwd, f_inv)
    return _dft_out(z, n, e_out, u, gate, bias)


DFT_UNROLL = 8


def _rdft_tables(seq_len):
    big = 2 * seq_len
    n = int(round(math.sqrt(big)))
    assert n * n == big and n % 32 == 0, "sequence length must give a square, tile-aligned DFT"
    kh = n // 2 + 1
    kp = -(-kh // 8) * 8
    k1 = np.arange(kp)[None, :, None]
    n1 = np.arange(n // 2)[None, None, :]
    n2 = np.arange(n)[:, None, None]
    ang = -2.0 * np.pi * ((k1 * (n * n1 + n2)) % big) / big
    live = (k1 < kh).astype(np.float64)
    e_in = np.concatenate([np.cos(ang) * live, np.sin(ang) * live], axis=1)
    wgt = np.where((k1 == 0) | (k1 == n // 2), 1.0, 2.0) * live
    e_out = np.transpose(np.concatenate([np.cos(ang) * wgt, np.sin(ang) * wgt], axis=1), (0, 2, 1))
    a2 = -2.0 * np.pi * ((np.arange(n)[:, None] * np.arange(n)[None, :]) % n) / n
    fr, fi = np.cos(a2), np.sin(a2)
    f_fwd = np.block([[fr, -fi], [fi, fr]])
    f_inv = np.block([[fr, fi], [-fi, fr]])
    kg = max(g for g in range(1, 21) if kh % g == 0)
    cast = lambda a: jnp.asarray(a, dtype=F32).astype(BF16)
    return dict(n=n, kh=kh, kp=kp, kg=kg, e_in=cast(e_in), e_out=cast(e_out), f_fwd=cast(f_fwd), f_inv=cast(f_inv))


def _rdft_first_stage(u_ref, e_in_ref, a_scr, n, kp):
    def body(i, carry):
        n2s = [i * DFT_UNROLL + j for j in range(DFT_UNROLL)]
        xs = [u_ref[0, pl.ds(n2, n // 2, stride=n), :].astype(BF16) for n2 in n2s]
        acc = [_mm(e_in_ref[n2], x) for n2, x in zip(n2s, xs)]
        for n2, a in zip(n2s, acc):
            row = pl.multiple_of(n2 * kp, 8)
            a_scr[0, pl.ds(row, kp), :] = a[:kp]
            a_scr[1, pl.ds(row, kp), :] = a[kp:]
        return carry

    lax.fori_loop(0, n // DFT_UNROLL, body, 0)


def _rdft_second_stage(a_scr, k1s, n, kp, f_fwd):
    cols = [jnp.concatenate([a_scr[0, pl.ds(k1, n, stride=kp), :], a_scr[1, pl.ds(k1, n, stride=kp), :]],
                            axis=0).astype(BF16) for k1 in k1s]
    return [_mm(f_fwd, c) for c in cols]


def _hy_conv_kernel(u_ref, gate_ref, h_ref, ein_ref, eout_ref, ff_ref, fi_ref, bias_ref, o_ref, a_scr,
                    *, n, kp, kg):
    step = pl.program_id(2)

    @pl.when(step == 0)
    def _():
        _rdft_first_stage(u_ref, ein_ref, a_scr, n, kp)

    f_fwd, f_inv = ff_ref[...], fi_ref[...]
    for j0 in range(0, kg, DFT_UNROLL):
        js = list(range(j0, min(j0 + DFT_UNROLL, kg)))
        k1s = [step * kg + j for j in js]
        xs = _rdft_second_stage(a_scr, k1s, n, kp, f_fwd)
        ys = []
        for j, x in zip(js, xs):
            xr, xi = x[:n], x[n:]
            hr, hi = h_ref[0, j], h_ref[1, j]
            ys.append(jnp.concatenate([xr * hr - xi * hi, xr * hi + xi * hr], axis=0).astype(BF16))
        zs = [_mm(f_inv, y) for y in ys]
        for k1, z in zip(k1s, zs):
            a_scr[0, pl.ds(k1, n, stride=kp), :] = z[:n]
            a_scr[1, pl.ds(k1, n, stride=kp), :] = z[n:]

    @pl.when(step == pl.num_programs(2) - 1)
    def _():
        inv_n = 1.0 / (n * n)
        bias = bias_ref[...]

        def body(i, carry):
            n2s = [i * DFT_UNROLL + j for j in range(DFT_UNROLL)]
            zc = []
            for n2 in n2s:
                row = pl.multiple_of(n2 * kp, 8)
                zc.append(jnp.concatenate([a_scr[0, pl.ds(row, kp), :], a_scr[1, pl.ds(row, kp), :]],
                                          axis=0).astype(BF16))
            ys = [_mm(eout_ref[n2], z) * inv_n for n2, z in zip(n2s, zc)]
            for n2, y in zip(n2s, ys):
                rows = pl.ds(n2, n // 2, stride=n)
                o_ref[0, rows, :] = gate_ref[0, rows, :] * (y + u_ref[0, rows, :] * bias)
            return carry

        lax.fori_loop(0, n // DFT_UNROLL, body, 0)


def _hy_conv(u, gate, spec, f_idx, bias, tb):
    b, l, d = u.shape
    n, kh, kp, kg = tb["n"], tb["kh"], tb["kp"], tb["kg"]
    nd = d // LANES
    tok = lambda bufs: pl.BlockSpec((1, l, LANES), lambda c, bi, g: (bi, 0, c), **bufs)
    once = dict(pipeline_mode=pl.Buffered(1))
    full = lambda a: pl.BlockSpec(a.shape, lambda c, bi, g: (0,) * a.ndim, **once)
    return pl.pallas_call(
        functools.partial(_hy_conv_kernel, n=n, kp=kp, kg=kg),
        grid=(nd, b, kh // kg),
        in_specs=[tok(once), tok(once),
                  pl.BlockSpec((2, kg, n, LANES), lambda c, bi, g: (0, g, 0, f_idx * nd + c)),
                  full(tb["e_in"]), full(tb["e_out"]), full(tb["f_fwd"]), full(tb["f_inv"]),
                  pl.BlockSpec((1, LANES), lambda c, bi, g: (0, c))],
        out_specs=tok({}),
        out_shape=jax.ShapeDtypeStruct((b, l, d), F32),
        scratch_shapes=[pltpu.VMEM((2, n * kp, LANES), F32)],
        compiler_params=_cparams("parallel", "parallel", "arbitrary"),
        name="hyena_conv",
    )(u, gate, spec, tb["e_in"], tb["e_out"], tb["f_fwd"], tb["f_inv"], bias.reshape(1, d))


def _hy_spec_kernel(u_ref, ein_ref, ff_ref, h_ref, a_scr, *, n, kp, kg):
    step = pl.program_id(1)

    @pl.when(step == 0)
    def _():
        _rdft_first_stage(u_ref, ein_ref, a_scr, n, kp)

    f_fwd = ff_ref[...]
    for j0 in range(0, kg, DFT_UNROLL):
        js = list(range(j0, min(j0 + DFT_UNROLL, kg)))
        xs = _rdft_second_stage(a_scr, [step * kg + j for j in js], n, kp, f_fwd)
        for j, x in zip(js, xs):
            h_ref[0, j] = x[:n]
            h_ref[1, j] = x[n:]


def _hy_spectrum(filt, tb):
    _, l, dx = filt.shape
    n, kh, kp, kg = tb["n"], tb["kh"], tb["kp"], tb["kg"]
    full = lambda a: pl.BlockSpec(a.shape, lambda c, g: (0,) * a.ndim, pipeline_mode=pl.Buffered(1))
    return pl.pallas_call(
        functools.partial(_hy_spec_kernel, n=n, kp=kp, kg=kg),
        grid=(dx // LANES, kh // kg),
        in_specs=[pl.BlockSpec((1, l, LANES), lambda c, g: (0, 0, c)), full(tb["e_in"]), full(tb["f_fwd"])],
        out_specs=pl.BlockSpec((2, kg, n, LANES), lambda c, g: (0, g, 0, c)),
        out_shape=jax.ShapeDtypeStruct((2, kh, n, dx), F32),
        scratch_shapes=[pltpu.VMEM((2, n * kp, LANES), F32)],
        compiler_params=_cparams("parallel", "arbitrary"),
        name="hyena_filter_spectrum",
    )(filt, tb["e_in"], tb["f_fwd"])


def _mix_out_kernel(x_ref, z_ref, mod_ref, w_ref, o_ref):
    out = jnp.dot(z_ref[...].astype(BF16), w_ref[...], preferred_element_type=F32)
    o_ref[...] = x_ref[...] + mod_ref[0][5:6] * out


def _mix_out(x, z, mods, rows_per_mod, w_out, tm):
    m, d = x.shape
    tok = pl.BlockSpec((tm, d), lambda i: (i, 0))
    return pl.pallas_call(
        _mix_out_kernel,
        grid=(m // tm,),
        in_specs=[tok, tok, pl.BlockSpec((1, 9, d), lambda i: ((i * tm) // rows_per_mod, 0, 0)),
                  pl.BlockSpec(w_out.shape, lambda i: (0, 0), pipeline_mode=pl.Buffered(1))],
        out_specs=tok,
        out_shape=jax.ShapeDtypeStruct((m, d), F32),
        compiler_params=_cparams("parallel"),
        name="hyena_out",
    )(x, z, mods, w_out)


def _hyena_mixer(hl, batch, hp, tm):
    (w_in, conv_w, conv_b, f_w1, f_b1, f_freq, f_w2, f_b2, f_w3, deltas, bias) = hp
    m, d = hl.shape
    seq_len = m // batch
    x1, x2, v = _hy_in(hl, seq_len, w_in.astype(BF16), conv_w, conv_b.reshape(1, -1), tm)
    three = lambda a: a.reshape(batch, seq_len, d)
    x1, x2, v = three(x1), three(x2), three(v)
    tables = _rdft_tables(seq_len)
    filt = _hy_filters(seq_len, f_w1, f_b1, f_freq, f_w2, f_b2, f_w3, deltas)
    spec = _hy_spectrum(filt[None], tables)
    z = _hy_conv(v, x1, spec, 0, bias[0], tables)
    z = _hy_conv(z, x2, spec, 1, bias[1], tables)
    return z.reshape(m, d)


def _rwkv_weights(mix, w_rkv, w0, w1, w2, a0, a1, a2, g1, g2, k_k, k_a, r_k):
    d = mix.shape[-1]
    lora = w1.shape[-1]
    zero = jnp.zeros((lora, d), F32)
    blockdiag = lambda m: jnp.concatenate([jnp.concatenate([m[0], zero], axis=1),
                                           jnp.concatenate([zero, m[1]], axis=1)], axis=0)
    gl = g1.shape[-1]
    glp = -(-gl // LANES) * LANES
    vec = jnp.zeros((8, 2 * d), F32)
    vec = vec.at[0].set(w0.reshape(-1)).at[1].set(a0.reshape(-1))
    vec = vec.at[2, :d].set(k_k).at[3, :d].set(k_a).at[4, :d].set(r_k.reshape(-1))
    return (mix, w_rkv.astype(BF16),
            jnp.concatenate([w1[0], w1[1]], axis=1).astype(BF16), blockdiag(w2).astype(BF16),
            jnp.concatenate([a1[0], a1[1]], axis=1).astype(BF16), blockdiag(a2).astype(BF16),
            jnp.pad(g1, ((0, 0), (0, glp - gl))).astype(BF16),
            jnp.pad(g2, ((0, glp - gl), (0, 0))).astype(BF16), vec)


def _rwkv_mixer(hl, hc, batch, wts, k_a, tm, tb):
    d = hl.shape[1]
    t_lat, t_ctx = hl.shape[0] // batch, hc.shape[0] // batch
    k_a = k_a.reshape(1, d)
    s0 = jnp.zeros((batch, d // LANES, LANES, LANES), F32)
    three = lambda arrs, t: [a.reshape(batch, t, d) for a in arrs]
    kc, vc, kkc, rc, lw0c, lw1c, a0c, a1c, _, _ = _rwkv_prep(hc, t_ctx, None, wts, min(tm, t_ctx))
    kc, vc, kkc, rc, lw0c, lw1c, a0c, a1c = three((kc, vc, kkc, rc, lw0c, lw1c, a0c, a1c), t_ctx)
    tbc = min(tb, t_ctx)
    _, s_f = _rwkv_scan(kc, vc, kkc, rc, lw0c, a0c, k_a, s0, False, tbc)
    _, s_b = _rwkv_scan(kc, vc, kkc, rc, lw1c, a1c, k_a, s0, True, tbc)
    k, v, kk, r, lw0, lw1, a0, a1, bonus, g = _rwkv_prep(hl, t_lat, GRID_W, wts, tm)
    k, v, kk, r, lw0, lw1, a0, a1 = three((k, v, kk, r, lw0, lw1, a0, a1), t_lat)
    yf, _ = _rwkv_scan(k, v, kk, r, lw0, a0, k_a, s_f, False, tb)
    yb, _ = _rwkv_scan(k, v, kk, r, lw1, a1, k_a, s_b, True, tb)
    return yf.reshape(-1, d), yb.reshape(-1, d), bonus, g


TOKEN_TILE = 512
PREP_TILE = 256
SCAN_BLOCK = 512


def kernel(x, c, ctx, c_ctx, mod_w, mod_b, norm_g, ffn_w_gu, ffn_w_down, rw_mix, rw_w_rkv, rw_w_o, rw_w0, rw_w1, rw_w2, rw_a0, rw_a1, rw_a2, rw_g1, rw_g2, rw_k_k, rw_k_a, rw_r_k, rw_ln_g, rw_ln_b, hy_w_in, hy_conv_w, hy_conv_b, hy_f_w1, hy_f_b1, hy_f_freq, hy_f_w2, hy_f_b2, hy_f_w3, hy_deltas, hy_bias, hy_w_out, final_g):
    batch, seq, d = x.shape
    t_ctx = ctx.shape[1]
    depth = mod_w.shape[0]
    assert depth == 2 and rw_mix.shape[0] == 1 and hy_w_in.shape[0] == 1
    assert seq % TOKEN_TILE == 0 and seq % GRID_W == 0 and t_ctx % CHUNK == 0 and batch + 1 <= 8

    cc = jnp.concatenate([c, c_ctx[None], jnp.zeros((8 - batch - 1, d), F32)], axis=0)
    mods0 = _modulation(cc, mod_w[0], mod_b[0])
    mods1 = _modulation(cc, mod_w[1], mod_b[1])
    wgu = ffn_w_gu.astype(BF16)
    wd = ffn_w_down.astype(BF16)
    xl = x.reshape(batch * seq, d)
    xc = ctx.reshape(batch * t_ctx, d)
    tm = TOKEN_TILE
    tmc = min(tm, t_ctx)

    xl, hl = _ffn(xl, mods0, 0, seq, norm_g[0, 0:2], wgu[0, 0], wd[0, 0], 0, 3, tm)
    _, hc = _ffn(xc, mods0, batch, batch * t_ctx, norm_g[0, 0:2], wgu[0, 0], wd[0, 0], 0, 3, tmc)
    wts = _rwkv_weights(rw_mix[0], rw_w_rkv[0], rw_w0[0], rw_w1[0], rw_w2[0], rw_a0[0], rw_a1[0], rw_a2[0],
                        rw_g1[0], rw_g2[0], rw_k_k[0], rw_k_a[0], rw_r_k[0])
    yf, yb, bonus, g = _rwkv_mixer(hl, hc, batch, wts, rw_k_a[0], PREP_TILE, SCAN_BLOCK)
    xl = _rwkv_out(xl, yf, yb, bonus, g, mods0, seq, jnp.stack([rw_ln_g[0], rw_ln_b[0]]),
                   rw_w_o[0].astype(BF16), tm)
    xl, _ = _ffn(xl, mods0, 0, seq, jnp.stack([norm_g[0, 2], final_g]), wgu[0, 1], wd[0, 1], 6, None, tm)

    xl, hl = _ffn(xl, mods1, 0, seq, norm_g[1, 0:2], wgu[1, 0], wd[1, 0], 0, 3, tm)
    hp = (hy_w_in[0], hy_conv_w[0], hy_conv_b[0], hy_f_w1[0], hy_f_b1[0], hy_f_freq[0], hy_f_w2[0],
          hy_f_b2[0], hy_f_w3[0], hy_deltas[0], hy_bias[0])
    z = _hyena_mixer(hl, batch, hp, tm)
    xl = _mix_out(xl, z, mods1, seq, hy_w_out[0].astype(BF16), tm)
    out, _ = _ffn(xl, mods1, 0, seq, jnp.stack([norm_g[1, 2], final_g]), wgu[1, 1], wd[1, 1], 6, "final", tm)
    return out.reshape(batch, seq, d)
```

```python
import functools
import math

import jax
import jax.numpy as jnp
import numpy as np
from jax import lax
from jax.experimental import pallas as pl
from jax.experimental.pallas import tpu as pltpu

F32 = jnp.float32
BF16 = jnp.bfloat16

GRID_W = 64
RW_HEAD = 64
NORM_EPS = 1e-6
GN_EPS = 64e-5
HY_BANDS = 16

LANES = 128
CHUNK = 64
VMEM_LIMIT = 56 * 1024 * 1024


def _cparams(*sem):
    return pltpu.CompilerParams(dimension_semantics=sem, vmem_limit_bytes=VMEM_LIMIT)


def _bdot(a, b):
    return jnp.dot(a.astype(BF16), b.astype(BF16), preferred_element_type=F32)


def _bdot_nt(a, b):
    return lax.dot_general(a.astype(BF16), b.astype(BF16), (((1,), (1,)), ((), ())),
                           preferred_element_type=F32)


def _bdot_tn(a, b):
    return lax.dot_general(a.astype(BF16), b.astype(BF16), (((0,), (0,)), ((), ())),
                           preferred_element_type=F32)


def _rms_mod(x, g, shift, scale):
    ms = jnp.mean(x * x, axis=-1, keepdims=True)
    return (x * lax.rsqrt(ms + NORM_EPS) * g) * (1.0 + scale) + shift


def _mod_kernel(c_ref, w_ref, b_ref, o_ref):
    c = c_ref[...]
    o_ref[...] = _bdot(c * jax.nn.sigmoid(c), w_ref[...]) + b_ref[...]


def _modulation(cc, w, b):
    m, d = cc.shape
    n = w.shape[1]
    tn = 1152
    out = pl.pallas_call(
        _mod_kernel,
        grid=(n // tn,),
        in_specs=[pl.BlockSpec((m, d), lambda j: (0, 0)),
                  pl.BlockSpec((d, tn), lambda j: (0, j)),
                  pl.BlockSpec((1, tn), lambda j: (0, j))],
        out_specs=pl.BlockSpec((m, tn), lambda j: (0, j)),
        out_shape=jax.ShapeDtypeStruct((m, n), F32),
        compiler_params=_cparams("parallel"),
        name="modulation",
    )(cc, w, b.reshape(1, n))
    return out.reshape(m, 9, d)


def _ffn_kernel(x_ref, mod_ref, g_ref, wgu_ref, wd_ref, *out_refs, mod_off, f_chunk, next_off):
    tm = x_ref.shape[0]
    mod = mod_ref[0]
    g = g_ref[...]
    f = wd_ref.shape[0]
    nh = 2 if tm % 32 == 0 else 1
    rows = [slice(i * tm // nh, (i + 1) * tm // nh) for i in range(nh)]
    xs, hs, gu, accs = [None] * nh, [None] * nh, [None] * nh, [None] * nh
    for c0 in range(0, f, f_chunk):
        for i in range(nh):
            if c0 == 0:
                xs[i] = x_ref[rows[i], :]
                hs[i] = _rms_mod(xs[i], g[0:1], mod[mod_off:mod_off + 1], mod[mod_off + 1:mod_off + 2]).astype(BF16)
            gu[i] = (jnp.dot(hs[i], wgu_ref[:, c0:c0 + f_chunk], preferred_element_type=F32),
                     jnp.dot(hs[i], wgu_ref[:, f + c0:f + c0 + f_chunk], preferred_element_type=F32))
        for i in range(nh):
            gate, up = gu[i]
            a = (gate * jax.nn.sigmoid(gate) * up).astype(BF16)
            down = jnp.dot(a, wd_ref[c0:c0 + f_chunk, :], preferred_element_type=F32)
            accs[i] = down if c0 == 0 else accs[i] + down
    for i in range(nh):
        xn = xs[i] + (0.5 * mod[mod_off + 2:mod_off + 3]) * accs[i]
        if next_off == "final":
            ms = jnp.mean(xn * xn, axis=-1, keepdims=True)
            out_refs[0][rows[i], :] = xn * lax.rsqrt(ms + NORM_EPS) * g[1:2]
            continue
        out_refs[0][rows[i], :] = xn
        if next_off is not None:
            out_refs[1][rows[i], :] = _rms_mod(xn, g[1:2], mod[next_off:next_off + 1],
                                               mod[next_off + 1:next_off + 2])


def _ffn(x, mods, mod_row0, rows_per_mod, norm_g2, wgu, wd, mod_off, next_off, tm):
    m, d = x.shape
    f = wd.shape[0]
    f_chunk = f // 2 if (f // 2) % LANES == 0 else f
    n_out = 2 if isinstance(next_off, int) else 1
    kern = functools.partial(_ffn_kernel, mod_off=mod_off, f_chunk=f_chunk, next_off=next_off)
    tok = pl.BlockSpec((tm, d), lambda i: (i, 0))
    outs = pl.pallas_call(
        kern,
        grid=(m // tm,),
        in_specs=[tok,
                  pl.BlockSpec((1, 9, d), lambda i: (mod_row0 + (i * tm) // rows_per_mod, 0, 0)),
                  pl.BlockSpec((2, d), lambda i: (0, 0)),
                  pl.BlockSpec(wgu.shape, lambda i: (0, 0), pipeline_mode=pl.Buffered(1)),
                  pl.BlockSpec(wd.shape, lambda i: (0, 0), pipeline_mode=pl.Buffered(1))],
        out_specs=[tok] * n_out,
        out_shape=[jax.ShapeDtypeStruct((m, d), F32)] * n_out,
        compiler_params=_cparams("parallel"),
        name="ffn_halfstep",
    )(x, mods, norm_g2, wgu, wd)
    return outs if n_out == 2 else (outs[0], None)


def _head_sum(x):
    row = lax.broadcasted_iota(jnp.int32, (LANES, LANES), 0) // RW_HEAD
    col = lax.broadcasted_iota(jnp.int32, (LANES, LANES), 1) // RW_HEAD
    ones_bd = jnp.where(row == col, 1.0, 0.0).astype(BF16)
    hi = x.astype(BF16)
    lo = (x - hi.astype(F32)).astype(BF16)
    outs = []
    for j in range(x.shape[1] // LANES):
        sl = slice(j * LANES, (j + 1) * LANES)
        outs.append(jnp.dot(hi[:, sl], ones_bd, preferred_element_type=F32)
                    + jnp.dot(lo[:, sl], ones_bd, preferred_element_type=F32))
    return jnp.concatenate(outs, axis=1)


def _rwkv_prep_kernel(h_ref, hp_ref, hn_ref, mix_ref, wrkv_ref, w1_ref, w2_ref, a1_ref, a2_ref,
                      g1_ref, g2_ref, vec_ref,
                      k_ref, v_ref, kk_ref, r_ref, lw0_ref, lw1_ref, a0_ref, a1o_ref, bonus_ref, g_ref,
                      *, seq_len, grid_w, halo):
    tm, d = h_ref.shape
    q = d // 4
    h = h_ref[...]
    hext = jnp.concatenate([hp_ref[...], h, hn_ref[...]], axis=0)
    t = (pl.program_id(0) * tm + lax.broadcasted_iota(jnp.int32, (tm, 1), 0)) % seq_len
    if grid_w is None:
        offs = (-1, 1, -1, 1)
    else:
        offs = (-1, 1, -grid_w, grid_w)
    parts = []
    for qi, o in enumerate(offs):
        piece = hext[halo + o:halo + o + tm, qi * q:(qi + 1) * q]
        ok = (t + o >= 0) & (t + o < seq_len)
        if grid_w is not None and abs(o) == 1:
            colp = t % grid_w + o
            ok = ok & (colp >= 0) & (colp < grid_w)
        parts.append(jnp.where(ok, piece, 0.0))
    dx = jnp.concatenate(parts, axis=1) - h
    mix = mix_ref[...]
    vec = vec_ref[...]
    w0c, a0c = vec[0:1], vec[1:2]
    k_k, k_a, r_k = vec[2:3, :d], vec[3:4, :d], vec[4:5, :d]
    xr, xw, xk, xv, xa, xg = [(h + dx * mix[j:j + 1]).astype(BF16) for j in range(6)]
    r = jnp.dot(xr, wrkv_ref[0], preferred_element_type=F32)
    k = jnp.dot(xk, wrkv_ref[1], preferred_element_type=F32)
    v = jnp.dot(xv, wrkv_ref[2], preferred_element_type=F32)
    wl = jnp.tanh(jnp.dot(xw, w1_ref[...], preferred_element_type=F32))
    w_pre = w0c + _bdot(wl, w2_ref[...])
    lw = (-math.exp(-0.5)) * jax.nn.sigmoid(w_pre)
    al = jnp.dot(xa, a1_ref[...], preferred_element_type=F32)
    a = jax.nn.sigmoid(a0c + _bdot(al, a2_ref[...]))
    gl = jax.nn.sigmoid(jnp.dot(xg, g1_ref[...], preferred_element_type=F32))
    g = _bdot(gl, g2_ref[...])
    kk = k * k_k
    ss = _head_sum(kk * kk)
    kk = kk * lax.rsqrt(jnp.maximum(ss, 1e-24))
    a_f, a_b = a[:, :d], a[:, d:]
    k_bonus = k * (1.0 + (0.5 * (a_f + a_b) - 1.0) * k_a)
    bonus = _head_sum(r * k_bonus * r_k) * v
    k_ref[...] = k
    v_ref[...] = v
    kk_ref[...] = kk
    r_ref[...] = r
    lw0_ref[...] = lw[:, :d]
    lw1_ref[...] = lw[:, d:]
    a0_ref[...] = a_f
    a1o_ref[...] = a_b
    bonus_ref[...] = bonus
    g_ref[...] = g.astype(g_ref.dtype)


def _rwkv_prep(h, seq_len, grid_w, p, tm):
    (mix, w_rkv, w1c, w2bd, a1c, a2bd, g1p, g2p, vec) = p
    m, d = h.shape
    halo = GRID_W
    nb = tm // halo
    last = m // halo - 1
    full = lambda a: pl.BlockSpec(a.shape, lambda i: (0,) * a.ndim, pipeline_mode=pl.Buffered(1))
    tok = pl.BlockSpec((tm, d), lambda i: (i, 0))
    kern = functools.partial(_rwkv_prep_kernel, seq_len=seq_len, grid_w=grid_w, halo=halo)
    return pl.pallas_call(
        kern,
        grid=(m // tm,),
        in_specs=[tok,
                  pl.BlockSpec((halo, d), lambda i: (jnp.maximum(i * nb - 1, 0), 0)),
                  pl.BlockSpec((halo, d), lambda i: (jnp.minimum((i + 1) * nb, last), 0)),
                  full(mix), full(w_rkv), full(w1c), full(w2bd), full(a1c), full(a2bd),
                  full(g1p), full(g2p), full(vec)],
        out_specs=[tok] * 10,
        out_shape=[jax.ShapeDtypeStruct((m, d), F32)] * 9 + [jax.ShapeDtypeStruct((m, d), BF16)],
        compiler_params=_cparams("parallel"),
        name="rwkv_prep",
    )(h, h, h, mix, w_rkv, w1c, w2bd, a1c, a2bd, g1p, g2p, vec)


def _stack_heads(x):
    lane = lax.broadcasted_iota(jnp.int32, x.shape, 1)
    first = lane < RW_HEAD
    return jnp.concatenate([jnp.where(first, x, 0.0), jnp.where(first, 0.0, x)], axis=0)


def _each(f, *lists):
    return [f(*args) for args in zip(*lists)]


def _mm(a, b):
    return jnp.dot(a, b, preferred_element_type=F32)


def _mm_nt(a, b):
    return lax.dot_general(a, b, (((1,), (1,)), ((), ())), preferred_element_type=F32)


def _mm_tn(a, b):
    return lax.dot_general(a, b, (((0,), (0,)), ((), ())), preferred_element_type=F32)


def _to_bf16(x):
    return x.astype(BF16)


INV_BASE = 16


def _row_blocks(x, size, parity):
    return jnp.concatenate([x[b * size:(b + 1) * size] for b in range(x.shape[0] // size) if b % 2 == parity],
                           axis=0)


def _put_row_blocks(xc, size, parity, base):
    out, k = [], 0
    for b in range(2 * xc.shape[0] // size):
        if b % 2 == parity:
            out.append(xc[k * size:(k + 1) * size])
            k += 1
        else:
            out.append(jnp.zeros((size, xc.shape[1]), xc.dtype) if base is None else base[b * size:(b + 1) * size])
    return jnp.concatenate(out, axis=0)


def _unit_tri_inverse(l_mats, ri, ci, reverse):
    nb = INV_BASE
    n = l_mats[0].shape[0]
    b16 = (ri // nb) == (ci // nb)
    b32 = (ri // (2 * nb)) == (ci // (2 * nb))
    sr = lax.broadcasted_iota(jnp.int32, (nb, n), 0)
    sc = lax.broadcasted_iota(jnp.int32, (nb, n), 1)
    eye_s = jnp.where(sr == sc % nb, 1.0, 0.0)
    strip = lambda m: functools.reduce(lambda a, b: a + b, [m[b * nb:(b + 1) * nb] for b in range(n // nb)])
    block_diag = lambda s: jnp.where(b16, jnp.concatenate([s] * (n // nb), axis=0), 0.0)

    l16 = _each(lambda l: jnp.where(b16, l, 0.0), l_mats)
    l16s = _each(strip, l16)
    xs = _each(lambda s: eye_s - s, l16s)
    ps = _each(lambda s, m: _mm(s.astype(BF16), m.astype(BF16)), l16s, l16)
    for it in range(3):
        pbd = _each(lambda s: block_diag(s).astype(BF16), ps)
        if it < 2:
            both = _each(lambda x_, s, p_: _mm(jnp.concatenate([x_, s], axis=0).astype(BF16), p_), xs, ps, pbd)
            xs = _each(lambda x_, b_: x_ + b_[:nb], xs, both)
            ps = _each(lambda b_: b_[nb:], both)
        else:
            xs = _each(lambda x_, p_: x_ + _mm(x_.astype(BF16), p_), xs, pbd)
    x = _each(block_diag, xs)

    par = 0 if reverse else 1
    for size, off_mask in ((nb, b32 & (~b16)), (2 * nb, ~b32)):
        lc = _each(lambda l: _row_blocks(jnp.where(off_mask, l, 0.0), size, par).astype(BF16), l_mats)
        xb = _each(_to_bf16, x)
        t = _each(lambda l_, x_: _put_row_blocks(_mm(l_, x_).astype(BF16), size, par, None), lc, xb)
        xc = _each(lambda x_: _row_blocks(x_, size, par), x)
        r = _each(lambda x_, t_: x_ - _mm(x_.astype(BF16), t_), xc, t)
        x = _each(lambda r_, x_: _put_row_blocks(r_, size, par, x_), r, x)
    return x


def _chunk_cumsum(x, reverse):
    c = x.shape[0]
    row = lax.broadcasted_iota(jnp.int32, x.shape, 0)
    s = 1
    while s < c:
        if reverse:
            x = x + jnp.where(row < c - s, pltpu.roll(x, c - s, axis=0), 0.0)
        else:
            x = x + jnp.where(row >= s, pltpu.roll(x, s, axis=0), 0.0)
        s *= 2
    return x


def _chunk_local(lw, k, v, kk, r, a, k_a, reverse):
    c = lw[0].shape[0]
    n = 2 * c
    kd = _each(lambda k_, a_, ka_: k_ * (1.0 + (a_ - 1.0) * ka_), k, a, k_a)
    ka = _each(lambda kk_, a_: kk_ * a_, kk, a)
    cum = _each(lambda x: _chunk_cumsum(x, reverse), lw)
    tot = _each(lambda x: x[0:1] if reverse else x[c - 1:c], cum)
    e_neg = _each(lambda x: jnp.exp(-x), cum)
    e_end = _each(lambda t_, x: jnp.exp(t_ - x), tot, cum)
    stack_b = lambda x: _stack_heads(x).astype(BF16)
    alpha = _each(lambda kk_, c_, l_: stack_b(kk_ * jnp.exp(c_ - l_)), kk, cum, lw)
    beta = _each(lambda x, e: stack_b(x * e), ka, e_neg)
    kappa = _each(lambda x, e: stack_b(x * e), kd, e_neg)
    rho = _each(lambda r_, c_: _stack_heads(r_ * jnp.exp(c_)), r, cum)
    kappa_e = _each(lambda x, e: stack_b(x * e), kd, e_end)
    beta_e = _each(lambda x, e: stack_b(x * e), ka, e_end)
    v_st = _each(stack_b, v)
    w_end = _each(jnp.exp, tot)

    sc = _each(lambda al, rh, be, kp: _mm_nt(jnp.concatenate([al, rh.astype(BF16)], axis=0),
                                             jnp.concatenate([be, kp], axis=0)), alpha, rho, beta, kappa)
    ri = lax.broadcasted_iota(jnp.int32, (n, n), 0)
    ci = lax.broadcasted_iota(jnp.int32, (n, n), 1)
    strict = (ci > ri) if reverse else (ci < ri)
    incl = (ci >= ri) if reverse else (ci <= ri)
    l_mat = _each(lambda x: jnp.where(strict, x[:n, :n], 0.0), sc)
    apk = _each(lambda x: jnp.concatenate([jnp.where(strict, x[:n, n:], 0.0), jnp.where(incl, x[n:, n:], 0.0)],
                                          axis=0).astype(BF16), sc)
    pb = _each(lambda x: jnp.where(incl, x[n:, :n], 0.0).astype(BF16), sc)

    apkv = _each(_mm, apk, v_st)
    x = _each(_to_bf16, _unit_tri_inverse(l_mat, ri, ci, reverse))
    uwb = _each(lambda x_, kv, al: _mm(x_, jnp.concatenate([kv[:n].astype(BF16), al], axis=1)).astype(BF16),
                x, apkv, alpha)
    yr = _each(lambda kv, rh, pb_, uw: jnp.concatenate([kv[n:], rh], axis=1) - _mm(pb_, uw), apkv, rho, pb, uwb)
    uq = _each(_mm_tn, uwb, beta_e)
    n_t = _each(lambda v_, ke, uq_: _mm_tn(v_, ke) - uq_[:n], v_st, kappa_e, uq)
    q = _each(lambda uq_: uq_[n:].astype(BF16), uq)
    return list(zip(yr, n_t, q, w_end))


def _chunk_state(s, local):
    n = local[0][0].shape[0]
    c = n // 2
    sb = _each(_to_bf16, s)
    y_st = _each(lambda lc, sb_: lc[0][:, :n] + _mm_nt(lc[0][:, n:].astype(BF16), sb_), local, sb)
    y = _each(lambda x_: x_[:c] + x_[c:], y_st)
    s_new = _each(lambda s_, sb_, lc: s_ * lc[3] - _mm(sb_, lc[2]) + lc[1], s, sb, local)
    return s_new, y


SCAN_SLABS = 8
SCAN_PAIR = 2


def _scan_kernel(k_ref, v_ref, kk_ref, r_ref, lw_ref, a_ref, ka_ref, s0_ref, y_ref, sout_ref, s_scr,
                 *, reverse, chunk):
    tb = k_ref.shape[1]
    nslab = s_scr.shape[0]
    nchunk = tb // chunk
    step = pl.program_id(2)

    @pl.when(step == 0)
    def _():
        s_scr[...] = s0_ref[0]

    lanes = [slice(g * LANES, (g + 1) * LANES) for g in range(nslab)]

    def body(i, carry):
        cis = [(nchunk - 1 - (i * SCAN_PAIR + p)) if reverse else (i * SCAN_PAIR + p) for p in range(SCAN_PAIR)]
        sls = [pl.ds(pl.multiple_of(ci * chunk, chunk), chunk) for ci in cis]
        get = lambda ref: [ref[0, sl, ln] for sl in sls for ln in lanes]
        local = _chunk_local(get(lw_ref), get(k_ref), get(v_ref), get(kk_ref), get(r_ref), get(a_ref),
                             [ka_ref[:, ln] for _ in sls for ln in lanes], reverse)
        s = [s_scr[g] for g in range(nslab)]
        for p, sl in enumerate(sls):
            s, y = _chunk_state(s, local[p * nslab:(p + 1) * nslab])
            for g in range(nslab):
                y_ref[0, sl, lanes[g]] = y[g]
        for g in range(nslab):
            s_scr[g] = s[g]
        return carry

    lax.fori_loop(0, nchunk // SCAN_PAIR, body, 0)

    @pl.when(step == pl.num_programs(2) - 1)
    def _():
        sout_ref[0] = s_scr[...]


def _rwkv_scan(k, v, kk, r, lw, a, k_a, s0, reverse, tb):
    b, t, d = k.shape
    g = SCAN_SLABS
    gw = g * LANES
    nt = t // tb
    tmap = (lambda bi, j, c: (bi, nt - 1 - c, j)) if reverse else (lambda bi, j, c: (bi, c, j))
    tok = pl.BlockSpec((1, tb, gw), tmap)
    st = pl.BlockSpec((1, g, LANES, LANES), lambda bi, j, c: (bi, j, 0, 0))
    kern = functools.partial(_scan_kernel, reverse=reverse, chunk=CHUNK)
    return pl.pallas_call(
        kern,
        grid=(b, d // gw, nt),
        in_specs=[tok] * 6 + [pl.BlockSpec((1, gw), lambda bi, j, c: (0, j)), st],
        out_specs=[tok, st],
        out_shape=[jax.ShapeDtypeStruct((b, t, d), F32),
                   jax.ShapeDtypeStruct((b, d // LANES, LANES, LANES), F32)],
        scratch_shapes=[pltpu.VMEM((g, LANES, LANES), F32)],
        compiler_params=_cparams("parallel", "parallel", "arbitrary"),
        name="rwkv_scan_bwd" if reverse else "rwkv_scan_fwd",
    )(k, v, kk, r, lw, a, k_a, s0)


def _rwkv_out_kernel(x_ref, yf_ref, yb_ref, bonus_ref, g_ref, mod_ref, ln_ref, wo_ref, o_ref):
    y = yf_ref[...] + yb_ref[...]
    ln = ln_ref[...]
    inv_n = 1.0 / RW_HEAD
    mu = _head_sum(y) * inv_n
    yc = y - mu
    var = _head_sum(yc * yc) * inv_n
    yn = yc * lax.rsqrt(var + GN_EPS) * ln[0:1] + ln[1:2]
    z = ((yn + bonus_ref[...]) * g_ref[...]).astype(BF16)
    out = jnp.dot(z, wo_ref[...], preferred_element_type=F32)
    o_ref[...] = x_ref[...] + mod_ref[0][5:6] * out


def _rwkv_out(x, yf, yb, bonus, g, mods, rows_per_mod, ln, w_o, tm):
    m, d = x.shape
    tok = pl.BlockSpec((tm, d), lambda i: (i, 0))
    return pl.pallas_call(
        _rwkv_out_kernel,
        grid=(m // tm,),
        in_specs=[tok] * 5 + [pl.BlockSpec((1, 9, d), lambda i: ((i * tm) // rows_per_mod, 0, 0)),
                              pl.BlockSpec((2, d), lambda i: (0, 0)),
                              pl.BlockSpec(w_o.shape, lambda i: (0, 0), pipeline_mode=pl.Buffered(1))],
        out_specs=tok,
        out_shape=jax.ShapeDtypeStruct((m, d), F32),
        compiler_params=_cparams("parallel"),
        name="rwkv_out",
    )(x, yf, yb, bonus, g, mods, ln, w_o)


def _hy_in_kernel(h_ref, hp_ref, hn_ref, w_ref, cw_ref, cb_ref, x1_ref, x2_ref, v_ref, *, seq_len):
    tm, d = h_ref.shape
    halo = hp_ref.shape[0]
    hext = jnp.concatenate([hp_ref[...], h_ref[...], hn_ref[...]], axis=0).astype(BF16)
    pe = jnp.dot(hext, w_ref[...], preferred_element_type=F32)
    t = (pl.program_id(0) * tm + lax.broadcasted_iota(jnp.int32, (tm, 1), 0)) % seq_len
    cw = cw_ref[...]
    prev = jnp.where(t >= 1, pe[halo - 1:halo - 1 + tm], 0.0)
    nxt = jnp.where(t + 1 < seq_len, pe[halo + 1:halo + 1 + tm], 0.0)
    u = prev * cw[0:1] + pe[halo:halo + tm] * cw[1:2] + nxt * cw[2:3] + cb_ref[...]
    x1_ref[...] = u[:, :d]
    x2_ref[...] = u[:, d:2 * d]
    v_ref[...] = u[:, 2 * d:]


def _hy_in(h, seq_len, w_in, conv_w, conv_b, tm):
    m, d = h.shape
    halo = 8
    nb = tm // halo
    last = m // halo - 1
    tok = pl.BlockSpec((tm, d), lambda i: (i, 0))
    full = lambda a: pl.BlockSpec(a.shape, lambda i: (0,) * a.ndim, pipeline_mode=pl.Buffered(1))
    return pl.pallas_call(
        functools.partial(_hy_in_kernel, seq_len=seq_len),
        grid=(m // tm,),
        in_specs=[tok,
                  pl.BlockSpec((halo, d), lambda i: (jnp.maximum(i * nb - 1, 0), 0)),
                  pl.BlockSpec((halo, d), lambda i: (jnp.minimum((i + 1) * nb, last), 0)),
                  full(w_in), full(conv_w), full(conv_b)],
        out_specs=[tok] * 3,
        out_shape=[jax.ShapeDtypeStruct((m, d), F32)] * 3,
        compiler_params=_cparams("parallel"),
        name="hyena_in",
    )(h, h, h, w_in, conv_w, conv_b)


def _hy_filter_kernel(z_ref, w1_ref, b1_ref, fr_ref, w2_ref, b2_ref, w3_ref, dl_ref, o_ref):
    hp = lax.Precision.HIGHEST
    z = z_ref[...]
    fr = fr_ref[...]
    hid = jnp.sin(fr[0:1] * (jnp.dot(z, w1_ref[...], precision=hp, preferred_element_type=F32) + b1_ref[...]))
    hid = jnp.sin(fr[1:2] * (jnp.dot(hid, w2_ref[...], precision=hp, preferred_element_type=F32) + b2_ref[...]))
    filt = jnp.dot(hid, w3_ref[...], precision=hp, preferred_element_type=F32)
    o_ref[...] = filt * jnp.exp(-z[:, 0:1] * jnp.abs(dl_ref[...]))


def _hy_filters(seq_len, f_w1, f_b1, f_freq, f_w2, f_b2, f_w3, deltas):
    t = jnp.linspace(0.0, 1.0, seq_len, dtype=F32)[:, None]
    ang = ((2 * math.pi / seq_len) * jnp.arange(seq_len, dtype=F32)[:, None]
           * jnp.linspace(1e-4, HY_BANDS - 1, HY_BANDS, dtype=F32)[None])
    z = jnp.concatenate([t, jnp.cos(ang), -jnp.sin(ang)], axis=-1)
    emb, hid = f_w1.shape
    z = jnp.pad(z, ((0, 0), (0, LANES - emb)))
    w1 = jnp.pad(f_w1, ((0, LANES - emb), (0, 0)))
    n_out = f_w3.shape[1]
    tl = min(seq_len, 512)
    full = lambda a: pl.BlockSpec(a.shape, lambda i: (0,) * a.ndim)
    args = (w1, f_b1.reshape(1, hid), f_freq, f_w2, f_b2.reshape(1, hid), f_w3, deltas.reshape(1, n_out))
    return pl.pallas_call(
        _hy_filter_kernel,
        grid=(seq_len // tl,),
        in_specs=[pl.BlockSpec((tl, LANES), lambda i: (i, 0))] + [full(a) for a in args],
        out_specs=pl.BlockSpec((tl, n_out), lambda i: (i, 0)),
        out_shape=jax.ShapeDtypeStruct((seq_len, n_out), F32),
        compiler_params=_cparams("parallel"),
        name="hyena_filters",
    )(z, *args)


def _dft_tables(seq_len):
    big = 2 * seq_len
    n = int(round(math.sqrt(big)))
    assert n * n == big and n % 32 == 0, "sequence length must give a square, tile-aligned DFT"
    k1 = np.arange(n)[None, :, None]
    n1 = np.arange(n // 2)[None, None, :]
    n2 = np.arange(n)[:, None, None]
    ang = -2.0 * np.pi * ((k1 * (n * n1 + n2)) % big) / big
    e_in = np.concatenate([np.cos(ang), np.sin(ang)], axis=1)
    e_out = np.transpose(e_in, (0, 2, 1))
    a2 = -2.0 * np.pi * ((np.arange(n)[:, None] * np.arange(n)[None, :]) % n) / n
    fr, fi = np.cos(a2), np.sin(a2)
    f_fwd = np.block([[fr, -fi], [fi, fr]])
    f_inv = np.block([[fr, fi], [-fi, fr]])
    cast = lambda a: jnp.asarray(a, dtype=F32).astype(BF16)
    return n, cast(e_in), cast(e_out), cast(f_fwd), cast(f_inv)


DFT_GROUP = 8


def _dft_in_kernel(x_ref, e_ref, o_ref, *, dx):
    for j in range(DFT_GROUP):
        sl = slice(j * dx, (j + 1) * dx)
        o_ref[0, :, sl] = jnp.dot(e_ref[j], x_ref[0, :, sl].astype(BF16),
                                  preferred_element_type=F32).astype(o_ref.dtype)


def _dft_in(x, n, e_in):
    b, l, dx = x.shape
    gw = DFT_GROUP * dx
    out = pl.pallas_call(
        functools.partial(_dft_in_kernel, dx=dx),
        grid=(b, n // DFT_GROUP),
        in_specs=[pl.BlockSpec((1, n // 2, gw), lambda bi, j: (bi, 0, j)),
                  pl.BlockSpec((DFT_GROUP, 2 * n, n // 2), lambda bi, j: (j, 0, 0))],
        out_specs=pl.BlockSpec((1, 2 * n, gw), lambda bi, j: (bi, 0, j)),
        out_shape=jax.ShapeDtypeStruct((b, 2 * n, n * dx), BF16),
        compiler_params=_cparams("parallel", "parallel"),
        name="dft_stage1",
    )(x.reshape(b, n // 2, n * dx), e_in)
    return out.reshape(b, 2, n, n, dx)


def _dft_mid_fwd_kernel(a_ref, f_ref, o_ref):
    two, n, td = a_ref.shape[1], a_ref.shape[2], a_ref.shape[3]
    a = a_ref[0].reshape(two * n, td)
    o_ref[...] = jnp.dot(f_ref[...], a, preferred_element_type=F32).reshape(two, n, td)


def _dft_spectrum(a, n, f_fwd):
    dx = a.shape[-1]
    td = min(dx, 1024)
    return pl.pallas_call(
        _dft_mid_fwd_kernel,
        grid=(n, dx // td),
        in_specs=[pl.BlockSpec((1, 2, None, n, td), lambda k, c: (0, 0, k, 0, c)),
                  pl.BlockSpec((2 * n, 2 * n), lambda k, c: (0, 0))],
        out_specs=pl.BlockSpec((2, None, n, td), lambda k, c: (0, k, 0, c)),
        out_shape=jax.ShapeDtypeStruct((2, n, n, dx), F32),
        compiler_params=_cparams("parallel", "parallel"),
        name="dft_filter_spectrum",
    )(a, f_fwd)


def _dft_mid_kernel(a_ref, h_ref, ff_ref, fi_ref, o_ref):
    two, n, td = a_ref.shape[1], a_ref.shape[2], a_ref.shape[3]
    x = jnp.dot(ff_ref[...], a_ref[0].reshape(two * n, td), preferred_element_type=F32)
    xr, xi = x[:n], x[n:]
    hr, hi = h_ref[0], h_ref[1]
    y = jnp.concatenate([xr * hr - xi * hi, xr * hi + xi * hr], axis=0).astype(BF16)
    o_ref[0] = jnp.dot(fi_ref[...], y, preferred_element_type=F32).reshape(two, n, td).astype(o_ref.dtype)


def _dft_mid(a, spec, f_idx, n, f_fwd, f_inv):
    b, dx = a.shape[0], a.shape[-1]
    return pl.pallas_call(
        _dft_mid_kernel,
        grid=(n, b),
        in_specs=[pl.BlockSpec((1, 2, None, n, dx), lambda k, bi: (bi, 0, k, 0, 0)),
                  pl.BlockSpec((2, None, n, dx), lambda k, bi: (0, k, 0, f_idx)),
                  pl.BlockSpec((2 * n, 2 * n), lambda k, bi: (0, 0)),
                  pl.BlockSpec((2 * n, 2 * n), lambda k, bi: (0, 0))],
        out_specs=pl.BlockSpec((1, 2, None, n, dx), lambda k, bi: (bi, 0, k, 0, 0)),
        out_shape=jax.ShapeDtypeStruct(a.shape, BF16),
        compiler_params=_cparams("parallel", "parallel"),
        name="dft_stage2_filter",
    )(a, spec, f_fwd, f_inv)


def _dft_out_kernel(z_ref, e_ref, u_ref, gate_ref, bias_ref, o_ref, *, inv_n, d):
    bias = bias_ref[...]
    for j in range(DFT_GROUP):
        sl = slice(j * d, (j + 1) * d)
        y = jnp.dot(e_ref[j], z_ref[0, :, sl], preferred_element_type=F32) * inv_n
        o_ref[0, :, sl] = gate_ref[0, :, sl] * (y + u_ref[0, :, sl] * bias)


def _dft_out(z, n, e_out, u, gate, bias):
    b, l, d = u.shape
    gw = DFT_GROUP * d
    row = pl.BlockSpec((1, n // 2, gw), lambda bi, j: (bi, 0, j))
    out = pl.pallas_call(
        functools.partial(_dft_out_kernel, inv_n=1.0 / (n * n), d=d),
        grid=(b, n // DFT_GROUP),
        in_specs=[pl.BlockSpec((1, 2 * n, gw), lambda bi, j: (bi, 0, j)),
                  pl.BlockSpec((DFT_GROUP, n // 2, 2 * n), lambda bi, j: (j, 0, 0)),
                  row, row, pl.BlockSpec((1, d), lambda bi, j: (0, 0))],
        out_specs=row,
        out_shape=jax.ShapeDtypeStruct((b, n // 2, n * d), F32),
        compiler_params=_cparams("parallel", "parallel"),
        name="dft_inverse_stage1",
    )(z.reshape(b, 2 * n, n * d), e_out, u.reshape(b, n // 2, n * d), gate.reshape(b, n // 2, n * d),
      bias.reshape(1, d))
    return out.reshape(b, l, d)


def _hyena_conv(u, gate, spec, f_idx, bias, tables):
    n, e_in, e_out, f_fwd, f_inv = tables
    a = _dft_in(u, n, e_in)
    z = _dft_mid(a, spec, f_idx, n, f_fwd, f_inv)
    return _dft_out(z, n, e_out, u, gate, bias)


DFT_UNROLL = 8


def _rdft_tables(seq_len):
    big = 2 * seq_len
    n = int(round(math.sqrt(big)))
    assert n * n == big and n % 32 == 0, "sequence length must give a square, tile-aligned DFT"
    kh = n // 2 + 1
    kp = -(-kh // 8) * 8
    k1 = np.arange(kp)[None, :, None]
    n1 = np.arange(n // 2)[None, None, :]
    n2 = np.arange(n)[:, None, None]
    ang = -2.0 * np.pi * ((k1 * (n * n1 + n2)) % big) / big
    live = (k1 < kh).astype(np.float64)
    e_in = np.concatenate([np.cos(ang) * live, np.sin(ang) * live], axis=1)
    wgt = np.where((k1 == 0) | (k1 == n // 2), 1.0, 2.0) * live
    e_out = np.transpose(np.concatenate([np.cos(ang) * wgt, np.sin(ang) * wgt], axis=1), (0, 2, 1))
    a2 = -2.0 * np.pi * ((np.arange(n)[:, None] * np.arange(n)[None, :]) % n) / n
    fr, fi = np.cos(a2), np.sin(a2)
    f_fwd = np.block([[fr, -fi], [fi, fr]])
    f_inv = np.block([[fr, fi], [-fi, fr]])
    kg = max(g for g in range(1, 21) if kh % g == 0)
    cast = lambda a: jnp.asarray(a, dtype=F32).astype(BF16)
    return dict(n=n, kh=kh, kp=kp, kg=kg, e_in=cast(e_in), e_out=cast(e_out), f_fwd=cast(f_fwd), f_inv=cast(f_inv))


ROW_PAD = 8


def _rdft_first_stage(u_ref, e_in_ref, x_pad, a_scr, n, kp):
    pitch = n + ROW_PAD
    for n1 in range(n // 2):
        x_pad[n1 * pitch:n1 * pitch + n, :] = u_ref[0, n1 * n:(n1 + 1) * n, :]

    def body(i, carry):
        n2s = [i * DFT_UNROLL + j for j in range(DFT_UNROLL)]
        xs = [x_pad[pl.ds(n2, n // 2, stride=pitch), :].astype(BF16) for n2 in n2s]
        acc = [_mm(e_in_ref[n2], x) for n2, x in zip(n2s, xs)]
        for n2, a in zip(n2s, acc):
            row = pl.multiple_of(n2 * kp, 8)
            a_scr[0, pl.ds(row, kp), :] = a[:kp]
            a_scr[1, pl.ds(row, kp), :] = a[kp:]
        return carry

    lax.fori_loop(0, n // DFT_UNROLL, body, 0)


def _rdft_second_stage(a_scr, k1s, n, kp, f_fwd):
    cols = [jnp.concatenate([a_scr[0, pl.ds(k1, n, stride=kp), :], a_scr[1, pl.ds(k1, n, stride=kp), :]],
                            axis=0).astype(BF16) for k1 in k1s]
    return [_mm(f_fwd, c) for c in cols]


def _hy_conv_kernel(u_ref, gate_ref, h_ref, ein_ref, eout_ref, ff_ref, fi_ref, bias_ref, o_ref, x_pad, a_scr,
                    *, n, kp, kg):
    step = pl.program_id(2)

    @pl.when(step == 0)
    def _():
        _rdft_first_stage(u_ref, ein_ref, x_pad, a_scr, n, kp)

    f_fwd, f_inv = ff_ref[...], fi_ref[...]
    for j0 in range(0, kg, DFT_UNROLL):
        js = list(range(j0, min(j0 + DFT_UNROLL, kg)))
        k1s = [step * kg + j for j in js]
        xs = _rdft_second_stage(a_scr, k1s, n, kp, f_fwd)
        ys = []
        for j, x in zip(js, xs):
            xr, xi = x[:n], x[n:]
            hr, hi = h_ref[0, j], h_ref[1, j]
            ys.append(jnp.concatenate([xr * hr - xi * hi, xr * hi + xi * hr], axis=0).astype(BF16))
        zs = [_mm(f_inv, y) for y in ys]
        for k1, z in zip(k1s, zs):
            a_scr[0, pl.ds(k1, n, stride=kp), :] = z[:n]
            a_scr[1, pl.ds(k1, n, stride=kp), :] = z[n:]

    @pl.when(step == pl.num_programs(2) - 1)
    def _():
        inv_n = 1.0 / (n * n)
        bias = bias_ref[...]
        pitch = n + ROW_PAD

        def body(i, carry):
            n2s = [i * DFT_UNROLL + j for j in range(DFT_UNROLL)]
            zc = []
            for n2 in n2s:
                row = pl.multiple_of(n2 * kp, 8)
                zc.append(jnp.concatenate([a_scr[0, pl.ds(row, kp), :], a_scr[1, pl.ds(row, kp), :]],
                                          axis=0).astype(BF16))
            ys = [_mm(eout_ref[n2], z) * inv_n for n2, z in zip(n2s, zc)]
            for n2, y in zip(n2s, ys):
                x_pad[pl.ds(n2, n // 2, stride=pitch), :] = y
            return carry

        lax.fori_loop(0, n // DFT_UNROLL, body, 0)
        for n1 in range(n // 2):
            rows = slice(n1 * n, (n1 + 1) * n)
            y = x_pad[n1 * pitch:n1 * pitch + n, :]
            o_ref[0, rows, :] = gate_ref[0, rows, :] * (y + u_ref[0, rows, :] * bias)


def _hy_conv(u, gate, spec, f_idx, bias, tb):
    b, l, d = u.shape
    n, kh, kp, kg = tb["n"], tb["kh"], tb["kp"], tb["kg"]
    nd = d // LANES
    tok = lambda bufs: pl.BlockSpec((1, l, LANES), lambda c, bi, g: (bi, 0, c), **bufs)
    once = dict(pipeline_mode=pl.Buffered(1))
    full = lambda a: pl.BlockSpec(a.shape, lambda c, bi, g: (0,) * a.ndim, **once)
    return pl.pallas_call(
        functools.partial(_hy_conv_kernel, n=n, kp=kp, kg=kg),
        grid=(nd, b, kh // kg),
        in_specs=[tok(once), tok(once),
                  pl.BlockSpec((2, kg, n, LANES), lambda c, bi, g: (0, g, 0, f_idx * nd + c)),
                  full(tb["e_in"]), full(tb["e_out"]), full(tb["f_fwd"]), full(tb["f_inv"]),
                  pl.BlockSpec((1, LANES), lambda c, bi, g: (0, c))],
        out_specs=tok({}),
        out_shape=jax.ShapeDtypeStruct((b, l, d), F32),
        scratch_shapes=[pltpu.VMEM((n // 2 * (n + ROW_PAD), LANES), F32), pltpu.VMEM((2, n * kp, LANES), F32)],
        compiler_params=_cparams("parallel", "parallel", "arbitrary"),
        name="hyena_conv",
    )(u, gate, spec, tb["e_in"], tb["e_out"], tb["f_fwd"], tb["f_inv"], bias.reshape(1, d))


def _hy_spec_kernel(u_ref, ein_ref, ff_ref, h_ref, x_pad, a_scr, *, n, kp, kg):
    step = pl.program_id(1)

    @pl.when(step == 0)
    def _():
        _rdft_first_stage(u_ref, ein_ref, x_pad, a_scr, n, kp)

    f_fwd = ff_ref[...]
    for j0 in range(0, kg, DFT_UNROLL):
        js = list(range(j0, min(j0 + DFT_UNROLL, kg)))
        xs = _rdft_second_stage(a_scr, [step * kg + j for j in js], n, kp, f_fwd)
        for j, x in zip(js, xs):
            h_ref[0, j] = x[:n]
            h_ref[1, j] = x[n:]


def _hy_spectrum(filt, tb):
    _, l, dx = filt.shape
    n, kh, kp, kg = tb["n"], tb["kh"], tb["kp"], tb["kg"]
    full = lambda a: pl.BlockSpec(a.shape, lambda c, g: (0,) * a.ndim, pipeline_mode=pl.Buffered(1))
    return pl.pallas_call(
        functools.partial(_hy_spec_kernel, n=n, kp=kp, kg=kg),
        grid=(dx // LANES, kh // kg),
        in_specs=[pl.BlockSpec((1, l, LANES), lambda c, g: (0, 0, c)), full(tb["e_in"]), full(tb["f_fwd"])],
        out_specs=pl.BlockSpec((2, kg, n, LANES), lambda c, g: (0, g, 0, c)),
        out_shape=jax.ShapeDtypeStruct((2, kh, n, dx), F32),
        scratch_shapes=[pltpu.VMEM((n // 2 * (n + ROW_PAD), LANES), F32), pltpu.VMEM((2, n * kp, LANES), F32)],
        compiler_params=_cparams("parallel", "arbitrary"),
        name="hyena_filter_spectrum",
    )(filt, tb["e_in"], tb["f_fwd"])


def _mix_out_kernel(x_ref, z_ref, mod_ref, w_ref, o_ref):
    out = jnp.dot(z_ref[...].astype(BF16), w_ref[...], preferred_element_type=F32)
    o_ref[...] = x_ref[...] + mod_ref[0][5:6] * out


def _mix_out(x, z, mods, rows_per_mod, w_out, tm):
    m, d = x.shape
    tok = pl.BlockSpec((tm, d), lambda i: (i, 0))
    return pl.pallas_call(
        _mix_out_kernel,
        grid=(m // tm,),
        in_specs=[tok, tok, pl.BlockSpec((1, 9, d), lambda i: ((i * tm) // rows_per_mod, 0, 0)),
                  pl.BlockSpec(w_out.shape, lambda i: (0, 0), pipeline_mode=pl.Buffered(1))],
        out_specs=tok,
        out_shape=jax.ShapeDtypeStruct((m, d), F32),
        compiler_params=_cparams("parallel"),
        name="hyena_out",
    )(x, z, mods, w_out)


def _hyena_mixer(hl, batch, hp, tm):
    (w_in, conv_w, conv_b, f_w1, f_b1, f_freq, f_w2, f_b2, f_w3, deltas, bias) = hp
    m, d = hl.shape
    seq_len = m // batch
    x1, x2, v = _hy_in(hl, seq_len, w_in.astype(BF16), conv_w, conv_b.reshape(1, -1), tm)
    three = lambda a: a.reshape(batch, seq_len, d)
    x1, x2, v = three(x1), three(x2), three(v)
    tables = _rdft_tables(seq_len)
    filt = _hy_filters(seq_len, f_w1, f_b1, f_freq, f_w2, f_b2, f_w3, deltas)
    spec = _hy_spectrum(filt[None], tables)
    z = _hy_conv(v, x1, spec, 0, bias[0], tables)
    z = _hy_conv(z, x2, spec, 1, bias[1], tables)
    return z.reshape(m, d)


def _rwkv_weights(mix, w_rkv, w0, w1, w2, a0, a1, a2, g1, g2, k_k, k_a, r_k):
    d = mix.shape[-1]
    lora = w1.shape[-1]
    zero = jnp.zeros((lora, d), F32)
    blockdiag = lambda m: jnp.concatenate([jnp.concatenate([m[0], zero], axis=1),
                                           jnp.concatenate([zero, m[1]], axis=1)], axis=0)
    gl = g1.shape[-1]
    glp = -(-gl // LANES) * LANES
    vec = jnp.zeros((8, 2 * d), F32)
    vec = vec.at[0].set(w0.reshape(-1)).at[1].set(a0.reshape(-1))
    vec = vec.at[2, :d].set(k_k).at[3, :d].set(k_a).at[4, :d].set(r_k.reshape(-1))
    return (mix, w_rkv.astype(BF16),
            jnp.concatenate([w1[0], w1[1]], axis=1).astype(BF16), blockdiag(w2).astype(BF16),
            jnp.concatenate([a1[0], a1[1]], axis=1).astype(BF16), blockdiag(a2).astype(BF16),
            jnp.pad(g1, ((0, 0), (0, glp - gl))).astype(BF16),
            jnp.pad(g2, ((0, glp - gl), (0, 0))).astype(BF16), vec)


def _rwkv_mixer(hl, hc, batch, wts, k_a, tm, tb):
    d = hl.shape[1]
    t_lat, t_ctx = hl.shape[0] // batch, hc.shape[0] // batch
    k_a = k_a.reshape(1, d)
    s0 = jnp.zeros((batch, d // LANES, LANES, LANES), F32)
    three = lambda arrs, t: [a.reshape(batch, t, d) for a in arrs]
    kc, vc, kkc, rc, lw0c, lw1c, a0c, a1c, _, _ = _rwkv_prep(hc, t_ctx, None, wts, min(tm, t_ctx))
    kc, vc, kkc, rc, lw0c, lw1c, a0c, a1c = three((kc, vc, kkc, rc, lw0c, lw1c, a0c, a1c), t_ctx)
    tbc = min(tb, t_ctx)
    _, s_f = _rwkv_scan(kc, vc, kkc, rc, lw0c, a0c, k_a, s0, False, tbc)
    _, s_b = _rwkv_scan(kc, vc, kkc, rc, lw1c, a1c, k_a, s0, True, tbc)
    k, v, kk, r, lw0, lw1, a0, a1, bonus, g = _rwkv_prep(hl, t_lat, GRID_W, wts, tm)
    k, v, kk, r, lw0, lw1, a0, a1 = three((k, v, kk, r, lw0, lw1, a0, a1), t_lat)
    yf, _ = _rwkv_scan(k, v, kk, r, lw0, a0, k_a, s_f, False, tb)
    yb, _ = _rwkv_scan(k, v, kk, r, lw1, a1, k_a, s_b, True, tb)
    return yf.reshape(-1, d), yb.reshape(-1, d), bonus, g


TOKEN_TILE = 512
PREP_TILE = 256
SCAN_BLOCK = 512


def kernel(x, c, ctx, c_ctx, mod_w, mod_b, norm_g, ffn_w_gu, ffn_w_down, rw_mix, rw_w_rkv, rw_w_o, rw_w0, rw_w1, rw_w2, rw_a0, rw_a1, rw_a2, rw_g1, rw_g2, rw_k_k, rw_k_a, rw_r_k, rw_ln_g, rw_ln_b, hy_w_in, hy_conv_w, hy_conv_b, hy_f_w1, hy_f_b1, hy_f_freq, hy_f_w2, hy_f_b2, hy_f_w3, hy_deltas, hy_bias, hy_w_out, final_g):
    batch, seq, d = x.shape
    t_ctx = ctx.shape[1]
    depth = mod_w.shape[0]
    assert depth == 2 and rw_mix.shape[0] == 1 and hy_w_in.shape[0] == 1
    assert seq % TOKEN_TILE == 0 and seq % GRID_W == 0 and t_ctx % CHUNK == 0 and batch + 1 <= 8

    cc = jnp.concatenate([c, c_ctx[None], jnp.zeros((8 - batch - 1, d), F32)], axis=0)
    mods0 = _modulation(cc, mod_w[0], mod_b[0])
    mods1 = _modulation(cc, mod_w[1], mod_b[1])
    wgu = [[ffn_w_gu[i, j].astype(BF16) for j in range(2)] for i in range(depth)]
    wd = [[ffn_w_down[i, j].astype(BF16) for j in range(2)] for i in range(depth)]
    xl = x.reshape(batch * seq, d)
    xc = ctx.reshape(batch * t_ctx, d)
    tm = TOKEN_TILE
    tmc = min(tm, t_ctx)

    xl, hl = _ffn(xl, mods0, 0, seq, norm_g[0, 0:2], wgu[0][0], wd[0][0], 0, 3, tm)
    _, hc = _ffn(xc, mods0, batch, batch * t_ctx, norm_g[0, 0:2], wgu[0][0], wd[0][0], 0, 3, tmc)
    wts = _rwkv_weights(rw_mix[0], rw_w_rkv[0], rw_w0[0], rw_w1[0], rw_w2[0], rw_a0[0], rw_a1[0], rw_a2[0],
                        rw_g1[0], rw_g2[0], rw_k_k[0], rw_k_a[0], rw_r_k[0])
    yf, yb, bonus, g = _rwkv_mixer(hl, hc, batch, wts, rw_k_a[0], PREP_TILE, SCAN_BLOCK)
    xl = _rwkv_out(xl, yf, yb, bonus, g, mods0, seq, jnp.stack([rw_ln_g[0], rw_ln_b[0]]),
                   rw_w_o[0].astype(BF16), tm)
    xl, _ = _ffn(xl, mods0, 0, seq, jnp.stack([norm_g[0, 2], final_g]), wgu[0][1], wd[0][1], 6, None, tm)

    xl, hl = _ffn(xl, mods1, 0, seq, norm_g[1, 0:2], wgu[1][0], wd[1][0], 0, 3, tm)
    hp = (hy_w_in[0], hy_conv_w[0], hy_conv_b[0], hy_f_w1[0], hy_f_b1[0], hy_f_freq[0], hy_f_w2[0],
          hy_f_b2[0], hy_f_w3[0], hy_deltas[0], hy_bias[0])
    z = _hyena_mixer(hl, batch, hp, tm)
    xl = _mix_out(xl, z, mods1, seq, hy_w_out[0].astype(BF16), tm)
    out, _ = _ffn(xl, mods1, 0, seq, jnp.stack([norm_g[1, 2], final_g]), wgu[1][1], wd[1][1], 6, "final", tm)
    return out.reshape(batch, seq, d)
```

```python
import functools
import math

import jax
import jax.numpy as jnp
import numpy as np
from jax import lax
from jax.experimental import pallas as pl
from jax.experimental.pallas import tpu as pltpu

F32 = jnp.float32
BF16 = jnp.bfloat16

GRID_W = 64
RW_HEAD = 64
NORM_EPS = 1e-6
GN_EPS = 64e-5
HY_BANDS = 16

LANES = 128
CHUNK = 64
VMEM_LIMIT = 56 * 1024 * 1024


def _cparams(*sem):
    return pltpu.CompilerParams(dimension_semantics=sem, vmem_limit_bytes=VMEM_LIMIT)


def _bdot(a, b):
    return jnp.dot(a.astype(BF16), b.astype(BF16), preferred_element_type=F32)


def _bdot_nt(a, b):
    return lax.dot_general(a.astype(BF16), b.astype(BF16), (((1,), (1,)), ((), ())),
                           preferred_element_type=F32)


def _bdot_tn(a, b):
    return lax.dot_general(a.astype(BF16), b.astype(BF16), (((0,), (0,)), ((), ())),
                           preferred_element_type=F32)


def _rms_mod(x, g, shift, scale):
    ms = jnp.mean(x * x, axis=-1, keepdims=True)
    return (x * lax.rsqrt(ms + NORM_EPS) * g) * (1.0 + scale) + shift


def _mod_kernel(c_ref, w_ref, b_ref, o_ref):
    c = c_ref[...]
    o_ref[...] = _bdot(c * jax.nn.sigmoid(c), w_ref[...]) + b_ref[...]


def _modulation(cc, w, b):
    m, d = cc.shape
    n = w.shape[1]
    tn = 1152
    out = pl.pallas_call(
        _mod_kernel,
        grid=(n // tn,),
        in_specs=[pl.BlockSpec((m, d), lambda j: (0, 0)),
                  pl.BlockSpec((d, tn), lambda j: (0, j)),
                  pl.BlockSpec((1, tn), lambda j: (0, j))],
        out_specs=pl.BlockSpec((m, tn), lambda j: (0, j)),
        out_shape=jax.ShapeDtypeStruct((m, n), F32),
        compiler_params=_cparams("parallel"),
        name="modulation",
    )(cc, w, b.reshape(1, n))
    return out.reshape(m, 9, d)


def _ffn_kernel(x_ref, mod_ref, g_ref, wgu_ref, wd_ref, *out_refs, mod_off, f_chunk, next_off):
    tm = x_ref.shape[0]
    mod = mod_ref[0]
    g = g_ref[...]
    f = wd_ref.shape[0]
    nh = 2 if tm % 32 == 0 else 1
    rows = [slice(i * tm // nh, (i + 1) * tm // nh) for i in range(nh)]
    xs, hs, gu, accs = [None] * nh, [None] * nh, [None] * nh, [None] * nh
    for c0 in range(0, f, f_chunk):
        for i in range(nh):
            if c0 == 0:
                xs[i] = x_ref[rows[i], :]
                hs[i] = _rms_mod(xs[i], g[0:1], mod[mod_off:mod_off + 1], mod[mod_off + 1:mod_off + 2]).astype(BF16)
            gu[i] = (jnp.dot(hs[i], wgu_ref[:, c0:c0 + f_chunk], preferred_element_type=F32),
                     jnp.dot(hs[i], wgu_ref[:, f + c0:f + c0 + f_chunk], preferred_element_type=F32))
        for i in range(nh):
            gate, up = gu[i]
            a = (gate * jax.nn.sigmoid(gate) * up).astype(BF16)
            down = jnp.dot(a, wd_ref[c0:c0 + f_chunk, :], preferred_element_type=F32)
            accs[i] = down if c0 == 0 else accs[i] + down
    for i in range(nh):
        xn = xs[i] + (0.5 * mod[mod_off + 2:mod_off + 3]) * accs[i]
        if next_off == "final":
            ms = jnp.mean(xn * xn, axis=-1, keepdims=True)
            out_refs[0][rows[i], :] = xn * lax.rsqrt(ms + NORM_EPS) * g[1:2]
            continue
        out_refs[0][rows[i], :] = xn
        if next_off is not None:
            out_refs[1][rows[i], :] = _rms_mod(xn, g[1:2], mod[next_off:next_off + 1],
                                               mod[next_off + 1:next_off + 2])


def _ffn(x, mods, mod_row0, rows_per_mod, norm_g2, wgu, wd, layer, which, mod_off, next_off, tm):
    m, d = x.shape
    f = wd.shape[2]
    f_chunk = f // 2 if (f // 2) % LANES == 0 else f
    n_out = 2 if isinstance(next_off, int) else 1
    kern = functools.partial(_ffn_kernel, mod_off=mod_off, f_chunk=f_chunk, next_off=next_off)
    tok = pl.BlockSpec((tm, d), lambda i: (i, 0))
    outs = pl.pallas_call(
        kern,
        grid=(m // tm,),
        in_specs=[tok,
                  pl.BlockSpec((1, 9, d), lambda i: (mod_row0 + (i * tm) // rows_per_mod, 0, 0)),
                  pl.BlockSpec((2, d), lambda i: (0, 0)),
                  pl.BlockSpec((None, None) + wgu.shape[2:], lambda i: (layer, which, 0, 0),
                               pipeline_mode=pl.Buffered(1)),
                  pl.BlockSpec((None, None) + wd.shape[2:], lambda i: (layer, which, 0, 0),
                               pipeline_mode=pl.Buffered(1))],
        out_specs=[tok] * n_out,
        out_shape=[jax.ShapeDtypeStruct((m, d), F32)] * n_out,
        compiler_params=_cparams("parallel"),
        name="ffn_halfstep",
    )(x, mods, norm_g2, wgu, wd)
    return outs if n_out == 2 else (outs[0], None)


def _head_sum(x):
    row = lax.broadcasted_iota(jnp.int32, (LANES, LANES), 0) // RW_HEAD
    col = lax.broadcasted_iota(jnp.int32, (LANES, LANES), 1) // RW_HEAD
    ones_bd = jnp.where(row == col, 1.0, 0.0).astype(BF16)
    hi = x.astype(BF16)
    lo = (x - hi.astype(F32)).astype(BF16)
    outs = []
    for j in range(x.shape[1] // LANES):
        sl = slice(j * LANES, (j + 1) * LANES)
        outs.append(jnp.dot(hi[:, sl], ones_bd, preferred_element_type=F32)
                    + jnp.dot(lo[:, sl], ones_bd, preferred_element_type=F32))
    return jnp.concatenate(outs, axis=1)


def _rwkv_prep_kernel(h_ref, hp_ref, hn_ref, mix_ref, wrkv_ref, w1_ref, w2_ref, a1_ref, a2_ref,
                      g1_ref, g2_ref, vec_ref,
                      k_ref, v_ref, kk_ref, r_ref, lw0_ref, lw1_ref, a0_ref, a1o_ref, bonus_ref, g_ref,
                      *, seq_len, grid_w, halo):
    tm, d = h_ref.shape
    q = d // 4
    h = h_ref[...]
    hext = jnp.concatenate([hp_ref[...], h, hn_ref[...]], axis=0)
    t = (pl.program_id(0) * tm + lax.broadcasted_iota(jnp.int32, (tm, 1), 0)) % seq_len
    if grid_w is None:
        offs = (-1, 1, -1, 1)
    else:
        offs = (-1, 1, -grid_w, grid_w)
    parts = []
    for qi, o in enumerate(offs):
        piece = hext[halo + o:halo + o + tm, qi * q:(qi + 1) * q]
        ok = (t + o >= 0) & (t + o < seq_len)
        if grid_w is not None and abs(o) == 1:
            colp = t % grid_w + o
            ok = ok & (colp >= 0) & (colp < grid_w)
        parts.append(jnp.where(ok, piece, 0.0))
    dx = jnp.concatenate(parts, axis=1) - h
    mix = mix_ref[...]
    vec = vec_ref[...]
    w0c, a0c = vec[0:1], vec[1:2]
    k_k, k_a, r_k = vec[2:3, :d], vec[3:4, :d], vec[4:5, :d]
    xr, xw, xk, xv, xa, xg = [(h + dx * mix[j:j + 1]).astype(BF16) for j in range(6)]
    r = jnp.dot(xr, wrkv_ref[0], preferred_element_type=F32)
    k = jnp.dot(xk, wrkv_ref[1], preferred_element_type=F32)
    v = jnp.dot(xv, wrkv_ref[2], preferred_element_type=F32)
    wl = jnp.tanh(jnp.dot(xw, w1_ref[...], preferred_element_type=F32))
    w_pre = w0c + _bdot(wl, w2_ref[...])
    lw = (-math.exp(-0.5)) * jax.nn.sigmoid(w_pre)
    al = jnp.dot(xa, a1_ref[...], preferred_element_type=F32)
    a = jax.nn.sigmoid(a0c + _bdot(al, a2_ref[...]))
    gl = jax.nn.sigmoid(jnp.dot(xg, g1_ref[...], preferred_element_type=F32))
    g = _bdot(gl, g2_ref[...])
    kk = k * k_k
    ss = _head_sum(kk * kk)
    kk = kk * lax.rsqrt(jnp.maximum(ss, 1e-24))
    a_f, a_b = a[:, :d], a[:, d:]
    k_bonus = k * (1.0 + (0.5 * (a_f + a_b) - 1.0) * k_a)
    bonus = _head_sum(r * k_bonus * r_k) * v
    k_ref[...] = k
    v_ref[...] = v
    kk_ref[...] = kk
    r_ref[...] = r
    lw0_ref[...] = lw[:, :d]
    lw1_ref[...] = lw[:, d:]
    a0_ref[...] = a_f
    a1o_ref[...] = a_b
    bonus_ref[...] = bonus
    g_ref[...] = g.astype(g_ref.dtype)


def _rwkv_prep(h, seq_len, grid_w, p, tm):
    (mix, w_rkv, w1c, w2bd, a1c, a2bd, g1p, g2p, vec) = p
    m, d = h.shape
    halo = GRID_W
    nb = tm // halo
    last = m // halo - 1
    full = lambda a: pl.BlockSpec(a.shape, lambda i: (0,) * a.ndim, pipeline_mode=pl.Buffered(1))
    tok = pl.BlockSpec((tm, d), lambda i: (i, 0))
    kern = functools.partial(_rwkv_prep_kernel, seq_len=seq_len, grid_w=grid_w, halo=halo)
    return pl.pallas_call(
        kern,
        grid=(m // tm,),
        in_specs=[tok,
                  pl.BlockSpec((halo, d), lambda i: (jnp.maximum(i * nb - 1, 0), 0)),
                  pl.BlockSpec((halo, d), lambda i: (jnp.minimum((i + 1) * nb, last), 0)),
                  full(mix), full(w_rkv), full(w1c), full(w2bd), full(a1c), full(a2bd),
                  full(g1p), full(g2p), full(vec)],
        out_specs=[tok] * 10,
        out_shape=[jax.ShapeDtypeStruct((m, d), F32)] * 9 + [jax.ShapeDtypeStruct((m, d), BF16)],
        compiler_params=_cparams("parallel"),
        name="rwkv_prep",
    )(h, h, h, mix, w_rkv, w1c, w2bd, a1c, a2bd, g1p, g2p, vec)


def _stack_heads(x):
    lane = lax.broadcasted_iota(jnp.int32, x.shape, 1)
    first = lane < RW_HEAD
    return jnp.concatenate([jnp.where(first, x, 0.0), jnp.where(first, 0.0, x)], axis=0)


def _each(f, *lists):
    return [f(*args) for args in zip(*lists)]


def _mm(a, b):
    return jnp.dot(a, b, preferred_element_type=F32)


def _mm_nt(a, b):
    return lax.dot_general(a, b, (((1,), (1,)), ((), ())), preferred_element_type=F32)


def _mm_tn(a, b):
    return lax.dot_general(a, b, (((0,), (0,)), ((), ())), preferred_element_type=F32)


def _to_bf16(x):
    return x.astype(BF16)


INV_BASE = 16


def _row_blocks(x, size, parity):
    return jnp.concatenate([x[b * size:(b + 1) * size] for b in range(x.shape[0] // size) if b % 2 == parity],
                           axis=0)


def _put_row_blocks(xc, size, parity, base):
    out, k = [], 0
    for b in range(2 * xc.shape[0] // size):
        if b % 2 == parity:
            out.append(xc[k * size:(k + 1) * size])
            k += 1
        else:
            out.append(jnp.zeros((size, xc.shape[1]), xc.dtype) if base is None else base[b * size:(b + 1) * size])
    return jnp.concatenate(out, axis=0)


def _unit_tri_inverse(l_mats, ri, ci, reverse):
    nb = INV_BASE
    n = l_mats[0].shape[0]
    b16 = (ri // nb) == (ci // nb)
    b32 = (ri // (2 * nb)) == (ci // (2 * nb))
    sr = lax.broadcasted_iota(jnp.int32, (nb, n), 0)
    sc = lax.broadcasted_iota(jnp.int32, (nb, n), 1)
    eye_s = jnp.where(sr == sc % nb, 1.0, 0.0)
    strip = lambda m: functools.reduce(lambda a, b: a + b, [m[b * nb:(b + 1) * nb] for b in range(n // nb)])
    block_diag = lambda s: jnp.where(b16, jnp.concatenate([s] * (n // nb), axis=0), 0.0)

    l16 = _each(lambda l: jnp.where(b16, l, 0.0), l_mats)
    l16s = _each(strip, l16)
    xs = _each(lambda s: eye_s - s, l16s)
    ps = _each(lambda s, m: _mm(s.astype(BF16), m.astype(BF16)), l16s, l16)
    for it in range(3):
        pbd = _each(lambda s: block_diag(s).astype(BF16), ps)
        if it < 2:
            both = _each(lambda x_, s, p_: _mm(jnp.concatenate([x_, s], axis=0).astype(BF16), p_), xs, ps, pbd)
            xs = _each(lambda x_, b_: x_ + b_[:nb], xs, both)
            ps = _each(lambda b_: b_[nb:], both)
        else:
            xs = _each(lambda x_, p_: x_ + _mm(x_.astype(BF16), p_), xs, pbd)
    x = _each(block_diag, xs)

    par = 0 if reverse else 1
    for size, off_mask in ((nb, b32 & (~b16)), (2 * nb, ~b32)):
        lc = _each(lambda l: _row_blocks(jnp.where(off_mask, l, 0.0), size, par).astype(BF16), l_mats)
        xb = _each(_to_bf16, x)
        t = _each(lambda l_, x_: _put_row_blocks(_mm(l_, x_).astype(BF16), size, par, None), lc, xb)
        xc = _each(lambda x_: _row_blocks(x_, size, par), x)
        r = _each(lambda x_, t_: x_ - _mm(x_.astype(BF16), t_), xc, t)
        x = _each(lambda r_, x_: _put_row_blocks(r_, size, par, x_), r, x)
    return x


def _chunk_cumsum(x, reverse):
    c = x.shape[0]
    row = lax.broadcasted_iota(jnp.int32, x.shape, 0)
    s = 1
    while s < c:
        if reverse:
            x = x + jnp.where(row < c - s, pltpu.roll(x, c - s, axis=0), 0.0)
        else:
            x = x + jnp.where(row >= s, pltpu.roll(x, s, axis=0), 0.0)
        s *= 2
    return x


def _chunk_local(lw, k, v, kk, r, a, k_a, reverse):
    c = lw[0].shape[0]
    n = 2 * c
    kd = _each(lambda k_, a_, ka_: k_ * (1.0 + (a_ - 1.0) * ka_), k, a, k_a)
    ka = _each(lambda kk_, a_: kk_ * a_, kk, a)
    cum = _each(lambda x: _chunk_cumsum(x, reverse), lw)
    tot = _each(lambda x: x[0:1] if reverse else x[c - 1:c], cum)
    e_neg = _each(lambda x: jnp.exp(-x), cum)
    e_end = _each(lambda t_, x: jnp.exp(t_ - x), tot, cum)
    stack_b = lambda x: _stack_heads(x).astype(BF16)
    alpha = _each(lambda kk_, c_, l_: stack_b(kk_ * jnp.exp(c_ - l_)), kk, cum, lw)
    beta = _each(lambda x, e: stack_b(x * e), ka, e_neg)
    kappa = _each(lambda x, e: stack_b(x * e), kd, e_neg)
    rho = _each(lambda r_, c_: _stack_heads(r_ * jnp.exp(c_)), r, cum)
    kappa_e = _each(lambda x, e: stack_b(x * e), kd, e_end)
    beta_e = _each(lambda x, e: stack_b(x * e), ka, e_end)
    v_st = _each(stack_b, v)
    w_end = _each(jnp.exp, tot)

    sc = _each(lambda al, rh, be, kp: _mm_nt(jnp.concatenate([al, rh.astype(BF16)], axis=0),
                                             jnp.concatenate([be, kp], axis=0)), alpha, rho, beta, kappa)
    ri = lax.broadcasted_iota(jnp.int32, (n, n), 0)
    ci = lax.broadcasted_iota(jnp.int32, (n, n), 1)
    strict = (ci > ri) if reverse else (ci < ri)
    incl = (ci >= ri) if reverse else (ci <= ri)
    l_mat = _each(lambda x: jnp.where(strict, x[:n, :n], 0.0), sc)
    apk = _each(lambda x: jnp.concatenate([jnp.where(strict, x[:n, n:], 0.0), jnp.where(incl, x[n:, n:], 0.0)],
                                          axis=0).astype(BF16), sc)
    pb = _each(lambda x: jnp.where(incl, x[n:, :n], 0.0).astype(BF16), sc)

    apkv = _each(_mm, apk, v_st)
    x = _each(_to_bf16, _unit_tri_inverse(l_mat, ri, ci, reverse))
    uwb = _each(lambda x_, kv, al: _mm(x_, jnp.concatenate([kv[:n].astype(BF16), al], axis=1)).astype(BF16),
                x, apkv, alpha)
    yr = _each(lambda kv, rh, pb_, uw: jnp.concatenate([kv[n:], rh], axis=1) - _mm(pb_, uw), apkv, rho, pb, uwb)
    uq = _each(_mm_tn, uwb, beta_e)
    n_t = _each(lambda v_, ke, uq_: _mm_tn(v_, ke) - uq_[:n], v_st, kappa_e, uq)
    q = _each(lambda uq_: uq_[n:].astype(BF16), uq)
    return list(zip(yr, n_t, q, w_end))


def _chunk_state(s, local):
    n = local[0][0].shape[0]
    c = n // 2
    sb = _each(_to_bf16, s)
    y_st = _each(lambda lc, sb_: lc[0][:, :n] + _mm_nt(lc[0][:, n:].astype(BF16), sb_), local, sb)
    y = _each(lambda x_: x_[:c] + x_[c:], y_st)
    s_new = _each(lambda s_, sb_, lc: s_ * lc[3] - _mm(sb_, lc[2]) + lc[1], s, sb, local)
    return s_new, y


SCAN_SLABS = 8
SCAN_PAIR = 4


def _scan_kernel(k_ref, v_ref, kk_ref, r_ref, lw_ref, a_ref, ka_ref, s0_ref, y_ref, sout_ref, s_scr,
                 *, reverse, chunk):
    tb = k_ref.shape[1]
    nslab = s_scr.shape[0]
    nchunk = tb // chunk
    step = pl.program_id(2)

    @pl.when(step == 0)
    def _():
        s_scr[...] = s0_ref[0]

    lanes = [slice(g * LANES, (g + 1) * LANES) for g in range(nslab)]
    pair = math.gcd(SCAN_PAIR, nchunk)

    def body(i, carry):
        cis = [(nchunk - 1 - (i * pair + p)) if reverse else (i * pair + p) for p in range(pair)]
        sls = [pl.ds(pl.multiple_of(ci * chunk, chunk), chunk) for ci in cis]
        get = lambda ref: [ref[0, sl, ln] for sl in sls for ln in lanes]
        local = _chunk_local(get(lw_ref), get(k_ref), get(v_ref), get(kk_ref), get(r_ref), get(a_ref),
                             [ka_ref[:, ln] for _ in sls for ln in lanes], reverse)
        s = [s_scr[g] for g in range(nslab)]
        for p, sl in enumerate(sls):
            s, y = _chunk_state(s, local[p * nslab:(p + 1) * nslab])
            for g in range(nslab):
                y_ref[0, sl, lanes[g]] = y[g]
        for g in range(nslab):
            s_scr[g] = s[g]
        return carry

    lax.fori_loop(0, nchunk // pair, body, 0)

    @pl.when(step == pl.num_programs(2) - 1)
    def _():
        sout_ref[0] = s_scr[...]


def _rwkv_scan(k, v, kk, r, lw, a, k_a, s0, reverse, tb):
    b, t, d = k.shape
    g = SCAN_SLABS
    gw = g * LANES
    nt = t // tb
    tmap = (lambda bi, j, c: (bi, nt - 1 - c, j)) if reverse else (lambda bi, j, c: (bi, c, j))
    tok = pl.BlockSpec((1, tb, gw), tmap)
    st = pl.BlockSpec((1, g, LANES, LANES), lambda bi, j, c: (bi, j, 0, 0))
    kern = functools.partial(_scan_kernel, reverse=reverse, chunk=CHUNK)
    return pl.pallas_call(
        kern,
        grid=(b, d // gw, nt),
        in_specs=[tok] * 6 + [pl.BlockSpec((1, gw), lambda bi, j, c: (0, j)), st],
        out_specs=[tok, st],
        out_shape=[jax.ShapeDtypeStruct((b, t, d), F32),
                   jax.ShapeDtypeStruct((b, d // LANES, LANES, LANES), F32)],
        scratch_shapes=[pltpu.VMEM((g, LANES, LANES), F32)],
        compiler_params=_cparams("parallel", "parallel", "arbitrary"),
        name="rwkv_scan_bwd" if reverse else "rwkv_scan_fwd",
    )(k, v, kk, r, lw, a, k_a, s0)


def _rwkv_out_kernel(x_ref, yf_ref, yb_ref, bonus_ref, g_ref, mod_ref, ln_ref, wo_ref, o_ref):
    y = yf_ref[...] + yb_ref[...]
    ln = ln_ref[...]
    inv_n = 1.0 / RW_HEAD
    mu = _head_sum(y) * inv_n
    yc = y - mu
    var = _head_sum(yc * yc) * inv_n
    yn = yc * lax.rsqrt(var + GN_EPS) * ln[0:1] + ln[1:2]
    z = ((yn + bonus_ref[...]) * g_ref[...]).astype(BF16)
    out = jnp.dot(z, wo_ref[...], preferred_element_type=F32)
    o_ref[...] = x_ref[...] + mod_ref[0][5:6] * out


def _rwkv_out(x, yf, yb, bonus, g, mods, rows_per_mod, ln, w_o, tm):
    m, d = x.shape
    tok = pl.BlockSpec((tm, d), lambda i: (i, 0))
    return pl.pallas_call(
        _rwkv_out_kernel,
        grid=(m // tm,),
        in_specs=[tok] * 5 + [pl.BlockSpec((1, 9, d), lambda i: ((i * tm) // rows_per_mod, 0, 0)),
                              pl.BlockSpec((2, d), lambda i: (0, 0)),
                              pl.BlockSpec(w_o.shape, lambda i: (0, 0), pipeline_mode=pl.Buffered(1))],
        out_specs=tok,
        out_shape=jax.ShapeDtypeStruct((m, d), F32),
        compiler_params=_cparams("parallel"),
        name="rwkv_out",
    )(x, yf, yb, bonus, g, mods, ln, w_o)


def _hy_in_kernel(h_ref, hp_ref, hn_ref, w_ref, cw_ref, cb_ref, x1_ref, x2_ref, v_ref, *, seq_len):
    tm, d = h_ref.shape
    halo = hp_ref.shape[0]
    cw = cw_ref[...]
    cb = cb_ref[...]
    nh = 2 if tm % (4 * halo) == 0 else 1
    th = tm // nh
    pes = []
    for i in range(nh):
        before = hp_ref[...] if i == 0 else h_ref[i * th - halo:i * th, :]
        after = hn_ref[...] if i == nh - 1 else h_ref[(i + 1) * th:(i + 1) * th + halo, :]
        hext = jnp.concatenate([before, h_ref[i * th:(i + 1) * th, :], after], axis=0).astype(BF16)
        pes.append(jnp.dot(hext, w_ref[...], preferred_element_type=F32))
    for i, pe in enumerate(pes):
        t = (pl.program_id(0) * tm + i * th + lax.broadcasted_iota(jnp.int32, (th, 1), 0)) % seq_len
        prev = jnp.where(t >= 1, pe[halo - 1:halo - 1 + th], 0.0)
        nxt = jnp.where(t + 1 < seq_len, pe[halo + 1:halo + 1 + th], 0.0)
        u = prev * cw[0:1] + pe[halo:halo + th] * cw[1:2] + nxt * cw[2:3] + cb
        rows = slice(i * th, (i + 1) * th)
        x1_ref[rows, :] = u[:, :d]
        x2_ref[rows, :] = u[:, d:2 * d]
        v_ref[rows, :] = u[:, 2 * d:]


def _hy_in(h, seq_len, w_in, conv_w, conv_b, tm):
    m, d = h.shape
    halo = 8
    nb = tm // halo
    last = m // halo - 1
    tok = pl.BlockSpec((tm, d), lambda i: (i, 0))
    full = lambda a: pl.BlockSpec(a.shape, lambda i: (0,) * a.ndim, pipeline_mode=pl.Buffered(1))
    return pl.pallas_call(
        functools.partial(_hy_in_kernel, seq_len=seq_len),
        grid=(m // tm,),
        in_specs=[tok,
                  pl.BlockSpec((halo, d), lambda i: (jnp.maximum(i * nb - 1, 0), 0)),
                  pl.BlockSpec((halo, d), lambda i: (jnp.minimum((i + 1) * nb, last), 0)),
                  full(w_in), full(conv_w), full(conv_b)],
        out_specs=[tok] * 3,
        out_shape=[jax.ShapeDtypeStruct((m, d), F32)] * 3,
        compiler_params=_cparams("parallel"),
        name="hyena_in",
    )(h, h, h, w_in, conv_w, conv_b)


def _hy_filter_kernel(z_ref, w1_ref, b1_ref, fr_ref, w2_ref, b2_ref, w3_ref, dl_ref, o_ref):
    hp = lax.Precision.HIGHEST
    z = z_ref[...]
    fr = fr_ref[...]
    hid = jnp.sin(fr[0:1] * (jnp.dot(z, w1_ref[...], precision=hp, preferred_element_type=F32) + b1_ref[...]))
    hid = jnp.sin(fr[1:2] * (jnp.dot(hid, w2_ref[...], precision=hp, preferred_element_type=F32) + b2_ref[...]))
    filt = jnp.dot(hid, w3_ref[...], precision=hp, preferred_element_type=F32)
    o_ref[...] = filt * jnp.exp(-z[:, 0:1] * jnp.abs(dl_ref[...]))


def _hy_filters(seq_len, f_w1, f_b1, f_freq, f_w2, f_b2, f_w3, deltas):
    t = jnp.linspace(0.0, 1.0, seq_len, dtype=F32)[:, None]
    ang = ((2 * math.pi / seq_len) * jnp.arange(seq_len, dtype=F32)[:, None]
           * jnp.linspace(1e-4, HY_BANDS - 1, HY_BANDS, dtype=F32)[None])
    z = jnp.concatenate([t, jnp.cos(ang), -jnp.sin(ang)], axis=-1)
    emb, hid = f_w1.shape
    z = jnp.pad(z, ((0, 0), (0, LANES - emb)))
    w1 = jnp.pad(f_w1, ((0, LANES - emb), (0, 0)))
    n_out = f_w3.shape[1]
    tl = min(seq_len, 512)
    full = lambda a: pl.BlockSpec(a.shape, lambda i: (0,) * a.ndim)
    args = (w1, f_b1.reshape(1, hid), f_freq, f_w2, f_b2.reshape(1, hid), f_w3, deltas.reshape(1, n_out))
    return pl.pallas_call(
        _hy_filter_kernel,
        grid=(seq_len // tl,),
        in_specs=[pl.BlockSpec((tl, LANES), lambda i: (i, 0))] + [full(a) for a in args],
        out_specs=pl.BlockSpec((tl, n_out), lambda i: (i, 0)),
        out_shape=jax.ShapeDtypeStruct((seq_len, n_out), F32),
        compiler_params=_cparams("parallel"),
        name="hyena_filters",
    )(z, *args)


DFT_UNROLL = 8


def _rdft_tables(seq_len):
    big = 2 * seq_len
    n = int(round(math.sqrt(big)))
    assert n * n == big and n % 32 == 0, "sequence length must give a square, tile-aligned DFT"
    kh = n // 2 + 1
    kp = -(-kh // 8) * 8
    k1 = np.arange(kp)[None, :, None]
    n1 = np.arange(n // 2)[None, None, :]
    n2 = np.arange(n)[:, None, None]
    ang = -2.0 * np.pi * ((k1 * (n * n1 + n2)) % big) / big
    live = (k1 < kh).astype(np.float64)
    e_in = np.concatenate([np.cos(ang) * live, np.sin(ang) * live], axis=1)
    wgt = np.where((k1 == 0) | (k1 == n // 2), 1.0, 2.0) * live
    e_out = np.transpose(np.concatenate([np.cos(ang) * wgt, np.sin(ang) * wgt], axis=1), (0, 2, 1))
    a2 = -2.0 * np.pi * ((np.arange(n)[:, None] * np.arange(n)[None, :]) % n) / n
    fr, fi = np.cos(a2), np.sin(a2)
    f_fwd = np.block([[fr, -fi], [fi, fr]])
    f_inv = np.block([[fr, fi], [-fi, fr]])
    kg = max(g for g in range(1, 21) if kh % g == 0)
    cast = lambda a: jnp.asarray(a, dtype=F32).astype(BF16)
    return dict(n=n, kh=kh, kp=kp, kg=kg, e_in=cast(e_in), e_out=cast(e_out), f_fwd=cast(f_fwd), f_inv=cast(f_inv))


ROW_PAD = 8


def _rdft_first_stage(u_ref, e_in_ref, x_pad, a_scr, n, kp):
    pitch = n + ROW_PAD
    for n1 in range(n // 2):
        x_pad[n1 * pitch:n1 * pitch + n, :] = u_ref[0, n1 * n:(n1 + 1) * n, :]

    def body(i, carry):
        n2s = [i * DFT_UNROLL + j for j in range(DFT_UNROLL)]
        xs = [x_pad[pl.ds(n2, n // 2, stride=pitch), :].astype(BF16) for n2 in n2s]
        acc = [_mm(e_in_ref[n2], x) for n2, x in zip(n2s, xs)]
        for n2, a in zip(n2s, acc):
            row = pl.multiple_of(n2 * kp, 8)
            a_scr[0, pl.ds(row, kp), :] = a[:kp]
            a_scr[1, pl.ds(row, kp), :] = a[kp:]
        return carry

    lax.fori_loop(0, n // DFT_UNROLL, body, 0)


def _rdft_second_stage(a_scr, k1s, n, kp, f_fwd):
    cols = [jnp.concatenate([a_scr[0, pl.ds(k1, n, stride=kp), :], a_scr[1, pl.ds(k1, n, stride=kp), :]],
                            axis=0).astype(BF16) for k1 in k1s]
    return [_mm(f_fwd, c) for c in cols]


def _hy_conv_kernel(u_ref, gate_ref, h_ref, ein_ref, eout_ref, ff_ref, fi_ref, bias_ref, o_ref, x_pad, a_scr,
                    *, n, kp, kh, mg):
    _rdft_first_stage(u_ref, ein_ref, x_pad, a_scr, n, kp)

    f_fwd, f_inv = ff_ref[...], fi_ref[...]

    def mid(i, carry):
        k1s = [i * mg + j for j in range(mg)]
        xs = _rdft_second_stage(a_scr, k1s, n, kp, f_fwd)
        ys = []
        for k1, x in zip(k1s, xs):
            xr, xi = x[:n], x[n:]
            hr, hi = h_ref[0, k1], h_ref[1, k1]
            ys.append(jnp.concatenate([xr * hr - xi * hi, xr * hi + xi * hr], axis=0).astype(BF16))
        zs = [_mm(f_inv, y) for y in ys]
        for k1, z in zip(k1s, zs):
            a_scr[0, pl.ds(k1, n, stride=kp), :] = z[:n]
            a_scr[1, pl.ds(k1, n, stride=kp), :] = z[n:]
        return carry

    lax.fori_loop(0, kh // mg, mid, 0)

    inv_n = 1.0 / (n * n)
    bias = bias_ref[...]
    pitch = n + ROW_PAD

    def last(i, carry):
        n2s = [i * DFT_UNROLL + j for j in range(DFT_UNROLL)]
        zc = []
        for n2 in n2s:
            row = pl.multiple_of(n2 * kp, 8)
            zc.append(jnp.concatenate([a_scr[0, pl.ds(row, kp), :], a_scr[1, pl.ds(row, kp), :]],
                                      axis=0).astype(BF16))
        ys = [_mm(eout_ref[n2], z) * inv_n for n2, z in zip(n2s, zc)]
        for n2, y in zip(n2s, ys):
            x_pad[pl.ds(n2, n // 2, stride=pitch), :] = y
        return carry

    lax.fori_loop(0, n // DFT_UNROLL, last, 0)
    for n1 in range(n // 2):
        rows = slice(n1 * n, (n1 + 1) * n)
        y = x_pad[n1 * pitch:n1 * pitch + n, :]
        o_ref[0, rows, :] = gate_ref[0, rows, :] * (y + u_ref[0, rows, :] * bias)


def _hy_conv(u, gate, spec, f_idx, bias, tb):
    b, l, d = u.shape
    n, kh, kp = tb["n"], tb["kh"], tb["kp"]
    mg = max(g for g in range(1, 9) if kh % g == 0)
    nd = d // LANES
    once = dict(pipeline_mode=pl.Buffered(1))
    tok = lambda bufs: pl.BlockSpec((1, l, LANES), lambda c, bi: (bi, 0, c), **bufs)
    full = lambda a: pl.BlockSpec(a.shape, lambda c, bi: (0,) * a.ndim, **once)
    return pl.pallas_call(
        functools.partial(_hy_conv_kernel, n=n, kp=kp, kh=kh, mg=mg),
        grid=(nd, b),
        in_specs=[tok({}), tok(once),
                  pl.BlockSpec((2, kh, n, LANES), lambda c, bi: (0, 0, 0, f_idx * nd + c), **once),
                  full(tb["e_in"]), full(tb["e_out"]), full(tb["f_fwd"]), full(tb["f_inv"]),
                  pl.BlockSpec((1, LANES), lambda c, bi: (0, c))],
        out_specs=tok({}),
        out_shape=jax.ShapeDtypeStruct((b, l, d), F32),
        scratch_shapes=[pltpu.VMEM((n // 2 * (n + ROW_PAD), LANES), F32), pltpu.VMEM((2, n * kp, LANES), F32)],
        compiler_params=_cparams("parallel", "parallel"),
        name="hyena_conv",
    )(u, gate, spec, tb["e_in"], tb["e_out"], tb["f_fwd"], tb["f_inv"], bias.reshape(1, d))


def _hy_spec_kernel(u_ref, ein_ref, ff_ref, h_ref, x_pad, a_scr, *, n, kp, kg):
    step = pl.program_id(1)

    @pl.when(step == 0)
    def _():
        _rdft_first_stage(u_ref, ein_ref, x_pad, a_scr, n, kp)

    f_fwd = ff_ref[...]
    for j0 in range(0, kg, DFT_UNROLL):
        js = list(range(j0, min(j0 + DFT_UNROLL, kg)))
        xs = _rdft_second_stage(a_scr, [step * kg + j for j in js], n, kp, f_fwd)
        for j, x in zip(js, xs):
            h_ref[0, j] = x[:n]
            h_ref[1, j] = x[n:]


def _hy_spectrum(filt, tb):
    _, l, dx = filt.shape
    n, kh, kp, kg = tb["n"], tb["kh"], tb["kp"], tb["kg"]
    full = lambda a: pl.BlockSpec(a.shape, lambda c, g: (0,) * a.ndim, pipeline_mode=pl.Buffered(1))
    return pl.pallas_call(
        functools.partial(_hy_spec_kernel, n=n, kp=kp, kg=kg),
        grid=(dx // LANES, kh // kg),
        in_specs=[pl.BlockSpec((1, l, LANES), lambda c, g: (0, 0, c)), full(tb["e_in"]), full(tb["f_fwd"])],
        out_specs=pl.BlockSpec((2, kg, n, LANES), lambda c, g: (0, g, 0, c)),
        out_shape=jax.ShapeDtypeStruct((2, kh, n, dx), F32),
        scratch_shapes=[pltpu.VMEM((n // 2 * (n + ROW_PAD), LANES), F32), pltpu.VMEM((2, n * kp, LANES), F32)],
        compiler_params=_cparams("parallel", "arbitrary"),
        name="hyena_filter_spectrum",
    )(filt, tb["e_in"], tb["f_fwd"])


def _mix_out_kernel(x_ref, z_ref, mod_ref, w_ref, o_ref):
    out = jnp.dot(z_ref[...].astype(BF16), w_ref[...], preferred_element_type=F32)
    o_ref[...] = x_ref[...] + mod_ref[0][5:6] * out


def _mix_out(x, z, mods, rows_per_mod, w_out, tm):
    m, d = x.shape
    tok = pl.BlockSpec((tm, d), lambda i: (i, 0))
    return pl.pallas_call(
        _mix_out_kernel,
        grid=(m // tm,),
        in_specs=[tok, tok, pl.BlockSpec((1, 9, d), lambda i: ((i * tm) // rows_per_mod, 0, 0)),
                  pl.BlockSpec(w_out.shape, lambda i: (0, 0), pipeline_mode=pl.Buffered(1))],
        out_specs=tok,
        out_shape=jax.ShapeDtypeStruct((m, d), F32),
        compiler_params=_cparams("parallel"),
        name="hyena_out",
    )(x, z, mods, w_out)


def _hyena_mixer(hl, batch, hp, tm):
    (w_in, conv_w, conv_b, f_w1, f_b1, f_freq, f_w2, f_b2, f_w3, deltas, bias) = hp
    m, d = hl.shape
    seq_len = m // batch
    x1, x2, v = _hy_in(hl, seq_len, w_in.astype(BF16), conv_w, conv_b.reshape(1, -1), tm)
    three = lambda a: a.reshape(batch, seq_len, d)
    x1, x2, v = three(x1), three(x2), three(v)
    tables = _rdft_tables(seq_len)
    filt = _hy_filters(seq_len, f_w1, f_b1, f_freq, f_w2, f_b2, f_w3, deltas)
    spec = _hy_spectrum(filt[None], tables)
    z = _hy_conv(v, x1, spec, 0, bias[0], tables)
    z = _hy_conv(z, x2, spec, 1, bias[1], tables)
    return z.reshape(m, d)


def _rwkv_weights(mix, w_rkv, w0, w1, w2, a0, a1, a2, g1, g2, k_k, k_a, r_k):
    d = mix.shape[-1]
    lora = w1.shape[-1]
    zero = jnp.zeros((lora, d), F32)
    blockdiag = lambda m: jnp.concatenate([jnp.concatenate([m[0], zero], axis=1),
                                           jnp.concatenate([zero, m[1]], axis=1)], axis=0)
    gl = g1.shape[-1]
    glp = -(-gl // LANES) * LANES
    vec = jnp.zeros((8, 2 * d), F32)
    vec = vec.at[0].set(w0.reshape(-1)).at[1].set(a0.reshape(-1))
    vec = vec.at[2, :d].set(k_k).at[3, :d].set(k_a).at[4, :d].set(r_k.reshape(-1))
    return (mix, w_rkv.astype(BF16),
            jnp.concatenate([w1[0], w1[1]], axis=1).astype(BF16), blockdiag(w2).astype(BF16),
            jnp.concatenate([a1[0], a1[1]], axis=1).astype(BF16), blockdiag(a2).astype(BF16),
            jnp.pad(g1, ((0, 0), (0, glp - gl))).astype(BF16),
            jnp.pad(g2, ((0, glp - gl), (0, 0))).astype(BF16), vec)


def _rwkv_mixer(hl, hc, batch, wts, k_a, tm, tb):
    d = hl.shape[1]
    t_lat, t_ctx = hl.shape[0] // batch, hc.shape[0] // batch
    k_a = k_a.reshape(1, d)
    s0 = jnp.zeros((batch, d // LANES, LANES, LANES), F32)
    three = lambda arrs, t: [a.reshape(batch, t, d) for a in arrs]
    kc, vc, kkc, rc, lw0c, lw1c, a0c, a1c, _, _ = _rwkv_prep(hc, t_ctx, None, wts, min(tm, t_ctx))
    kc, vc, kkc, rc, lw0c, lw1c, a0c, a1c = three((kc, vc, kkc, rc, lw0c, lw1c, a0c, a1c), t_ctx)
    tbc = min(tb, t_ctx)
    _, s_f = _rwkv_scan(kc, vc, kkc, rc, lw0c, a0c, k_a, s0, False, tbc)
    _, s_b = _rwkv_scan(kc, vc, kkc, rc, lw1c, a1c, k_a, s0, True, tbc)
    k, v, kk, r, lw0, lw1, a0, a1, bonus, g = _rwkv_prep(hl, t_lat, GRID_W, wts, tm)
    k, v, kk, r, lw0, lw1, a0, a1 = three((k, v, kk, r, lw0, lw1, a0, a1), t_lat)
    yf, _ = _rwkv_scan(k, v, kk, r, lw0, a0, k_a, s_f, False, tb)
    yb, _ = _rwkv_scan(k, v, kk, r, lw1, a1, k_a, s_b, True, tb)
    return yf.reshape(-1, d), yb.reshape(-1, d), bonus, g


TOKEN_TILE = 512
PREP_TILE = 256
SCAN_BLOCK = 512


def kernel(x, c, ctx, c_ctx, mod_w, mod_b, norm_g, ffn_w_gu, ffn_w_down, rw_mix, rw_w_rkv, rw_w_o, rw_w0, rw_w1, rw_w2, rw_a0, rw_a1, rw_a2, rw_g1, rw_g2, rw_k_k, rw_k_a, rw_r_k, rw_ln_g, rw_ln_b, hy_w_in, hy_conv_w, hy_conv_b, hy_f_w1, hy_f_b1, hy_f_freq, hy_f_w2, hy_f_b2, hy_f_w3, hy_deltas, hy_bias, hy_w_out, final_g):
    batch, seq, d = x.shape
    t_ctx = ctx.shape[1]
    depth = mod_w.shape[0]
    assert depth == 2 and rw_mix.shape[0] == 1 and hy_w_in.shape[0] == 1
    assert seq % TOKEN_TILE == 0 and seq % GRID_W == 0 and t_ctx % CHUNK == 0 and batch + 1 <= 8

    cc = jnp.concatenate([c, c_ctx[None], jnp.zeros((8 - batch - 1, d), F32)], axis=0)
    mods0 = _modulation(cc, mod_w[0], mod_b[0])
    mods1 = _modulation(cc, mod_w[1], mod_b[1])
    wgu = ffn_w_gu.astype(BF16)
    wd = ffn_w_down.astype(BF16)
    xl = x.reshape(batch * seq, d)
    xc = ctx.reshape(batch * t_ctx, d)
    tm = TOKEN_TILE
    tmc = min(tm, t_ctx)

    xl, hl = _ffn(xl, mods0, 0, seq, norm_g[0, 0:2], wgu, wd, 0, 0, 0, 3, tm)
    _, hc = _ffn(xc, mods0, batch, batch * t_ctx, norm_g[0, 0:2], wgu, wd, 0, 0, 0, 3, tmc)
    wts = _rwkv_weights(rw_mix[0], rw_w_rkv[0], rw_w0[0], rw_w1[0], rw_w2[0], rw_a0[0], rw_a1[0], rw_a2[0],
                        rw_g1[0], rw_g2[0], rw_k_k[0], rw_k_a[0], rw_r_k[0])
    yf, yb, bonus, g = _rwkv_mixer(hl, hc, batch, wts, rw_k_a[0], PREP_TILE, SCAN_BLOCK)
    xl = _rwkv_out(xl, yf, yb, bonus, g, mods0, seq, jnp.stack([rw_ln_g[0], rw_ln_b[0]]),
                   rw_w_o[0].astype(BF16), tm)
    xl, _ = _ffn(xl, mods0, 0, seq, jnp.stack([norm_g[0, 2], final_g]), wgu, wd, 0, 1, 6, None, tm)

    xl, hl = _ffn(xl, mods1, 0, seq, norm_g[1, 0:2], wgu, wd, 1, 0, 0, 3, tm)
    hp = (hy_w_in[0], hy_conv_w[0], hy_conv_b[0], hy_f_w1[0], hy_f_b1[0], hy_f_freq[0], hy_f_w2[0],
          hy_f_b2[0], hy_f_w3[0], hy_deltas[0], hy_bias[0])
    z = _hyena_mixer(hl, batch, hp, tm)
    xl = _mix_out(xl, z, mods1, seq, hy_w_out[0].astype(BF16), tm)
    out, _ = _ffn(xl, mods1, 0, seq, jnp.stack([norm_g[1, 2], final_g]), wgu, wd, 1, 1, 6, "final", tm)
    return out.reshape(batch, seq, d)
```

```python
import functools
import math

import jax
import jax.numpy as jnp
import numpy as np
from jax import lax
from jax.experimental import pallas as pl
from jax.experimental.pallas import tpu as pltpu

F32 = jnp.float32
BF16 = jnp.bfloat16

GRID_W = 64
RW_HEAD = 64
NORM_EPS = 1e-6
GN_EPS = 64e-5
HY_BANDS = 16

LANES = 128
CHUNK = 64
VMEM_LIMIT = 56 * 1024 * 1024


def _cparams(*sem):
    return pltpu.CompilerParams(dimension_semantics=sem, vmem_limit_bytes=VMEM_LIMIT)


def _bdot(a, b):
    return jnp.dot(a.astype(BF16), b.astype(BF16), preferred_element_type=F32)


def _bdot_nt(a, b):
    return lax.dot_general(a.astype(BF16), b.astype(BF16), (((1,), (1,)), ((), ())),
                           preferred_element_type=F32)


def _bdot_tn(a, b):
    return lax.dot_general(a.astype(BF16), b.astype(BF16), (((0,), (0,)), ((), ())),
                           preferred_element_type=F32)


def _rms_mod(x, g, shift, scale):
    ms = jnp.mean(x * x, axis=-1, keepdims=True)
    return (x * lax.rsqrt(ms + NORM_EPS) * g) * (1.0 + scale) + shift


def _mod_kernel(c_ref, w_ref, b_ref, o_ref):
    c = c_ref[...]
    o_ref[...] = _bdot(c * jax.nn.sigmoid(c), w_ref[...]) + b_ref[...]


def _modulation(cc, w, b):
    m, d = cc.shape
    n = w.shape[1]
    tn = 1152
    out = pl.pallas_call(
        _mod_kernel,
        grid=(n // tn,),
        in_specs=[pl.BlockSpec((m, d), lambda j: (0, 0)),
                  pl.BlockSpec((d, tn), lambda j: (0, j)),
                  pl.BlockSpec((1, tn), lambda j: (0, j))],
        out_specs=pl.BlockSpec((m, tn), lambda j: (0, j)),
        out_shape=jax.ShapeDtypeStruct((m, n), F32),
        compiler_params=_cparams("parallel"),
        name="modulation",
    )(cc, w, b.reshape(1, n))
    return out.reshape(m, 9, d)


def _ffn_kernel(x_ref, mod_ref, g_ref, wgu_ref, wd_ref, *out_refs, mod_off, f_chunk, next_off):
    tm = x_ref.shape[0]
    mod = mod_ref[0]
    g = g_ref[...]
    f = wd_ref.shape[0]
    nh = 2 if tm % 32 == 0 else 1
    rows = [slice(i * tm // nh, (i + 1) * tm // nh) for i in range(nh)]
    xs, hs, gu, accs = [None] * nh, [None] * nh, [None] * nh, [None] * nh
    for c0 in range(0, f, f_chunk):
        for i in range(nh):
            if c0 == 0:
                xs[i] = x_ref[rows[i], :]
                hs[i] = _rms_mod(xs[i], g[0:1], mod[mod_off:mod_off + 1], mod[mod_off + 1:mod_off + 2]).astype(BF16)
            gu[i] = (jnp.dot(hs[i], wgu_ref[:, c0:c0 + f_chunk], preferred_element_type=F32),
                     jnp.dot(hs[i], wgu_ref[:, f + c0:f + c0 + f_chunk], preferred_element_type=F32))
        for i in range(nh):
            gate, up = gu[i]
            a = (gate * jax.nn.sigmoid(gate) * up).astype(BF16)
            down = jnp.dot(a, wd_ref[c0:c0 + f_chunk, :], preferred_element_type=F32)
            accs[i] = down if c0 == 0 else accs[i] + down
    for i in range(nh):
        xn = xs[i] + (0.5 * mod[mod_off + 2:mod_off + 3]) * accs[i]
        if next_off == "final":
            ms = jnp.mean(xn * xn, axis=-1, keepdims=True)
            out_refs[0][rows[i], :] = xn * lax.rsqrt(ms + NORM_EPS) * g[1:2]
            continue
        out_refs[0][rows[i], :] = xn
        if next_off is not None:
            out_refs[1][rows[i], :] = _rms_mod(xn, g[1:2], mod[next_off:next_off + 1],
                                               mod[next_off + 1:next_off + 2])


def _ffn(x, mods, mod_row0, rows_per_mod, norm_g2, wgu, wd, layer, which, mod_off, next_off, tm):
    m, d = x.shape
    f = wd.shape[2]
    f_chunk = f // 2 if (f // 2) % LANES == 0 else f
    n_out = 2 if isinstance(next_off, int) else 1
    kern = functools.partial(_ffn_kernel, mod_off=mod_off, f_chunk=f_chunk, next_off=next_off)
    tok = pl.BlockSpec((tm, d), lambda i: (i, 0))
    outs = pl.pallas_call(
        kern,
        grid=(m // tm,),
        in_specs=[tok,
                  pl.BlockSpec((1, 9, d), lambda i: (mod_row0 + (i * tm) // rows_per_mod, 0, 0)),
                  pl.BlockSpec((2, d), lambda i: (0, 0)),
                  pl.BlockSpec((None, None) + wgu.shape[2:], lambda i: (layer, which, 0, 0),
                               pipeline_mode=pl.Buffered(1)),
                  pl.BlockSpec((None, None) + wd.shape[2:], lambda i: (layer, which, 0, 0),
                               pipeline_mode=pl.Buffered(1))],
        out_specs=[tok] * n_out,
        out_shape=[jax.ShapeDtypeStruct((m, d), F32)] * n_out,
        compiler_params=_cparams("parallel"),
        name="ffn_halfstep",
    )(x, mods, norm_g2, wgu, wd)
    return outs if n_out == 2 else (outs[0], None)


def _head_sum(x):
    row = lax.broadcasted_iota(jnp.int32, (LANES, LANES), 0) // RW_HEAD
    col = lax.broadcasted_iota(jnp.int32, (LANES, LANES), 1) // RW_HEAD
    ones_bd = jnp.where(row == col, 1.0, 0.0).astype(BF16)
    hi = x.astype(BF16)
    lo = (x - hi.astype(F32)).astype(BF16)
    outs = []
    for j in range(x.shape[1] // LANES):
        sl = slice(j * LANES, (j + 1) * LANES)
        outs.append(jnp.dot(hi[:, sl], ones_bd, preferred_element_type=F32)
                    + jnp.dot(lo[:, sl], ones_bd, preferred_element_type=F32))
    return jnp.concatenate(outs, axis=1)


def _rwkv_prep_kernel(h_ref, hp_ref, hn_ref, mix_ref, wrkv_ref, w1_ref, w2_ref, a1_ref, a2_ref,
                      g1_ref, g2_ref, vec_ref,
                      k_ref, v_ref, kk_ref, r_ref, lw0_ref, lw1_ref, a0_ref, a1o_ref, bonus_ref, g_ref,
                      *, seq_len, grid_w, halo):
    tm, d = h_ref.shape
    q = d // 4
    hext_all = jnp.concatenate([hp_ref[...], h_ref[...], hn_ref[...]], axis=0)
    offs = (-1, 1, -1, 1) if grid_w is None else (-1, 1, -grid_w, grid_w)
    mix = mix_ref[...]
    vec = vec_ref[...]
    w0c, a0c = vec[0:1], vec[1:2]
    k_k, k_a, r_k = vec[2:3, :d], vec[3:4, :d], vec[4:5, :d]
    nh = 2 if tm % 16 == 0 else 1
    th = tm // nh

    def shift_mix(s):
        i = s["i"]
        hext = hext_all[i * th:i * th + th + 2 * halo]
        h = hext[halo:halo + th]
        t = (pl.program_id(0) * tm + i * th + lax.broadcasted_iota(jnp.int32, (th, 1), 0)) % seq_len
        parts = []
        for qi, o in enumerate(offs):
            piece = hext[halo + o:halo + o + th, qi * q:(qi + 1) * q]
            ok = (t + o >= 0) & (t + o < seq_len)
            if grid_w is not None and abs(o) == 1:
                colp = t % grid_w + o
                ok = ok & (colp >= 0) & (colp < grid_w)
            parts.append(jnp.where(ok, piece, 0.0))
        dx = jnp.concatenate(parts, axis=1) - h
        s["x"] = [(h + dx * mix[j:j + 1]).astype(BF16) for j in range(6)]

    def project(s):
        xr, xw, xk, xv, xa, xg = s.pop("x")
        s["r"] = jnp.dot(xr, wrkv_ref[0], preferred_element_type=F32)
        s["k"] = jnp.dot(xk, wrkv_ref[1], preferred_element_type=F32)
        s["v"] = jnp.dot(xv, wrkv_ref[2], preferred_element_type=F32)
        s["wl"] = jnp.dot(xw, w1_ref[...], preferred_element_type=F32)
        s["al"] = jnp.dot(xa, a1_ref[...], preferred_element_type=F32)
        s["gl"] = jnp.dot(xg, g1_ref[...], preferred_element_type=F32)

    def lora_act(s):
        s["wl"] = jnp.tanh(s["wl"]).astype(BF16)
        s["al"] = s["al"].astype(BF16)
        s["gl"] = jax.nn.sigmoid(s["gl"]).astype(BF16)
        s["kk"] = s["k"] * k_k

    def lora_out(s):
        s["w_pre"] = w0c + jnp.dot(s.pop("wl"), w2_ref[...], preferred_element_type=F32)
        s["a_pre"] = a0c + jnp.dot(s.pop("al"), a2_ref[...], preferred_element_type=F32)
        s["g"] = jnp.dot(s.pop("gl"), g2_ref[...], preferred_element_type=F32)
        s["ss"] = _head_sum(s["kk"] * s["kk"])

    def gates(s):
        s["lw"] = (-math.exp(-0.5)) * jax.nn.sigmoid(s.pop("w_pre"))
        a = jax.nn.sigmoid(s.pop("a_pre"))
        s["a_f"], s["a_b"] = a[:, :d], a[:, d:]
        s["kk"] = s["kk"] * lax.rsqrt(jnp.maximum(s.pop("ss"), 1e-24))
        k_bonus = s["k"] * (1.0 + (0.5 * (s["a_f"] + s["a_b"]) - 1.0) * k_a)
        s["rk"] = s["r"] * k_bonus * r_k

    def bonus_sum(s):
        s["rk"] = _head_sum(s["rk"])

    def store(s):
        rows = slice(s["i"] * th, (s["i"] + 1) * th)
        k_ref[rows, :] = s["k"]
        v_ref[rows, :] = s["v"]
        kk_ref[rows, :] = s["kk"]
        r_ref[rows, :] = s["r"]
        lw0_ref[rows, :] = s["lw"][:, :d]
        lw1_ref[rows, :] = s["lw"][:, d:]
        a0_ref[rows, :] = s["a_f"]
        a1o_ref[rows, :] = s["a_b"]
        bonus_ref[rows, :] = s["rk"] * s["v"]
        g_ref[rows, :] = s["g"].astype(g_ref.dtype)

    stages = (shift_mix, project, lora_act, lora_out, gates, bonus_sum, store)
    halves = [{"i": i} for i in range(nh)]
    for step in range(len(stages) + nh - 1):
        for i, s in enumerate(halves):
            if 0 <= step - i < len(stages):
                stages[step - i](s)


def _rwkv_prep(h, seq_len, grid_w, p, tm):
    (mix, w_rkv, w1c, w2bd, a1c, a2bd, g1p, g2p, vec) = p
    m, d = h.shape
    halo = GRID_W
    nb = tm // halo
    last = m // halo - 1
    full = lambda a: pl.BlockSpec(a.shape, lambda i: (0,) * a.ndim, pipeline_mode=pl.Buffered(1))
    tok = pl.BlockSpec((tm, d), lambda i: (i, 0))
    kern = functools.partial(_rwkv_prep_kernel, seq_len=seq_len, grid_w=grid_w, halo=halo)
    return pl.pallas_call(
        kern,
        grid=(m // tm,),
        in_specs=[tok,
                  pl.BlockSpec((halo, d), lambda i: (jnp.maximum(i * nb - 1, 0), 0)),
                  pl.BlockSpec((halo, d), lambda i: (jnp.minimum((i + 1) * nb, last), 0)),
                  full(mix), full(w_rkv), full(w1c), full(w2bd), full(a1c), full(a2bd),
                  full(g1p), full(g2p), full(vec)],
        out_specs=[tok] * 10,
        out_shape=[jax.ShapeDtypeStruct((m, d), F32)] * 9 + [jax.ShapeDtypeStruct((m, d), BF16)],
        compiler_params=_cparams("parallel"),
        name="rwkv_prep",
    )(h, h, h, mix, w_rkv, w1c, w2bd, a1c, a2bd, g1p, g2p, vec)


def _stack_heads(x):
    lane = lax.broadcasted_iota(jnp.int32, x.shape, 1)
    first = lane < RW_HEAD
    return jnp.concatenate([jnp.where(first, x, 0.0), jnp.where(first, 0.0, x)], axis=0)


def _each(f, *lists):
    return [f(*args) for args in zip(*lists)]


def _mm(a, b):
    return jnp.dot(a, b, preferred_element_type=F32)


def _mm_nt(a, b):
    return lax.dot_general(a, b, (((1,), (1,)), ((), ())), preferred_element_type=F32)


def _mm_tn(a, b):
    return lax.dot_general(a, b, (((0,), (0,)), ((), ())), preferred_element_type=F32)


def _to_bf16(x):
    return x.astype(BF16)


INV_BASE = 16


def _row_blocks(x, size, parity):
    return jnp.concatenate([x[b * size:(b + 1) * size] for b in range(x.shape[0] // size) if b % 2 == parity],
                           axis=0)


def _put_row_blocks(xc, size, parity, base):
    out, k = [], 0
    for b in range(2 * xc.shape[0] // size):
        if b % 2 == parity:
            out.append(xc[k * size:(k + 1) * size])
            k += 1
        else:
            out.append(jnp.zeros((size, xc.shape[1]), xc.dtype) if base is None else base[b * size:(b + 1) * size])
    return jnp.concatenate(out, axis=0)


def _unit_tri_inverse(l_mats, ri, ci, reverse):
    nb = INV_BASE
    n = l_mats[0].shape[0]
    b16 = (ri // nb) == (ci // nb)
    b32 = (ri // (2 * nb)) == (ci // (2 * nb))
    sr = lax.broadcasted_iota(jnp.int32, (nb, n), 0)
    sc = lax.broadcasted_iota(jnp.int32, (nb, n), 1)
    eye_s = jnp.where(sr == sc % nb, 1.0, 0.0)
    strip = lambda m: functools.reduce(lambda a, b: a + b, [m[b * nb:(b + 1) * nb] for b in range(n // nb)])
    block_diag = lambda s: jnp.where(b16, jnp.concatenate([s] * (n // nb), axis=0), 0.0)

    l16 = _each(lambda l: jnp.where(b16, l, 0.0), l_mats)
    l16s = _each(strip, l16)
    xs = _each(lambda s: eye_s - s, l16s)
    ps = _each(lambda s, m: _mm(s.astype(BF16), m.astype(BF16)), l16s, l16)
    for it in range(3):
        pbd = _each(lambda s: block_diag(s).astype(BF16), ps)
        if it < 2:
            both = _each(lambda x_, s, p_: _mm(jnp.concatenate([x_, s], axis=0).astype(BF16), p_), xs, ps, pbd)
            xs = _each(lambda x_, b_: x_ + b_[:nb], xs, both)
            ps = _each(lambda b_: b_[nb:], both)
        else:
            xs = _each(lambda x_, p_: x_ + _mm(x_.astype(BF16), p_), xs, pbd)
    x = _each(block_diag, xs)

    par = 0 if reverse else 1
    for size, off_mask in ((nb, b32 & (~b16)), (2 * nb, ~b32)):
        lc = _each(lambda l: _row_blocks(jnp.where(off_mask, l, 0.0), size, par).astype(BF16), l_mats)
        xb = _each(_to_bf16, x)
        t = _each(lambda l_, x_: _put_row_blocks(_mm(l_, x_).astype(BF16), size, par, None), lc, xb)
        xc = _each(lambda x_: _row_blocks(x_, size, par), x)
        r = _each(lambda x_, t_: x_ - _mm(x_.astype(BF16), t_), xc, t)
        x = _each(lambda r_, x_: _put_row_blocks(r_, size, par, x_), r, x)
    return x


def _chunk_cumsum(x, reverse):
    c = x.shape[0]
    row = lax.broadcasted_iota(jnp.int32, x.shape, 0)
    s = 1
    while s < c:
        if reverse:
            x = x + jnp.where(row < c - s, pltpu.roll(x, c - s, axis=0), 0.0)
        else:
            x = x + jnp.where(row >= s, pltpu.roll(x, s, axis=0), 0.0)
        s *= 2
    return x


def _chunk_local(lw, k, v, kk, r, a, k_a, reverse):
    c = lw[0].shape[0]
    n = 2 * c
    kd = _each(lambda k_, a_, ka_: k_ * (1.0 + (a_ - 1.0) * ka_), k, a, k_a)
    ka = _each(lambda kk_, a_: kk_ * a_, kk, a)
    cum = _each(lambda x: _chunk_cumsum(x, reverse), lw)
    tot = _each(lambda x: x[0:1] if reverse else x[c - 1:c], cum)
    e_neg = _each(lambda x: jnp.exp(-x), cum)
    e_end = _each(lambda t_, x: jnp.exp(t_ - x), tot, cum)
    stack_b = lambda x: _stack_heads(x).astype(BF16)
    alpha = _each(lambda kk_, c_, l_: stack_b(kk_ * jnp.exp(c_ - l_)), kk, cum, lw)
    beta = _each(lambda x, e: stack_b(x * e), ka, e_neg)
    kappa = _each(lambda x, e: stack_b(x * e), kd, e_neg)
    rho = _each(lambda r_, c_: _stack_heads(r_ * jnp.exp(c_)), r, cum)
    kappa_e = _each(lambda x, e: stack_b(x * e), kd, e_end)
    beta_e = _each(lambda x, e: stack_b(x * e), ka, e_end)
    v_st = _each(stack_b, v)
    w_end = _each(jnp.exp, tot)

    sc = _each(lambda al, rh, be, kp: _mm_nt(jnp.concatenate([al, rh.astype(BF16)], axis=0),
                                             jnp.concatenate([be, kp], axis=0)), alpha, rho, beta, kappa)
    ri = lax.broadcasted_iota(jnp.int32, (n, n), 0)
    ci = lax.broadcasted_iota(jnp.int32, (n, n), 1)
    strict = (ci > ri) if reverse else (ci < ri)
    incl = (ci >= ri) if reverse else (ci <= ri)
    l_mat = _each(lambda x: jnp.where(strict, x[:n, :n], 0.0), sc)
    apk = _each(lambda x: jnp.concatenate([jnp.where(strict, x[:n, n:], 0.0), jnp.where(incl, x[n:, n:], 0.0)],
                                          axis=0).astype(BF16), sc)
    pb = _each(lambda x: jnp.where(incl, x[n:, :n], 0.0).astype(BF16), sc)

    apkv = _each(_mm, apk, v_st)
    x = _each(_to_bf16, _unit_tri_inverse(l_mat, ri, ci, reverse))
    uwb = _each(lambda x_, kv, al: _mm(x_, jnp.concatenate([kv[:n].astype(BF16), al], axis=1)).astype(BF16),
                x, apkv, alpha)
    yr = _each(lambda kv, rh, pb_, uw: jnp.concatenate([kv[n:], rh], axis=1) - _mm(pb_, uw), apkv, rho, pb, uwb)
    uq = _each(_mm_tn, uwb, beta_e)
    n_t = _each(lambda v_, ke, uq_: _mm_tn(v_, ke) - uq_[:n], v_st, kappa_e, uq)
    q = _each(lambda uq_: uq_[n:].astype(BF16), uq)
    return list(zip(yr, n_t, q, w_end))


def _chunk_state(s, local):
    n = local[0][0].shape[0]
    c = n // 2
    sb = _each(_to_bf16, s)
    y_st = _each(lambda lc, sb_: lc[0][:, :n] + _mm_nt(lc[0][:, n:].astype(BF16), sb_), local, sb)
    y = _each(lambda x_: x_[:c] + x_[c:], y_st)
    s_new = _each(lambda s_, sb_, lc: s_ * lc[3] - _mm(sb_, lc[2]) + lc[1], s, sb, local)
    return s_new, y


SCAN_SLABS = 8
SCAN_PAIR = 4


def _scan_kernel(k_ref, v_ref, kk_ref, r_ref, lw_ref, a_ref, ka_ref, s0_ref, y_ref, sout_ref, s_scr,
                 *, reverse, chunk):
    tb = k_ref.shape[1]
    nslab = s_scr.shape[0]
    nchunk = tb // chunk
    step = pl.program_id(2)

    @pl.when(step == 0)
    def _():
        s_scr[...] = s0_ref[0]

    lanes = [slice(g * LANES, (g + 1) * LANES) for g in range(nslab)]
    pair = math.gcd(SCAN_PAIR, nchunk)

    def body(i, carry):
        cis = [(nchunk - 1 - (i * pair + p)) if reverse else (i * pair + p) for p in range(pair)]
        sls = [pl.ds(pl.multiple_of(ci * chunk, chunk), chunk) for ci in cis]
        get = lambda ref: [ref[0, sl, ln] for sl in sls for ln in lanes]
        local = _chunk_local(get(lw_ref), get(k_ref), get(v_ref), get(kk_ref), get(r_ref), get(a_ref),
                             [ka_ref[:, ln] for _ in sls for ln in lanes], reverse)
        s = [s_scr[g] for g in range(nslab)]
        for p, sl in enumerate(sls):
            s, y = _chunk_state(s, local[p * nslab:(p + 1) * nslab])
            for g in range(nslab):
                y_ref[0, sl, lanes[g]] = y[g]
        for g in range(nslab):
            s_scr[g] = s[g]
        return carry

    lax.fori_loop(0, nchunk // pair, body, 0)

    @pl.when(step == pl.num_programs(2) - 1)
    def _():
        sout_ref[0] = s_scr[...]


def _rwkv_scan(k, v, kk, r, lw, a, k_a, s0, reverse, tb):
    b, t, d = k.shape
    g = SCAN_SLABS
    gw = g * LANES
    nt = t // tb
    tmap = (lambda bi, j, c: (bi, nt - 1 - c, j)) if reverse else (lambda bi, j, c: (bi, c, j))
    tok = pl.BlockSpec((1, tb, gw), tmap)
    st = pl.BlockSpec((1, g, LANES, LANES), lambda bi, j, c: (bi, j, 0, 0))
    kern = functools.partial(_scan_kernel, reverse=reverse, chunk=CHUNK)
    return pl.pallas_call(
        kern,
        grid=(b, d // gw, nt),
        in_specs=[tok] * 6 + [pl.BlockSpec((1, gw), lambda bi, j, c: (0, j)), st],
        out_specs=[tok, st],
        out_shape=[jax.ShapeDtypeStruct((b, t, d), F32),
                   jax.ShapeDtypeStruct((b, d // LANES, LANES, LANES), F32)],
        scratch_shapes=[pltpu.VMEM((g, LANES, LANES), F32)],
        compiler_params=_cparams("parallel", "parallel", "arbitrary"),
        name="rwkv_scan_bwd" if reverse else "rwkv_scan_fwd",
    )(k, v, kk, r, lw, a, k_a, s0)


def _rwkv_out_kernel(x_ref, yf_ref, yb_ref, bonus_ref, g_ref, mod_ref, ln_ref, wo_ref, o_ref):
    y = yf_ref[...] + yb_ref[...]
    ln = ln_ref[...]
    inv_n = 1.0 / RW_HEAD
    mu = _head_sum(y) * inv_n
    yc = y - mu
    var = _head_sum(yc * yc) * inv_n
    yn = yc * lax.rsqrt(var + GN_EPS) * ln[0:1] + ln[1:2]
    z = ((yn + bonus_ref[...]) * g_ref[...]).astype(BF16)
    out = jnp.dot(z, wo_ref[...], preferred_element_type=F32)
    o_ref[...] = x_ref[...] + mod_ref[0][5:6] * out


def _rwkv_out(x, yf, yb, bonus, g, mods, rows_per_mod, ln, w_o, tm):
    m, d = x.shape
    tok = pl.BlockSpec((tm, d), lambda i: (i, 0))
    return pl.pallas_call(
        _rwkv_out_kernel,
        grid=(m // tm,),
        in_specs=[tok] * 5 + [pl.BlockSpec((1, 9, d), lambda i: ((i * tm) // rows_per_mod, 0, 0)),
                              pl.BlockSpec((2, d), lambda i: (0, 0)),
                              pl.BlockSpec(w_o.shape, lambda i: (0, 0), pipeline_mode=pl.Buffered(1))],
        out_specs=tok,
        out_shape=jax.ShapeDtypeStruct((m, d), F32),
        compiler_params=_cparams("parallel"),
        name="rwkv_out",
    )(x, yf, yb, bonus, g, mods, ln, w_o)


def _hy_in_kernel(h_ref, hp_ref, hn_ref, w_ref, cw_ref, cb_ref, x1_ref, x2_ref, v_ref, *, seq_len):
    tm, d = h_ref.shape
    halo = hp_ref.shape[0]
    cw = cw_ref[...]
    cb = cb_ref[...]
    nh = 2 if tm % (4 * halo) == 0 else 1
    th = tm // nh
    pes = []
    for i in range(nh):
        before = hp_ref[...] if i == 0 else h_ref[i * th - halo:i * th, :]
        after = hn_ref[...] if i == nh - 1 else h_ref[(i + 1) * th:(i + 1) * th + halo, :]
        hext = jnp.concatenate([before, h_ref[i * th:(i + 1) * th, :], after], axis=0).astype(BF16)
        pes.append(jnp.dot(hext, w_ref[...], preferred_element_type=F32))
    for i, pe in enumerate(pes):
        t = (pl.program_id(0) * tm + i * th + lax.broadcasted_iota(jnp.int32, (th, 1), 0)) % seq_len
        prev = jnp.where(t >= 1, pe[halo - 1:halo - 1 + th], 0.0)
        nxt = jnp.where(t + 1 < seq_len, pe[halo + 1:halo + 1 + th], 0.0)
        u = prev * cw[0:1] + pe[halo:halo + th] * cw[1:2] + nxt * cw[2:3] + cb
        rows = slice(i * th, (i + 1) * th)
        x1_ref[rows, :] = u[:, :d]
        x2_ref[rows, :] = u[:, d:2 * d]
        v_ref[rows, :] = u[:, 2 * d:]


def _hy_in(h, seq_len, w_in, conv_w, conv_b, tm):
    m, d = h.shape
    halo = 8
    nb = tm // halo
    last = m // halo - 1
    tok = pl.BlockSpec((tm, d), lambda i: (i, 0))
    full = lambda a: pl.BlockSpec(a.shape, lambda i: (0,) * a.ndim, pipeline_mode=pl.Buffered(1))
    return pl.pallas_call(
        functools.partial(_hy_in_kernel, seq_len=seq_len),
        grid=(m // tm,),
        in_specs=[tok,
                  pl.BlockSpec((halo, d), lambda i: (jnp.maximum(i * nb - 1, 0), 0)),
                  pl.BlockSpec((halo, d), lambda i: (jnp.minimum((i + 1) * nb, last), 0)),
                  full(w_in), full(conv_w), full(conv_b)],
        out_specs=[tok] * 3,
        out_shape=[jax.ShapeDtypeStruct((m, d), F32)] * 3,
        compiler_params=_cparams("parallel"),
        name="hyena_in",
    )(h, h, h, w_in, conv_w, conv_b)


def _hy_filter_kernel(z_ref, w1_ref, b1_ref, fr_ref, w2_ref, b2_ref, w3_ref, dl_ref, o_ref):
    hp = lax.Precision.HIGHEST
    z = z_ref[...]
    fr = fr_ref[...]
    hid = jnp.sin(fr[0:1] * (jnp.dot(z, w1_ref[...], precision=hp, preferred_element_type=F32) + b1_ref[...]))
    hid = jnp.sin(fr[1:2] * (jnp.dot(hid, w2_ref[...], precision=hp, preferred_element_type=F32) + b2_ref[...]))
    filt = jnp.dot(hid, w3_ref[...], precision=hp, preferred_element_type=F32)
    o_ref[...] = filt * jnp.exp(-z[:, 0:1] * jnp.abs(dl_ref[...]))


def _hy_filters(seq_len, f_w1, f_b1, f_freq, f_w2, f_b2, f_w3, deltas):
    t = jnp.linspace(0.0, 1.0, seq_len, dtype=F32)[:, None]
    ang = ((2 * math.pi / seq_len) * jnp.arange(seq_len, dtype=F32)[:, None]
           * jnp.linspace(1e-4, HY_BANDS - 1, HY_BANDS, dtype=F32)[None])
    z = jnp.concatenate([t, jnp.cos(ang), -jnp.sin(ang)], axis=-1)
    emb, hid = f_w1.shape
    z = jnp.pad(z, ((0, 0), (0, LANES - emb)))
    w1 = jnp.pad(f_w1, ((0, LANES - emb), (0, 0)))
    n_out = f_w3.shape[1]
    tl = min(seq_len, 512)
    full = lambda a: pl.BlockSpec(a.shape, lambda i: (0,) * a.ndim)
    args = (w1, f_b1.reshape(1, hid), f_freq, f_w2, f_b2.reshape(1, hid), f_w3, deltas.reshape(1, n_out))
    return pl.pallas_call(
        _hy_filter_kernel,
        grid=(seq_len // tl,),
        in_specs=[pl.BlockSpec((tl, LANES), lambda i: (i, 0))] + [full(a) for a in args],
        out_specs=pl.BlockSpec((tl, n_out), lambda i: (i, 0)),
        out_shape=jax.ShapeDtypeStruct((seq_len, n_out), F32),
        compiler_params=_cparams("parallel"),
        name="hyena_filters",
    )(z, *args)


DFT_UNROLL = 8


def _rdft_tables(seq_len):
    big = 2 * seq_len
    n = int(round(math.sqrt(big)))
    assert n * n == big and n % 32 == 0, "sequence length must give a square, tile-aligned DFT"
    kh = n // 2 + 1
    kp = -(-kh // 8) * 8
    k1 = np.arange(kp)[None, :, None]
    n1 = np.arange(n // 2)[None, None, :]
    n2 = np.arange(n)[:, None, None]
    ang = -2.0 * np.pi * ((k1 * (n * n1 + n2)) % big) / big
    live = (k1 < kh).astype(np.float64)
    e_in = np.concatenate([np.cos(ang) * live, np.sin(ang) * live], axis=1)
    wgt = np.where((k1 == 0) | (k1 == n // 2), 1.0, 2.0) * live
    e_out = np.transpose(np.concatenate([np.cos(ang) * wgt, np.sin(ang) * wgt], axis=1), (0, 2, 1))
    a2 = -2.0 * np.pi * ((np.arange(n)[:, None] * np.arange(n)[None, :]) % n) / n
    fr, fi = np.cos(a2), np.sin(a2)
    f_fwd = np.block([[fr, -fi], [fi, fr]])
    f_inv = np.block([[fr, fi], [-fi, fr]])
    kg = max(g for g in range(1, 21) if kh % g == 0)
    cast = lambda a: jnp.asarray(a, dtype=F32).astype(BF16)
    return dict(n=n, kh=kh, kp=kp, kg=kg, e_in=cast(e_in), e_out=cast(e_out), f_fwd=cast(f_fwd), f_inv=cast(f_inv))


ROW_PAD = 8


def _rdft_first_stage(u_ref, e_in_ref, x_pad, a_scr, n, kp):
    pitch = n + ROW_PAD
    for n1 in range(n // 2):
        x_pad[n1 * pitch:n1 * pitch + n, :] = u_ref[0, n1 * n:(n1 + 1) * n, :]

    def body(i, carry):
        n2s = [i * DFT_UNROLL + j for j in range(DFT_UNROLL)]
        xs = [x_pad[pl.ds(n2, n // 2, stride=pitch), :].astype(BF16) for n2 in n2s]
        acc = [_mm(e_in_ref[n2], x) for n2, x in zip(n2s, xs)]
        for n2, a in zip(n2s, acc):
            a_scr[0, pl.ds(n2, kp, stride=pitch), :] = a[:kp]
            a_scr[1, pl.ds(n2, kp, stride=pitch), :] = a[kp:]
        return carry

    lax.fori_loop(0, n // DFT_UNROLL, body, 0)


def _rdft_second_stage(a_scr, k1s, n, f_fwd):
    pitch = n + ROW_PAD
    cols = []
    for k1 in k1s:
        rows = pl.ds(pl.multiple_of(k1 * pitch, 8), n)
        cols.append(jnp.concatenate([a_scr[0, rows, :], a_scr[1, rows, :]], axis=0).astype(BF16))
    return [_mm(f_fwd, c) for c in cols]


def _hy_conv_kernel(u_ref, gate_ref, h_ref, ein_ref, eout_ref, ff_ref, fi_ref, bias_ref, o_ref, x_pad, a_scr,
                    *, n, kp, kh, mg):
    _rdft_first_stage(u_ref, ein_ref, x_pad, a_scr, n, kp)

    f_fwd, f_inv = ff_ref[...], fi_ref[...]

    pitch = n + ROW_PAD

    def mid(i, carry):
        for j0 in range(0, mg, DFT_UNROLL):
            k1s = [i * mg + j for j in range(j0, min(j0 + DFT_UNROLL, mg))]
            xs = _rdft_second_stage(a_scr, k1s, n, f_fwd)
            ys = []
            for k1, x in zip(k1s, xs):
                xr, xi = x[:n], x[n:]
                hr, hi = h_ref[0, k1], h_ref[1, k1]
                ys.append(jnp.concatenate([xr * hr - xi * hi, xr * hi + xi * hr], axis=0).astype(BF16))
            zs = [_mm(f_inv, y) for y in ys]
            for k1, z in zip(k1s, zs):
                rows = pl.ds(pl.multiple_of(k1 * pitch, 8), n)
                a_scr[0, rows, :] = z[:n]
                a_scr[1, rows, :] = z[n:]
        return carry

    lax.fori_loop(0, kh // mg, mid, 0)

    inv_n = 1.0 / (n * n)
    bias = bias_ref[...]

    def last(i, carry):
        n2s = [i * DFT_UNROLL + j for j in range(DFT_UNROLL)]
        zc = []
        for n2 in n2s:
            rows = pl.ds(n2, kp, stride=pitch)
            zc.append(jnp.concatenate([a_scr[0, rows, :], a_scr[1, rows, :]], axis=0).astype(BF16))
        ys = [_mm(eout_ref[n2], z) * inv_n for n2, z in zip(n2s, zc)]
        for n2, y in zip(n2s, ys):
            x_pad[pl.ds(n2, n // 2, stride=pitch), :] = y
        return carry

    lax.fori_loop(0, n // DFT_UNROLL, last, 0)
    for n1 in range(n // 2):
        rows = slice(n1 * n, (n1 + 1) * n)
        y = x_pad[n1 * pitch:n1 * pitch + n, :]
        o_ref[0, rows, :] = gate_ref[0, rows, :] * (y + u_ref[0, rows, :] * bias)


def _hy_conv(u, gate, spec, f_idx, bias, tb):
    b, l, d = u.shape
    n, kh, kp, mg = tb["n"], tb["kh"], tb["kp"], tb["kg"]
    nd = d // LANES
    once = dict(pipeline_mode=pl.Buffered(1))
    tok = lambda bufs: pl.BlockSpec((1, l, LANES), lambda c, bi: (bi, 0, c), **bufs)
    full = lambda a: pl.BlockSpec(a.shape, lambda c, bi: (0,) * a.ndim, **once)
    return pl.pallas_call(
        functools.partial(_hy_conv_kernel, n=n, kp=kp, kh=kh, mg=mg),
        grid=(nd, b),
        in_specs=[tok({}), tok(once),
                  pl.BlockSpec((2, kh, n, LANES), lambda c, bi: (0, 0, 0, f_idx * nd + c), **once),
                  full(tb["e_in"]), full(tb["e_out"]), full(tb["f_fwd"]), full(tb["f_inv"]),
                  pl.BlockSpec((1, LANES), lambda c, bi: (0, c))],
        out_specs=tok({}),
        out_shape=jax.ShapeDtypeStruct((b, l, d), F32),
        scratch_shapes=[pltpu.VMEM((n // 2 * (n + ROW_PAD), LANES), F32), pltpu.VMEM((2, kp * (n + ROW_PAD), LANES), F32)],
        compiler_params=_cparams("parallel", "parallel"),
        name="hyena_conv",
    )(u, gate, spec, tb["e_in"], tb["e_out"], tb["f_fwd"], tb["f_inv"], bias.reshape(1, d))


def _hy_spec_kernel(u_ref, ein_ref, ff_ref, h_ref, x_pad, a_scr, *, n, kp, kg):
    step = pl.program_id(1)

    @pl.when(step == 0)
    def _():
        _rdft_first_stage(u_ref, ein_ref, x_pad, a_scr, n, kp)

    f_fwd = ff_ref[...]
    for j0 in range(0, kg, DFT_UNROLL):
        js = list(range(j0, min(j0 + DFT_UNROLL, kg)))
        xs = _rdft_second_stage(a_scr, [step * kg + j for j in js], n, f_fwd)
        for j, x in zip(js, xs):
            h_ref[0, j] = x[:n]
            h_ref[1, j] = x[n:]


def _hy_spectrum(filt, tb):
    _, l, dx = filt.shape
    n, kh, kp, kg = tb["n"], tb["kh"], tb["kp"], tb["kg"]
    full = lambda a: pl.BlockSpec(a.shape, lambda c, g: (0,) * a.ndim, pipeline_mode=pl.Buffered(1))
    return pl.pallas_call(
        functools.partial(_hy_spec_kernel, n=n, kp=kp, kg=kg),
        grid=(dx // LANES, kh // kg),
        in_specs=[pl.BlockSpec((1, l, LANES), lambda c, g: (0, 0, c)), full(tb["e_in"]), full(tb["f_fwd"])],
        out_specs=pl.BlockSpec((2, kg, n, LANES), lambda c, g: (0, g, 0, c)),
        out_shape=jax.ShapeDtypeStruct((2, kh, n, dx), F32),
        scratch_shapes=[pltpu.VMEM((n // 2 * (n + ROW_PAD), LANES), F32), pltpu.VMEM((2, kp * (n + ROW_PAD), LANES), F32)],
        compiler_params=_cparams("parallel", "arbitrary"),
        name="hyena_filter_spectrum",
    )(filt, tb["e_in"], tb["f_fwd"])


def _mix_out_kernel(x_ref, z_ref, mod_ref, w_ref, o_ref):
    out = jnp.dot(z_ref[...].astype(BF16), w_ref[...], preferred_element_type=F32)
    o_ref[...] = x_ref[...] + mod_ref[0][5:6] * out


def _mix_out(x, z, mods, rows_per_mod, w_out, tm):
    m, d = x.shape
    tok = pl.BlockSpec((tm, d), lambda i: (i, 0))
    return pl.pallas_call(
        _mix_out_kernel,
        grid=(m // tm,),
        in_specs=[tok, tok, pl.BlockSpec((1, 9, d), lambda i: ((i * tm) // rows_per_mod, 0, 0)),
                  pl.BlockSpec(w_out.shape, lambda i: (0, 0), pipeline_mode=pl.Buffered(1))],
        out_specs=tok,
        out_shape=jax.ShapeDtypeStruct((m, d), F32),
        compiler_params=_cparams("parallel"),
        name="hyena_out",
    )(x, z, mods, w_out)


def _hyena_mixer(hl, batch, hp, tm):
    (w_in, conv_w, conv_b, f_w1, f_b1, f_freq, f_w2, f_b2, f_w3, deltas, bias) = hp
    m, d = hl.shape
    seq_len = m // batch
    x1, x2, v = _hy_in(hl, seq_len, w_in.astype(BF16), conv_w, conv_b.reshape(1, -1), tm)
    three = lambda a: a.reshape(batch, seq_len, d)
    x1, x2, v = three(x1), three(x2), three(v)
    tables = _rdft_tables(seq_len)
    filt = _hy_filters(seq_len, f_w1, f_b1, f_freq, f_w2, f_b2, f_w3, deltas)
    spec = _hy_spectrum(filt[None], tables)
    z = _hy_conv(v, x1, spec, 0, bias[0], tables)
    z = _hy_conv(z, x2, spec, 1, bias[1], tables)
    return z.reshape(m, d)


def _rwkv_weights(mix, w_rkv, w0, w1, w2, a0, a1, a2, g1, g2, k_k, k_a, r_k):
    d = mix.shape[-1]
    lora = w1.shape[-1]
    zero = jnp.zeros((lora, d), F32)
    blockdiag = lambda m: jnp.concatenate([jnp.concatenate([m[0], zero], axis=1),
                                           jnp.concatenate([zero, m[1]], axis=1)], axis=0)
    gl = g1.shape[-1]
    glp = -(-gl // LANES) * LANES
    vec = jnp.zeros((8, 2 * d), F32)
    vec = vec.at[0].set(w0.reshape(-1)).at[1].set(a0.reshape(-1))
    vec = vec.at[2, :d].set(k_k).at[3, :d].set(k_a).at[4, :d].set(r_k.reshape(-1))
    return (mix, w_rkv.astype(BF16),
            jnp.concatenate([w1[0], w1[1]], axis=1).astype(BF16), blockdiag(w2).astype(BF16),
            jnp.concatenate([a1[0], a1[1]], axis=1).astype(BF16), blockdiag(a2).astype(BF16),
            jnp.pad(g1, ((0, 0), (0, glp - gl))).astype(BF16),
            jnp.pad(g2, ((0, glp - gl), (0, 0))).astype(BF16), vec)


def _rwkv_mixer(hl, hc, batch, wts, k_a, tm, tb):
    d = hl.shape[1]
    t_lat, t_ctx = hl.shape[0] // batch, hc.shape[0] // batch
    k_a = k_a.reshape(1, d)
    s0 = jnp.zeros((batch, d // LANES, LANES, LANES), F32)
    three = lambda arrs, t: [a.reshape(batch, t, d) for a in arrs]
    kc, vc, kkc, rc, lw0c, lw1c, a0c, a1c, _, _ = _rwkv_prep(hc, t_ctx, None, wts, min(tm, t_ctx))
    kc, vc, kkc, rc, lw0c, lw1c, a0c, a1c = three((kc, vc, kkc, rc, lw0c, lw1c, a0c, a1c), t_ctx)
    tbc = min(tb, t_ctx)
    _, s_f = _rwkv_scan(kc, vc, kkc, rc, lw0c, a0c, k_a, s0, False, tbc)
    _, s_b = _rwkv_scan(kc, vc, kkc, rc, lw1c, a1c, k_a, s0, True, tbc)
    k, v, kk, r, lw0, lw1, a0, a1, bonus, g = _rwkv_prep(hl, t_lat, GRID_W, wts, tm)
    k, v, kk, r, lw0, lw1, a0, a1 = three((k, v, kk, r, lw0, lw1, a0, a1), t_lat)
    yf, _ = _rwkv_scan(k, v, kk, r, lw0, a0, k_a, s_f, False, tb)
    yb, _ = _rwkv_scan(k, v, kk, r, lw1, a1, k_a, s_b, True, tb)
    return yf.reshape(-1, d), yb.reshape(-1, d), bonus, g


TOKEN_TILE = 512
PREP_TILE = 256
SCAN_BLOCK = 512


def kernel(x, c, ctx, c_ctx, mod_w, mod_b, norm_g, ffn_w_gu, ffn_w_down, rw_mix, rw_w_rkv, rw_w_o, rw_w0, rw_w1, rw_w2, rw_a0, rw_a1, rw_a2, rw_g1, rw_g2, rw_k_k, rw_k_a, rw_r_k, rw_ln_g, rw_ln_b, hy_w_in, hy_conv_w, hy_conv_b, hy_f_w1, hy_f_b1, hy_f_freq, hy_f_w2, hy_f_b2, hy_f_w3, hy_deltas, hy_bias, hy_w_out, final_g):
    batch, seq, d = x.shape
    t_ctx = ctx.shape[1]
    depth = mod_w.shape[0]
    assert depth == 2 and rw_mix.shape[0] == 1 and hy_w_in.shape[0] == 1
    assert seq % TOKEN_TILE == 0 and seq % GRID_W == 0 and t_ctx % CHUNK == 0 and batch + 1 <= 8

    cc = jnp.concatenate([c, c_ctx[None], jnp.zeros((8 - batch - 1, d), F32)], axis=0)
    mods0 = _modulation(cc, mod_w[0], mod_b[0])
    mods1 = _modulation(cc, mod_w[1], mod_b[1])
    wgu = ffn_w_gu.astype(BF16)
    wd = ffn_w_down.astype(BF16)
    xl = x.reshape(batch * seq, d)
    xc = ctx.reshape(batch * t_ctx, d)
    tm = TOKEN_TILE
    tmc = min(tm, t_ctx)

    xl, hl = _ffn(xl, mods0, 0, seq, norm_g[0, 0:2], wgu, wd, 0, 0, 0, 3, tm)
    _, hc = _ffn(xc, mods0, batch, batch * t_ctx, norm_g[0, 0:2], wgu, wd, 0, 0, 0, 3, tmc)
    wts = _rwkv_weights(rw_mix[0], rw_w_rkv[0], rw_w0[0], rw_w1[0], rw_w2[0], rw_a0[0], rw_a1[0], rw_a2[0],
                        rw_g1[0], rw_g2[0], rw_k_k[0], rw_k_a[0], rw_r_k[0])
    yf, yb, bonus, g = _rwkv_mixer(hl, hc, batch, wts, rw_k_a[0], PREP_TILE, SCAN_BLOCK)
    xl = _rwkv_out(xl, yf, yb, bonus, g, mods0, seq, jnp.stack([rw_ln_g[0], rw_ln_b[0]]),
                   rw_w_o[0].astype(BF16), tm)
    xl, _ = _ffn(xl, mods0, 0, seq, jnp.stack([norm_g[0, 2], final_g]), wgu, wd, 0, 1, 6, None, tm)

    xl, hl = _ffn(xl, mods1, 0, seq, norm_g[1, 0:2], wgu, wd, 1, 0, 0, 3, tm)
    hp = (hy_w_in[0], hy_conv_w[0], hy_conv_b[0], hy_f_w1[0], hy_f_b1[0], hy_f_freq[0], hy_f_w2[0],
          hy_f_b2[0], hy_f_w3[0], hy_deltas[0], hy_bias[0])
    z = _hyena_mixer(hl, batch, hp, tm)
    xl = _mix_out(xl, z, mods1, seq, hy_w_out[0].astype(BF16), tm)
    out, _ = _ffn(xl, mods1, 0, seq, jnp.stack([norm_g[1, 2], final_g]), wgu, wd, 1, 1, 6, "final", tm)
    return out.reshape(batch, seq, d)
```

```python
import functools
import math

import jax
import jax.numpy as jnp
import numpy as np
from jax import lax
from jax.experimental import pallas as pl
from jax.experimental.pallas import tpu as pltpu

F32 = jnp.float32
BF16 = jnp.bfloat16

GRID_W = 64
RW_HEAD = 64
NORM_EPS = 1e-6
GN_EPS = 64e-5
HY_BANDS = 16

LANES = 128
CHUNK = 64
VMEM_LIMIT = 56 * 1024 * 1024


def _cparams(*sem):
    return pltpu.CompilerParams(dimension_semantics=sem, vmem_limit_bytes=VMEM_LIMIT)


def _bdot(a, b):
    return jnp.dot(a.astype(BF16), b.astype(BF16), preferred_element_type=F32)


def _rms_mod(x, g, shift, scale):
    ms = jnp.mean(x * x, axis=-1, keepdims=True)
    return (x * lax.rsqrt(ms + NORM_EPS) * g) * (1.0 + scale) + shift


def _mod_kernel(c_ref, w_ref, b_ref, o_ref):
    c = c_ref[...]
    o_ref[...] = _bdot(c * jax.nn.sigmoid(c), w_ref[...]) + b_ref[...]


def _modulation(cc, w, b):
    m, d = cc.shape
    n = w.shape[1]
    tn = 1152
    out = pl.pallas_call(
        _mod_kernel,
        grid=(n // tn,),
        in_specs=[pl.BlockSpec((m, d), lambda j: (0, 0)),
                  pl.BlockSpec((d, tn), lambda j: (0, j)),
                  pl.BlockSpec((1, tn), lambda j: (0, j))],
        out_specs=pl.BlockSpec((m, tn), lambda j: (0, j)),
        out_shape=jax.ShapeDtypeStruct((m, n), F32),
        compiler_params=_cparams("parallel"),
        name="modulation",
    )(cc, w, b.reshape(1, n))
    return out.reshape(m, 9, d)


def _mixer_residual(pre, extras, rows, x, gate):
    if pre == "proj":
        z_ref, wo_ref = extras
        z = z_ref[rows, :].astype(BF16)
    else:
        yf_ref, yb_ref, bonus_ref, gt_ref, ln_ref, wo_ref = extras
        y = yf_ref[rows, :] + yb_ref[rows, :]
        ln = ln_ref[...]
        inv_n = 1.0 / RW_HEAD
        yc = y - _head_sum(y) * inv_n
        var = _head_sum(yc * yc) * inv_n
        yn = yc * lax.rsqrt(var + GN_EPS) * ln[0:1] + ln[1:2]
        z = ((yn + bonus_ref[rows, :]) * gt_ref[rows, :]).astype(BF16)
    return x + gate * jnp.dot(z, wo_ref[...], preferred_element_type=F32)


def _ffn_kernel(*refs, mod_off, f_chunk, next_off, pre):
    n_extra = {None: 0, "proj": 2, "rwkv": 6}[pre]
    x_ref, extras = refs[0], refs[1:1 + n_extra]
    mod_ref, g_ref, wgu_ref, wd_ref = refs[1 + n_extra:5 + n_extra]
    out_refs = refs[5 + n_extra:]
    tm = x_ref.shape[0]
    mod = mod_ref[0]
    g = g_ref[...]
    f = wd_ref.shape[0]
    nh = 2 if tm % 32 == 0 else 1
    rows = [slice(i * tm // nh, (i + 1) * tm // nh) for i in range(nh)]
    xs, hs, gu, accs = [None] * nh, [None] * nh, [None] * nh, [None] * nh
    for c0 in range(0, f, f_chunk):
        for i in range(nh):
            if c0 == 0:
                xs[i] = x_ref[rows[i], :]
                if pre is not None:
                    xs[i] = _mixer_residual(pre, extras, rows[i], xs[i], mod[5:6])
                hs[i] = _rms_mod(xs[i], g[0:1], mod[mod_off:mod_off + 1], mod[mod_off + 1:mod_off + 2]).astype(BF16)
            gu[i] = (jnp.dot(hs[i], wgu_ref[:, c0:c0 + f_chunk], preferred_element_type=F32),
                     jnp.dot(hs[i], wgu_ref[:, f + c0:f + c0 + f_chunk], preferred_element_type=F32))
        for i in range(nh):
            gate, up = gu[i]
            a = (gate * jax.nn.sigmoid(gate) * up).astype(BF16)
            down = jnp.dot(a, wd_ref[c0:c0 + f_chunk, :], preferred_element_type=F32)
            accs[i] = down if c0 == 0 else accs[i] + down
    for i in range(nh):
        xn = xs[i] + (0.5 * mod[mod_off + 2:mod_off + 3]) * accs[i]
        if next_off == "final":
            ms = jnp.mean(xn * xn, axis=-1, keepdims=True)
            out_refs[0][rows[i], :] = xn * lax.rsqrt(ms + NORM_EPS) * g[1:2]
            continue
        out_refs[0][rows[i], :] = xn
        if next_off is not None:
            out_refs[1][rows[i], :] = _rms_mod(xn, g[1:2], mod[next_off:next_off + 1],
                                               mod[next_off + 1:next_off + 2])


def _ffn(x, mods, mod_row0, rows_per_mod, norm_g2, wgu, wd, layer, which, mod_off, next_off, tm, pre=None):
    m, d = x.shape
    f = wd.shape[2]
    f_chunk = f // 2 if (f // 2) % LANES == 0 else f
    n_out = 2 if isinstance(next_off, int) else 1
    kind = None if pre is None else pre[0]
    kern = functools.partial(_ffn_kernel, mod_off=mod_off, f_chunk=f_chunk, next_off=next_off, pre=kind)
    tok = pl.BlockSpec((tm, d), lambda i: (i, 0))
    extras, extra_specs = [], []
    for a in (() if pre is None else pre[1:]):
        extras.append(a)
        whole = pl.BlockSpec(a.shape, lambda i: (0, 0), pipeline_mode=pl.Buffered(1))
        extra_specs.append(tok if a.shape == (m, d) else whole)
    outs = pl.pallas_call(
        kern,
        grid=(m // tm,),
        in_specs=[tok] + extra_specs + [
                  pl.BlockSpec((1, 9, d), lambda i: (mod_row0 + (i * tm) // rows_per_mod, 0, 0)),
                  pl.BlockSpec((2, d), lambda i: (0, 0)),
                  pl.BlockSpec((None, None) + wgu.shape[2:], lambda i: (layer, which, 0, 0),
                               pipeline_mode=pl.Buffered(1)),
                  pl.BlockSpec((None, None) + wd.shape[2:], lambda i: (layer, which, 0, 0),
                               pipeline_mode=pl.Buffered(1))],
        out_specs=[tok] * n_out,
        out_shape=[jax.ShapeDtypeStruct((m, d), F32)] * n_out,
        compiler_params=_cparams("parallel"),
        name="ffn_halfstep",
    )(x, *extras, mods, norm_g2, wgu, wd)
    return outs if n_out == 2 else (outs[0], None)


def _head_sum(x):
    row = lax.broadcasted_iota(jnp.int32, (LANES, LANES), 0) // RW_HEAD
    col = lax.broadcasted_iota(jnp.int32, (LANES, LANES), 1) // RW_HEAD
    ones_bd = jnp.where(row == col, 1.0, 0.0).astype(BF16)
    hi = x.astype(BF16)
    lo = (x - hi.astype(F32)).astype(BF16)
    outs = []
    for j in range(x.shape[1] // LANES):
        sl = slice(j * LANES, (j + 1) * LANES)
        outs.append(jnp.dot(hi[:, sl], ones_bd, preferred_element_type=F32)
                    + jnp.dot(lo[:, sl], ones_bd, preferred_element_type=F32))
    return jnp.concatenate(outs, axis=1)


def _rwkv_prep_kernel(h_ref, hp_ref, hn_ref, mix_ref, wrkv_ref, w1_ref, w2_ref, a1_ref, a2_ref,
                      g1_ref, g2_ref, vec_ref,
                      k_ref, v_ref, kk_ref, r_ref, lw0_ref, lw1_ref, a0_ref, a1o_ref, bonus_ref, g_ref,
                      *, seq_len, grid_w, halo):
    tm, d = h_ref.shape
    q = d // 4
    hext_all = jnp.concatenate([hp_ref[...], h_ref[...], hn_ref[...]], axis=0)
    offs = (-1, 1, -1, 1) if grid_w is None else (-1, 1, -grid_w, grid_w)
    mix = mix_ref[...]
    vec = vec_ref[...]
    w0c, a0c = vec[0:1], vec[1:2]
    k_k, k_a, r_k = vec[2:3, :d], vec[3:4, :d], vec[4:5, :d]
    nh = 2 if tm % 16 == 0 else 1
    th = tm // nh

    def shift_mix(s):
        i = s["i"]
        hext = hext_all[i * th:i * th + th + 2 * halo]
        h = hext[halo:halo + th]
        t = (pl.program_id(0) * tm + i * th + lax.broadcasted_iota(jnp.int32, (th, 1), 0)) % seq_len
        parts = []
        for qi, o in enumerate(offs):
            piece = hext[halo + o:halo + o + th, qi * q:(qi + 1) * q]
            ok = (t + o >= 0) & (t + o < seq_len)
            if grid_w is not None and abs(o) == 1:
                colp = t % grid_w + o
                ok = ok & (colp >= 0) & (colp < grid_w)
            parts.append(jnp.where(ok, piece, 0.0))
        dx = jnp.concatenate(parts, axis=1) - h
        s["x"] = [(h + dx * mix[j:j + 1]).astype(BF16) for j in range(6)]

    def project(s):
        xr, xw, xk, xv, xa, xg = s.pop("x")
        s["r"] = jnp.dot(xr, wrkv_ref[0], preferred_element_type=F32)
        s["k"] = jnp.dot(xk, wrkv_ref[1], preferred_element_type=F32)
        s["v"] = jnp.dot(xv, wrkv_ref[2], preferred_element_type=F32)
        s["wl"] = jnp.dot(xw, w1_ref[...], preferred_element_type=F32)
        s["al"] = jnp.dot(xa, a1_ref[...], preferred_element_type=F32)
        s["gl"] = jnp.dot(xg, g1_ref[...], preferred_element_type=F32)

    def lora_act(s):
        s["wl"] = jnp.tanh(s["wl"]).astype(BF16)
        s["al"] = s["al"].astype(BF16)
        s["gl"] = jax.nn.sigmoid(s["gl"]).astype(BF16)
        s["kk"] = s["k"] * k_k

    def lora_out(s):
        s["w_pre"] = w0c + jnp.dot(s.pop("wl"), w2_ref[...], preferred_element_type=F32)
        s["a_pre"] = a0c + jnp.dot(s.pop("al"), a2_ref[...], preferred_element_type=F32)
        s["g"] = jnp.dot(s.pop("gl"), g2_ref[...], preferred_element_type=F32)
        s["ss"] = _head_sum(s["kk"] * s["kk"])

    def gates(s):
        s["lw"] = (-math.exp(-0.5)) * jax.nn.sigmoid(s.pop("w_pre"))
        a = jax.nn.sigmoid(s.pop("a_pre"))
        s["a_f"], s["a_b"] = a[:, :d], a[:, d:]
        s["kk"] = s["kk"] * lax.rsqrt(jnp.maximum(s.pop("ss"), 1e-24))
        k_bonus = s["k"] * (1.0 + (0.5 * (s["a_f"] + s["a_b"]) - 1.0) * k_a)
        s["rk"] = s["r"] * k_bonus * r_k

    def bonus_sum(s):
        s["rk"] = _head_sum(s["rk"])

    def store(s):
        rows = slice(s["i"] * th, (s["i"] + 1) * th)
        k_ref[rows, :] = s["k"]
        v_ref[rows, :] = s["v"]
        kk_ref[rows, :] = s["kk"]
        r_ref[rows, :] = s["r"]
        lw0_ref[rows, :] = s["lw"][:, :d]
        lw1_ref[rows, :] = s["lw"][:, d:]
        a0_ref[rows, :] = s["a_f"]
        a1o_ref[rows, :] = s["a_b"]
        bonus_ref[rows, :] = s["rk"] * s["v"]
        g_ref[rows, :] = s["g"].astype(g_ref.dtype)

    stages = (shift_mix, project, lora_act, lora_out, gates, bonus_sum, store)
    halves = [{"i": i} for i in range(nh)]
    for step in range(len(stages) + nh - 1):
        for i, s in enumerate(halves):
            if 0 <= step - i < len(stages):
                stages[step - i](s)


def _rwkv_prep(h, seq_len, grid_w, p, tm):
    (mix, w_rkv, w1c, w2bd, a1c, a2bd, g1p, g2p, vec) = p
    m, d = h.shape
    halo = GRID_W
    nb = tm // halo
    last = m // halo - 1
    full = lambda a: pl.BlockSpec(a.shape, lambda i: (0,) * a.ndim, pipeline_mode=pl.Buffered(1))
    tok = pl.BlockSpec((tm, d), lambda i: (i, 0))
    kern = functools.partial(_rwkv_prep_kernel, seq_len=seq_len, grid_w=grid_w, halo=halo)
    return pl.pallas_call(
        kern,
        grid=(m // tm,),
        in_specs=[tok,
                  pl.BlockSpec((halo, d), lambda i: (jnp.maximum(i * nb - 1, 0), 0)),
                  pl.BlockSpec((halo, d), lambda i: (jnp.minimum((i + 1) * nb, last), 0)),
                  full(mix), full(w_rkv), full(w1c), full(w2bd), full(a1c), full(a2bd),
                  full(g1p), full(g2p), full(vec)],
        out_specs=[tok] * 10,
        out_shape=[jax.ShapeDtypeStruct((m, d), F32)] * 9 + [jax.ShapeDtypeStruct((m, d), BF16)],
        compiler_params=_cparams("parallel"),
        name="rwkv_prep",
    )(h, h, h, mix, w_rkv, w1c, w2bd, a1c, a2bd, g1p, g2p, vec)


def _stack_heads(x):
    lane = lax.broadcasted_iota(jnp.int32, x.shape, 1)
    first = lane < RW_HEAD
    return jnp.concatenate([jnp.where(first, x, 0.0), jnp.where(first, 0.0, x)], axis=0)


def _each(f, *lists):
    return [f(*args) for args in zip(*lists)]


def _mm(a, b):
    return jnp.dot(a, b, preferred_element_type=F32)


def _mm_nt(a, b):
    return lax.dot_general(a, b, (((1,), (1,)), ((), ())), preferred_element_type=F32)


def _mm_tn(a, b):
    return lax.dot_general(a, b, (((0,), (0,)), ((), ())), preferred_element_type=F32)


def _to_bf16(x):
    return x.astype(BF16)


INV_BASE = 16


def _row_blocks(x, size, parity):
    return jnp.concatenate([x[b * size:(b + 1) * size] for b in range(x.shape[0] // size) if b % 2 == parity],
                           axis=0)


def _put_row_blocks(xc, size, parity, base):
    out, k = [], 0
    for b in range(2 * xc.shape[0] // size):
        if b % 2 == parity:
            out.append(xc[k * size:(k + 1) * size])
            k += 1
        else:
            out.append(jnp.zeros((size, xc.shape[1]), xc.dtype) if base is None else base[b * size:(b + 1) * size])
    return jnp.concatenate(out, axis=0)


def _unit_tri_inverse(l_mats, ri, ci, reverse):
    nb = INV_BASE
    n = l_mats[0].shape[0]
    b16 = (ri // nb) == (ci // nb)
    b32 = (ri // (2 * nb)) == (ci // (2 * nb))
    sr = lax.broadcasted_iota(jnp.int32, (nb, n), 0)
    sc = lax.broadcasted_iota(jnp.int32, (nb, n), 1)
    eye_s = jnp.where(sr == sc % nb, 1.0, 0.0)
    strip = lambda m: functools.reduce(lambda a, b: a + b, [m[b * nb:(b + 1) * nb] for b in range(n // nb)])
    block_diag = lambda s: jnp.where(b16, jnp.concatenate([s] * (n // nb), axis=0), 0.0)

    l16 = _each(lambda l: jnp.where(b16, l, 0.0), l_mats)
    l16s = _each(strip, l16)
    xs = _each(lambda s: eye_s - s, l16s)
    ps = _each(lambda s, m: _mm(s.astype(BF16), m.astype(BF16)), l16s, l16)
    for it in range(3):
        pbd = _each(lambda s: block_diag(s).astype(BF16), ps)
        if it < 2:
            both = _each(lambda x_, s, p_: _mm(jnp.concatenate([x_, s], axis=0).astype(BF16), p_), xs, ps, pbd)
            xs = _each(lambda x_, b_: x_ + b_[:nb], xs, both)
            ps = _each(lambda b_: b_[nb:], both)
        else:
            xs = _each(lambda x_, p_: x_ + _mm(x_.astype(BF16), p_), xs, pbd)
    x = _each(block_diag, xs)

    par = 0 if reverse else 1
    for size, off_mask in ((nb, b32 & (~b16)), (2 * nb, ~b32)):
        lc = _each(lambda l: _row_blocks(jnp.where(off_mask, l, 0.0), size, par).astype(BF16), l_mats)
        xb = _each(_to_bf16, x)
        t = _each(lambda l_, x_: _put_row_blocks(_mm(l_, x_).astype(BF16), size, par, None), lc, xb)
        xc = _each(lambda x_: _row_blocks(x_, size, par), x)
        r = _each(lambda x_, t_: x_ - _mm(x_.astype(BF16), t_), xc, t)
        x = _each(lambda r_, x_: _put_row_blocks(r_, size, par, x_), r, x)
    return x


def _chunk_cumsum(x, reverse):
    c = x.shape[0]
    row = lax.broadcasted_iota(jnp.int32, x.shape, 0)
    s = 1
    while s < c:
        if reverse:
            x = x + jnp.where(row < c - s, pltpu.roll(x, c - s, axis=0), 0.0)
        else:
            x = x + jnp.where(row >= s, pltpu.roll(x, s, axis=0), 0.0)
        s *= 2
    return x


def _chunk_local(lw, k, v, kk, r, a, k_a, reverse):
    c = lw[0].shape[0]
    n = 2 * c
    kd = _each(lambda k_, a_, ka_: k_ * (1.0 + (a_ - 1.0) * ka_), k, a, k_a)
    ka = _each(lambda kk_, a_: kk_ * a_, kk, a)
    cum = _each(lambda x: _chunk_cumsum(x, reverse), lw)
    tot = _each(lambda x: x[0:1] if reverse else x[c - 1:c], cum)
    e_neg = _each(lambda x: jnp.exp(-x), cum)
    e_end = _each(lambda t_, x: jnp.exp(t_ - x), tot, cum)
    stack_b = lambda x: _stack_heads(x).astype(BF16)
    alpha = _each(lambda kk_, c_, l_: stack_b(kk_ * jnp.exp(c_ - l_)), kk, cum, lw)
    beta = _each(lambda x, e: stack_b(x * e), ka, e_neg)
    kappa = _each(lambda x, e: stack_b(x * e), kd, e_neg)
    rho = _each(lambda r_, c_: _stack_heads(r_ * jnp.exp(c_)), r, cum)
    kappa_e = _each(lambda x, e: stack_b(x * e), kd, e_end)
    beta_e = _each(lambda x, e: stack_b(x * e), ka, e_end)
    v_st = _each(stack_b, v)
    w_end = _each(jnp.exp, tot)

    sc = _each(lambda al, rh, be, kp: _mm_nt(jnp.concatenate([al, rh.astype(BF16)], axis=0),
                                             jnp.concatenate([be, kp], axis=0)), alpha, rho, beta, kappa)
    ri = lax.broadcasted_iota(jnp.int32, (n, n), 0)
    ci = lax.broadcasted_iota(jnp.int32, (n, n), 1)
    strict = (ci > ri) if reverse else (ci < ri)
    incl = (ci >= ri) if reverse else (ci <= ri)
    l_mat = _each(lambda x: jnp.where(strict, x[:n, :n], 0.0), sc)
    apk = _each(lambda x: jnp.concatenate([jnp.where(strict, x[:n, n:], 0.0), jnp.where(incl, x[n:, n:], 0.0)],
                                          axis=0).astype(BF16), sc)
    pb = _each(lambda x: jnp.where(incl, x[n:, :n], 0.0).astype(BF16), sc)

    apkv = _each(_mm, apk, v_st)
    x = _each(_to_bf16, _unit_tri_inverse(l_mat, ri, ci, reverse))
    uwb = _each(lambda x_, kv, al: _mm(x_, jnp.concatenate([kv[:n].astype(BF16), al], axis=1)).astype(BF16),
                x, apkv, alpha)
    yr = _each(lambda kv, rh, pb_, uw: jnp.concatenate([kv[n:], rh], axis=1) - _mm(pb_, uw), apkv, rho, pb, uwb)
    uq = _each(_mm_tn, uwb, beta_e)
    n_t = _each(lambda v_, ke, uq_: _mm_tn(v_, ke) - uq_[:n], v_st, kappa_e, uq)
    q = _each(lambda uq_: uq_[n:].astype(BF16), uq)
    return list(zip(yr, n_t, q, w_end))


def _chunk_state(s, local):
    n = local[0][0].shape[0]
    c = n // 2
    sb = _each(_to_bf16, s)
    y_st = _each(lambda lc, sb_: lc[0][:, :n] + _mm_nt(lc[0][:, n:].astype(BF16), sb_), local, sb)
    y = _each(lambda x_: x_[:c] + x_[c:], y_st)
    s_new = _each(lambda s_, sb_, lc: s_ * lc[3] - _mm(sb_, lc[2]) + lc[1], s, sb, local)
    return s_new, y


SCAN_SLABS = 8
SCAN_PAIR = 4


def _scan_kernel(k_ref, v_ref, kk_ref, r_ref, lw_ref, a_ref, ka_ref, s0_ref, y_ref, sout_ref, s_scr,
                 *, reverse, chunk):
    tb = k_ref.shape[1]
    nslab = s_scr.shape[0]
    nchunk = tb // chunk
    step = pl.program_id(2)

    @pl.when(step == 0)
    def _():
        s_scr[...] = s0_ref[0]

    lanes = [slice(g * LANES, (g + 1) * LANES) for g in range(nslab)]
    pair = math.gcd(SCAN_PAIR, nchunk)

    def body(i, carry):
        cis = [(nchunk - 1 - (i * pair + p)) if reverse else (i * pair + p) for p in range(pair)]
        sls = [pl.ds(pl.multiple_of(ci * chunk, chunk), chunk) for ci in cis]
        get = lambda ref: [ref[0, sl, ln] for sl in sls for ln in lanes]
        local = _chunk_local(get(lw_ref), get(k_ref), get(v_ref), get(kk_ref), get(r_ref), get(a_ref),
                             [ka_ref[:, ln] for _ in sls for ln in lanes], reverse)
        s = [s_scr[g] for g in range(nslab)]
        for p, sl in enumerate(sls):
            s, y = _chunk_state(s, local[p * nslab:(p + 1) * nslab])
            for g in range(nslab):
                y_ref[0, sl, lanes[g]] = y[g]
        for g in range(nslab):
            s_scr[g] = s[g]
        return carry

    lax.fori_loop(0, nchunk // pair, body, 0)

    @pl.when(step == pl.num_programs(2) - 1)
    def _():
        sout_ref[0] = s_scr[...]


def _rwkv_scan(k, v, kk, r, lw, a, k_a, s0, reverse, tb):
    b, t, d = k.shape
    g = SCAN_SLABS
    gw = g * LANES
    nt = t // tb
    tmap = (lambda bi, j, c: (bi, nt - 1 - c, j)) if reverse else (lambda bi, j, c: (bi, c, j))
    tok = pl.BlockSpec((1, tb, gw), tmap)
    st = pl.BlockSpec((1, g, LANES, LANES), lambda bi, j, c: (bi, j, 0, 0))
    kern = functools.partial(_scan_kernel, reverse=reverse, chunk=CHUNK)
    return pl.pallas_call(
        kern,
        grid=(b, d // gw, nt),
        in_specs=[tok] * 6 + [pl.BlockSpec((1, gw), lambda bi, j, c: (0, j)), st],
        out_specs=[tok, st],
        out_shape=[jax.ShapeDtypeStruct((b, t, d), F32),
                   jax.ShapeDtypeStruct((b, d // LANES, LANES, LANES), F32)],
        scratch_shapes=[pltpu.VMEM((g, LANES, LANES), F32)],
        compiler_params=_cparams("parallel", "parallel", "arbitrary"),
        name="rwkv_scan_bwd" if reverse else "rwkv_scan_fwd",
    )(k, v, kk, r, lw, a, k_a, s0)


def _hy_in_kernel(h_ref, hp_ref, hn_ref, w_ref, cw_ref, cb_ref, x1_ref, x2_ref, v_ref, *, seq_len):
    tm, d = h_ref.shape
    halo = hp_ref.shape[0]
    cw = cw_ref[...]
    cb = cb_ref[...]
    nh = 2 if tm % (4 * halo) == 0 else 1
    th = tm // nh
    pes = []
    for i in range(nh):
        before = hp_ref[...] if i == 0 else h_ref[i * th - halo:i * th, :]
        after = hn_ref[...] if i == nh - 1 else h_ref[(i + 1) * th:(i + 1) * th + halo, :]
        hext = jnp.concatenate([before, h_ref[i * th:(i + 1) * th, :], after], axis=0).astype(BF16)
        pes.append(jnp.dot(hext, w_ref[...], preferred_element_type=F32))
    for i, pe in enumerate(pes):
        t = (pl.program_id(0) * tm + i * th + lax.broadcasted_iota(jnp.int32, (th, 1), 0)) % seq_len
        prev = jnp.where(t >= 1, pe[halo - 1:halo - 1 + th], 0.0)
        nxt = jnp.where(t + 1 < seq_len, pe[halo + 1:halo + 1 + th], 0.0)
        u = prev * cw[0:1] + pe[halo:halo + th] * cw[1:2] + nxt * cw[2:3] + cb
        rows = slice(i * th, (i + 1) * th)
        x1_ref[rows, :] = u[:, :d]
        x2_ref[rows, :] = u[:, d:2 * d]
        v_ref[rows, :] = u[:, 2 * d:]


def _hy_in(h, seq_len, w_in, conv_w, conv_b, tm):
    m, d = h.shape
    halo = 8
    nb = tm // halo
    last = m // halo - 1
    tok = pl.BlockSpec((tm, d), lambda i: (i, 0))
    full = lambda a: pl.BlockSpec(a.shape, lambda i: (0,) * a.ndim, pipeline_mode=pl.Buffered(1))
    return pl.pallas_call(
        functools.partial(_hy_in_kernel, seq_len=seq_len),
        grid=(m // tm,),
        in_specs=[tok,
                  pl.BlockSpec((halo, d), lambda i: (jnp.maximum(i * nb - 1, 0), 0)),
                  pl.BlockSpec((halo, d), lambda i: (jnp.minimum((i + 1) * nb, last), 0)),
                  full(w_in), full(conv_w), full(conv_b)],
        out_specs=[tok] * 3,
        out_shape=[jax.ShapeDtypeStruct((m, d), F32)] * 3,
        compiler_params=_cparams("parallel"),
        name="hyena_in",
    )(h, h, h, w_in, conv_w, conv_b)


def _hy_filter_kernel(z_ref, w1_ref, b1_ref, fr_ref, w2_ref, b2_ref, w3_ref, dl_ref, o_ref):
    hp = lax.Precision.HIGHEST
    z = z_ref[...]
    fr = fr_ref[...]
    hid = jnp.sin(fr[0:1] * (jnp.dot(z, w1_ref[...], precision=hp, preferred_element_type=F32) + b1_ref[...]))
    hid = jnp.sin(fr[1:2] * (jnp.dot(hid, w2_ref[...], precision=hp, preferred_element_type=F32) + b2_ref[...]))
    filt = jnp.dot(hid, w3_ref[...], precision=hp, preferred_element_type=F32)
    o_ref[...] = filt * jnp.exp(-z[:, 0:1] * jnp.abs(dl_ref[...]))


def _hy_filters(seq_len, f_w1, f_b1, f_freq, f_w2, f_b2, f_w3, deltas):
    t = jnp.linspace(0.0, 1.0, seq_len, dtype=F32)[:, None]
    ang = ((2 * math.pi / seq_len) * jnp.arange(seq_len, dtype=F32)[:, None]
           * jnp.linspace(1e-4, HY_BANDS - 1, HY_BANDS, dtype=F32)[None])
    z = jnp.concatenate([t, jnp.cos(ang), -jnp.sin(ang)], axis=-1)
    emb, hid = f_w1.shape
    z = jnp.pad(z, ((0, 0), (0, LANES - emb)))
    w1 = jnp.pad(f_w1, ((0, LANES - emb), (0, 0)))
    n_out = f_w3.shape[1]
    tl = min(seq_len, 512)
    full = lambda a: pl.BlockSpec(a.shape, lambda i: (0,) * a.ndim)
    args = (w1, f_b1.reshape(1, hid), f_freq, f_w2, f_b2.reshape(1, hid), f_w3, deltas.reshape(1, n_out))
    return pl.pallas_call(
        _hy_filter_kernel,
        grid=(seq_len // tl,),
        in_specs=[pl.BlockSpec((tl, LANES), lambda i: (i, 0))] + [full(a) for a in args],
        out_specs=pl.BlockSpec((tl, n_out), lambda i: (i, 0)),
        out_shape=jax.ShapeDtypeStruct((seq_len, n_out), F32),
        compiler_params=_cparams("parallel"),
        name="hyena_filters",
    )(z, *args)


DFT_UNROLL = 8


def _rdft_tables(seq_len):
    big = 2 * seq_len
    n = int(round(math.sqrt(big)))
    assert n * n == big and n % 32 == 0, "sequence length must give a square, tile-aligned DFT"
    kh = n // 2 + 1
    kp = -(-kh // 8) * 8
    k1 = np.arange(kp)[None, :, None]
    n1 = np.arange(n // 2)[None, None, :]
    n2 = np.arange(n)[:, None, None]
    ang = -2.0 * np.pi * ((k1 * (n * n1 + n2)) % big) / big
    live = (k1 < kh).astype(np.float64)
    e_in = np.concatenate([np.cos(ang) * live, np.sin(ang) * live], axis=1)
    wgt = np.where((k1 == 0) | (k1 == n // 2), 1.0, 2.0) * live
    e_out = np.transpose(np.concatenate([np.cos(ang) * wgt, np.sin(ang) * wgt], axis=1), (0, 2, 1))
    a2 = -2.0 * np.pi * ((np.arange(n)[:, None] * np.arange(n)[None, :]) % n) / n
    fr, fi = np.cos(a2), np.sin(a2)
    f_fwd = np.block([[fr, -fi], [fi, fr]])
    f_inv = np.block([[fr, fi], [-fi, fr]])
    kg = max(g for g in range(1, 21) if kh % g == 0)
    cast = lambda a: jnp.asarray(a, dtype=F32).astype(BF16)
    return dict(n=n, kh=kh, kp=kp, kg=kg, e_in=cast(e_in), e_out=cast(e_out), f_fwd=cast(f_fwd), f_inv=cast(f_inv))


ROW_PAD = 8


def _rdft_first_stage(u_ref, e_in_ref, x_pad, a_scr, n, kp):
    pitch = n + ROW_PAD
    for n1 in range(n // 2):
        x_pad[n1 * pitch:n1 * pitch + n, :] = u_ref[0, n1 * n:(n1 + 1) * n, :]

    def body(i, carry):
        n2s = [i * DFT_UNROLL + j for j in range(DFT_UNROLL)]
        xs = [x_pad[pl.ds(n2, n // 2, stride=pitch), :].astype(BF16) for n2 in n2s]
        acc = [_mm(e_in_ref[n2], x) for n2, x in zip(n2s, xs)]
        for n2, a in zip(n2s, acc):
            a_scr[0, pl.ds(n2, kp, stride=pitch), :] = a[:kp]
            a_scr[1, pl.ds(n2, kp, stride=pitch), :] = a[kp:]
        return carry

    lax.fori_loop(0, n // DFT_UNROLL, body, 0)


def _rdft_second_stage(a_scr, k1s, n, f_fwd):
    pitch = n + ROW_PAD
    cols = []
    for k1 in k1s:
        rows = pl.ds(pl.multiple_of(k1 * pitch, 8), n)
        cols.append(jnp.concatenate([a_scr[0, rows, :], a_scr[1, rows, :]], axis=0).astype(BF16))
    return [_mm(f_fwd, c) for c in cols]


def _hy_conv_kernel(u_ref, gate_ref, h_ref, ein_ref, eout_ref, ff_ref, fi_ref, bias_ref, o_ref, x_pad, a_scr,
                    *, n, kp, kh, mg):
    _rdft_first_stage(u_ref, ein_ref, x_pad, a_scr, n, kp)

    f_fwd, f_inv = ff_ref[...], fi_ref[...]

    pitch = n + ROW_PAD

    def mid(i, carry):
        for j0 in range(0, mg, DFT_UNROLL):
            k1s = [i * mg + j for j in range(j0, min(j0 + DFT_UNROLL, mg))]
            xs = _rdft_second_stage(a_scr, k1s, n, f_fwd)
            ys = []
            for k1, x in zip(k1s, xs):
                xr, xi = x[:n], x[n:]
                hr, hi = h_ref[0, k1], h_ref[1, k1]
                ys.append(jnp.concatenate([xr * hr - xi * hi, xr * hi + xi * hr], axis=0).astype(BF16))
            zs = [_mm(f_inv, y) for y in ys]
            for k1, z in zip(k1s, zs):
                rows = pl.ds(pl.multiple_of(k1 * pitch, 8), n)
                a_scr[0, rows, :] = z[:n]
                a_scr[1, rows, :] = z[n:]
        return carry

    lax.fori_loop(0, kh // mg, mid, 0)

    inv_n = 1.0 / (n * n)
    bias = bias_ref[...]

    def last(i, carry):
        n2s = [i * DFT_UNROLL + j for j in range(DFT_UNROLL)]
        zc = []
        for n2 in n2s:
            rows = pl.ds(n2, kp, stride=pitch)
            zc.append(jnp.concatenate([a_scr[0, rows, :], a_scr[1, rows, :]], axis=0).astype(BF16))
        ys = [_mm(eout_ref[n2], z) * inv_n for n2, z in zip(n2s, zc)]
        for n2, y in zip(n2s, ys):
            x_pad[pl.ds(n2, n // 2, stride=pitch), :] = y
        return carry

    lax.fori_loop(0, n // DFT_UNROLL, last, 0)
    for n1 in range(n // 2):
        rows = slice(n1 * n, (n1 + 1) * n)
        y = x_pad[n1 * pitch:n1 * pitch + n, :]
        o_ref[0, rows, :] = gate_ref[0, rows, :] * (y + u_ref[0, rows, :] * bias)


def _hy_conv(u, gate, spec, f_idx, bias, tb):
    b, l, d = u.shape
    n, kh, kp, mg = tb["n"], tb["kh"], tb["kp"], tb["kg"]
    nd = d // LANES
    once = dict(pipeline_mode=pl.Buffered(1))
    tok = lambda bufs: pl.BlockSpec((1, l, LANES), lambda c, bi: (bi, 0, c), **bufs)
    full = lambda a: pl.BlockSpec(a.shape, lambda c, bi: (0,) * a.ndim, **once)
    return pl.pallas_call(
        functools.partial(_hy_conv_kernel, n=n, kp=kp, kh=kh, mg=mg),
        grid=(nd, b),
        in_specs=[tok({}), tok(once),
                  pl.BlockSpec((2, kh, n, LANES), lambda c, bi: (0, 0, 0, f_idx * nd + c), **once),
                  full(tb["e_in"]), full(tb["e_out"]), full(tb["f_fwd"]), full(tb["f_inv"]),
                  pl.BlockSpec((1, LANES), lambda c, bi: (0, c))],
        out_specs=tok({}),
        out_shape=jax.ShapeDtypeStruct((b, l, d), F32),
        scratch_shapes=[pltpu.VMEM((n // 2 * (n + ROW_PAD), LANES), F32), pltpu.VMEM((2, kp * (n + ROW_PAD), LANES), F32)],
        compiler_params=_cparams("parallel", "parallel"),
        name="hyena_conv",
    )(u, gate, spec, tb["e_in"], tb["e_out"], tb["f_fwd"], tb["f_inv"], bias.reshape(1, d))


def _hy_spec_kernel(u_ref, ein_ref, ff_ref, h_ref, x_pad, a_scr, *, n, kp, kg):
    step = pl.program_id(1)

    @pl.when(step == 0)
    def _():
        _rdft_first_stage(u_ref, ein_ref, x_pad, a_scr, n, kp)

    f_fwd = ff_ref[...]
    for j0 in range(0, kg, DFT_UNROLL):
        js = list(range(j0, min(j0 + DFT_UNROLL, kg)))
        xs = _rdft_second_stage(a_scr, [step * kg + j for j in js], n, f_fwd)
        for j, x in zip(js, xs):
            h_ref[0, j] = x[:n]
            h_ref[1, j] = x[n:]


def _hy_spectrum(filt, tb):
    _, l, dx = filt.shape
    n, kh, kp, kg = tb["n"], tb["kh"], tb["kp"], tb["kg"]
    full = lambda a: pl.BlockSpec(a.shape, lambda c, g: (0,) * a.ndim, pipeline_mode=pl.Buffered(1))
    return pl.pallas_call(
        functools.partial(_hy_spec_kernel, n=n, kp=kp, kg=kg),
        grid=(dx // LANES, kh // kg),
        in_specs=[pl.BlockSpec((1, l, LANES), lambda c, g: (0, 0, c)), full(tb["e_in"]), full(tb["f_fwd"])],
        out_specs=pl.BlockSpec((2, kg, n, LANES), lambda c, g: (0, g, 0, c)),
        out_shape=jax.ShapeDtypeStruct((2, kh, n, dx), F32),
        scratch_shapes=[pltpu.VMEM((n // 2 * (n + ROW_PAD), LANES), F32), pltpu.VMEM((2, kp * (n + ROW_PAD), LANES), F32)],
        compiler_params=_cparams("parallel", "arbitrary"),
        name="hyena_filter_spectrum",
    )(filt, tb["e_in"], tb["f_fwd"])


def _hyena_mixer(hl, batch, hp, tm):
    (w_in, conv_w, conv_b, f_w1, f_b1, f_freq, f_w2, f_b2, f_w3, deltas, bias) = hp
    m, d = hl.shape
    seq_len = m // batch
    x1, x2, v = _hy_in(hl, seq_len, w_in.astype(BF16), conv_w, conv_b.reshape(1, -1), tm)
    three = lambda a: a.reshape(batch, seq_len, d)
    x1, x2, v = three(x1), three(x2), three(v)
    tables = _rdft_tables(seq_len)
    filt = _hy_filters(seq_len, f_w1, f_b1, f_freq, f_w2, f_b2, f_w3, deltas)
    spec = _hy_spectrum(filt[None], tables)
    z = _hy_conv(v, x1, spec, 0, bias[0], tables)
    z = _hy_conv(z, x2, spec, 1, bias[1], tables)
    return z.reshape(m, d)


def _rwkv_weights(mix, w_rkv, w0, w1, w2, a0, a1, a2, g1, g2, k_k, k_a, r_k):
    d = mix.shape[-1]
    lora = w1.shape[-1]
    zero = jnp.zeros((lora, d), F32)
    blockdiag = lambda m: jnp.concatenate([jnp.concatenate([m[0], zero], axis=1),
                                           jnp.concatenate([zero, m[1]], axis=1)], axis=0)
    gl = g1.shape[-1]
    glp = -(-gl // LANES) * LANES
    vec = jnp.zeros((8, 2 * d), F32)
    vec = vec.at[0].set(w0.reshape(-1)).at[1].set(a0.reshape(-1))
    vec = vec.at[2, :d].set(k_k).at[3, :d].set(k_a).at[4, :d].set(r_k.reshape(-1))
    return (mix, w_rkv.astype(BF16),
            jnp.concatenate([w1[0], w1[1]], axis=1).astype(BF16), blockdiag(w2).astype(BF16),
            jnp.concatenate([a1[0], a1[1]], axis=1).astype(BF16), blockdiag(a2).astype(BF16),
            jnp.pad(g1, ((0, 0), (0, glp - gl))).astype(BF16),
            jnp.pad(g2, ((0, glp - gl), (0, 0))).astype(BF16), vec)


def _rwkv_mixer(hl, hc, batch, wts, k_a, tm, tb):
    d = hl.shape[1]
    t_lat, t_ctx = hl.shape[0] // batch, hc.shape[0] // batch
    k_a = k_a.reshape(1, d)
    s0 = jnp.zeros((batch, d // LANES, LANES, LANES), F32)
    three = lambda arrs, t: [a.reshape(batch, t, d) for a in arrs]
    kc, vc, kkc, rc, lw0c, lw1c, a0c, a1c, _, _ = _rwkv_prep(hc, t_ctx, None, wts, min(tm, t_ctx))
    kc, vc, kkc, rc, lw0c, lw1c, a0c, a1c = three((kc, vc, kkc, rc, lw0c, lw1c, a0c, a1c), t_ctx)
    tbc = min(tb, t_ctx)
    _, s_f = _rwkv_scan(kc, vc, kkc, rc, lw0c, a0c, k_a, s0, False, tbc)
    _, s_b = _rwkv_scan(kc, vc, kkc, rc, lw1c, a1c, k_a, s0, True, tbc)
    k, v, kk, r, lw0, lw1, a0, a1, bonus, g = _rwkv_prep(hl, t_lat, GRID_W, wts, tm)
    k, v, kk, r, lw0, lw1, a0, a1 = three((k, v, kk, r, lw0, lw1, a0, a1), t_lat)
    yf, _ = _rwkv_scan(k, v, kk, r, lw0, a0, k_a, s_f, False, tb)
    yb, _ = _rwkv_scan(k, v, kk, r, lw1, a1, k_a, s_b, True, tb)
    return yf.reshape(-1, d), yb.reshape(-1, d), bonus, g


TOKEN_TILE = 512
PREP_TILE = 256
SCAN_BLOCK = 512


def kernel(x, c, ctx, c_ctx, mod_w, mod_b, norm_g, ffn_w_gu, ffn_w_down, rw_mix, rw_w_rkv, rw_w_o, rw_w0, rw_w1, rw_w2, rw_a0, rw_a1, rw_a2, rw_g1, rw_g2, rw_k_k, rw_k_a, rw_r_k, rw_ln_g, rw_ln_b, hy_w_in, hy_conv_w, hy_conv_b, hy_f_w1, hy_f_b1, hy_f_freq, hy_f_w2, hy_f_b2, hy_f_w3, hy_deltas, hy_bias, hy_w_out, final_g):
    batch, seq, d = x.shape
    t_ctx = ctx.shape[1]
    depth = mod_w.shape[0]
    assert depth == 2 and rw_mix.shape[0] == 1 and hy_w_in.shape[0] == 1
    assert seq % TOKEN_TILE == 0 and seq % GRID_W == 0 and t_ctx % CHUNK == 0 and batch + 1 <= 8

    cc = jnp.concatenate([c, c_ctx[None], jnp.zeros((8 - batch - 1, d), F32)], axis=0)
    mods0 = _modulation(cc, mod_w[0], mod_b[0])
    mods1 = _modulation(cc, mod_w[1], mod_b[1])
    wgu = ffn_w_gu.astype(BF16)
    wd = ffn_w_down.astype(BF16)
    xl = x.reshape(batch * seq, d)
    xc = ctx.reshape(batch * t_ctx, d)
    tm = TOKEN_TILE
    tmc = min(tm, t_ctx)

    xl, hl = _ffn(xl, mods0, 0, seq, norm_g[0, 0:2], wgu, wd, 0, 0, 0, 3, tm)
    _, hc = _ffn(xc, mods0, batch, batch * t_ctx, norm_g[0, 0:2], wgu, wd, 0, 0, 0, 3, tmc)
    wts = _rwkv_weights(rw_mix[0], rw_w_rkv[0], rw_w0[0], rw_w1[0], rw_w2[0], rw_a0[0], rw_a1[0], rw_a2[0],
                        rw_g1[0], rw_g2[0], rw_k_k[0], rw_k_a[0], rw_r_k[0])
    yf, yb, bonus, g = _rwkv_mixer(hl, hc, batch, wts, rw_k_a[0], PREP_TILE, SCAN_BLOCK)
    rwkv_res = ("rwkv", yf, yb, bonus, g, jnp.stack([rw_ln_g[0], rw_ln_b[0]]), rw_w_o[0].astype(BF16))
    xl, _ = _ffn(xl, mods0, 0, seq, jnp.stack([norm_g[0, 2], final_g]), wgu, wd, 0, 1, 6, None, tm,
                 pre=rwkv_res)

    xl, hl = _ffn(xl, mods1, 0, seq, norm_g[1, 0:2], wgu, wd, 1, 0, 0, 3, tm)
    hp = (hy_w_in[0], hy_conv_w[0], hy_conv_b[0], hy_f_w1[0], hy_f_b1[0], hy_f_freq[0], hy_f_w2[0],
          hy_f_b2[0], hy_f_w3[0], hy_deltas[0], hy_bias[0])
    z = _hyena_mixer(hl, batch, hp, tm)
    out, _ = _ffn(xl, mods1, 0, seq, jnp.stack([norm_g[1, 2], final_g]), wgu, wd, 1, 1, 6, "final", tm,
                  pre=("proj", z, hy_w_out[0].astype(BF16)))
    return out.reshape(batch, seq, d)
```

```python
import functools
import math

import jax
import jax.numpy as jnp
import numpy as np
from jax import lax
from jax.experimental import pallas as pl
from jax.experimental.pallas import tpu as pltpu

F32 = jnp.float32
BF16 = jnp.bfloat16

GRID_W = 64
RW_HEAD = 64
NORM_EPS = 1e-6
GN_EPS = 64e-5
HY_BANDS = 16

LANES = 128
CHUNK = 64
VMEM_LIMIT = 56 * 1024 * 1024


def _cparams(*sem):
    return pltpu.CompilerParams(dimension_semantics=sem, vmem_limit_bytes=VMEM_LIMIT)


def _bdot(a, b):
    return jnp.dot(a.astype(BF16), b.astype(BF16), preferred_element_type=F32)


def _rms_mod(x, g, shift, scale):
    ms = jnp.mean(x * x, axis=-1, keepdims=True)
    return (x * lax.rsqrt(ms + NORM_EPS) * g) * (1.0 + scale) + shift


def _mod_kernel(c_ref, w_ref, b_ref, o_ref):
    c = c_ref[...]
    o_ref[...] = _bdot(c * jax.nn.sigmoid(c), w_ref[...]) + b_ref[...]


def _modulation(cc, w, b, layer):
    m, d = cc.shape
    depth, _, n = w.shape
    tn = 1152
    out = pl.pallas_call(
        _mod_kernel,
        grid=(n // tn,),
        in_specs=[pl.BlockSpec((m, d), lambda j: (0, 0)),
                  pl.BlockSpec((None, d, tn), lambda j: (layer, 0, j)),
                  pl.BlockSpec((None, 1, tn), lambda j: (layer, 0, j))],
        out_specs=pl.BlockSpec((m, tn), lambda j: (0, j)),
        out_shape=jax.ShapeDtypeStruct((m, n), F32),
        compiler_params=_cparams("parallel"),
        name="modulation",
    )(cc, w, b.reshape(depth, 1, n))
    return out.reshape(m, 9, d)


def _mixer_residual(pre, extras, rows, x, gate):
    if pre == "proj":
        z_ref, wo_ref = extras
        z = z_ref[rows, :].astype(BF16)
    else:
        yf_ref, yb_ref, bonus_ref, gt_ref, ln_ref, wo_ref = extras
        y = yf_ref[rows, :] + yb_ref[rows, :]
        ln = ln_ref[...]
        inv_n = 1.0 / RW_HEAD
        yc = y - _head_sum(y) * inv_n
        var = _head_sum(yc * yc) * inv_n
        yn = yc * lax.rsqrt(var + GN_EPS) * ln[0:1] + ln[1:2]
        z = ((yn + bonus_ref[rows, :]) * gt_ref[rows, :]).astype(BF16)
    return x + gate * jnp.dot(z, wo_ref[...], preferred_element_type=F32)


def _ffn_kernel(*refs, mod_off, f_chunk, next_off, pre):
    n_extra = {None: 0, "proj": 2, "rwkv": 6}[pre]
    x_ref, extras = refs[0], refs[1:1 + n_extra]
    mod_ref, g_ref, wgu_ref, wd_ref = refs[1 + n_extra:5 + n_extra]
    out_refs = refs[5 + n_extra:]
    tm = x_ref.shape[0]
    mod = mod_ref[0]
    g = g_ref[...]
    f = wd_ref.shape[0]
    nh = 2 if tm % 32 == 0 else 1
    rows = [slice(i * tm // nh, (i + 1) * tm // nh) for i in range(nh)]
    xs, hs, gu, accs = [None] * nh, [None] * nh, [None] * nh, [None] * nh
    for c0 in range(0, f, f_chunk):
        for i in range(nh):
            if c0 == 0:
                xs[i] = x_ref[rows[i], :]
                if pre is not None:
                    xs[i] = _mixer_residual(pre, extras, rows[i], xs[i], mod[5:6])
                hs[i] = _rms_mod(xs[i], g[0:1], mod[mod_off:mod_off + 1], mod[mod_off + 1:mod_off + 2]).astype(BF16)
            gu[i] = (jnp.dot(hs[i], wgu_ref[:, c0:c0 + f_chunk], preferred_element_type=F32),
                     jnp.dot(hs[i], wgu_ref[:, f + c0:f + c0 + f_chunk], preferred_element_type=F32))
        for i in range(nh):
            gate, up = gu[i]
            a = (gate * jax.nn.sigmoid(gate) * up).astype(BF16)
            down = jnp.dot(a, wd_ref[c0:c0 + f_chunk, :], preferred_element_type=F32)
            accs[i] = down if c0 == 0 else accs[i] + down
    for i in range(nh):
        xn = xs[i] + (0.5 * mod[mod_off + 2:mod_off + 3]) * accs[i]
        if next_off == "final":
            ms = jnp.mean(xn * xn, axis=-1, keepdims=True)
            out_refs[0][rows[i], :] = xn * lax.rsqrt(ms + NORM_EPS) * g[1:2]
            continue
        out_refs[0][rows[i], :] = xn
        if next_off is not None:
            out_refs[1][rows[i], :] = _rms_mod(xn, g[1:2], mod[next_off:next_off + 1],
                                               mod[next_off + 1:next_off + 2])


def _ffn(x, mods, mod_row0, rows_per_mod, norm_g2, wgu, wd, layer, which, mod_off, next_off, tm, pre=None):
    m, d = x.shape
    f = wd.shape[2]
    f_chunk = f // 2 if (f // 2) % LANES == 0 else f
    n_out = 2 if isinstance(next_off, int) else 1
    kind = None if pre is None else pre[0]
    kern = functools.partial(_ffn_kernel, mod_off=mod_off, f_chunk=f_chunk, next_off=next_off, pre=kind)
    tok = pl.BlockSpec((tm, d), lambda i: (i, 0))
    extras, extra_specs = [], []
    for a in (() if pre is None else pre[1:]):
        extras.append(a)
        whole = pl.BlockSpec(a.shape, lambda i: (0, 0), pipeline_mode=pl.Buffered(1))
        extra_specs.append(tok if a.shape == (m, d) else whole)
    outs = pl.pallas_call(
        kern,
        grid=(m // tm,),
        in_specs=[tok] + extra_specs + [
                  pl.BlockSpec((1, 9, d), lambda i: (mod_row0 + (i * tm) // rows_per_mod, 0, 0)),
                  pl.BlockSpec((2, d), lambda i: (0, 0)),
                  pl.BlockSpec((None, None) + wgu.shape[2:], lambda i: (layer, which, 0, 0),
                               pipeline_mode=pl.Buffered(1)),
                  pl.BlockSpec((None, None) + wd.shape[2:], lambda i: (layer, which, 0, 0),
                               pipeline_mode=pl.Buffered(1))],
        out_specs=[tok] * n_out,
        out_shape=[jax.ShapeDtypeStruct((m, d), F32)] * n_out,
        compiler_params=_cparams("parallel"),
        name="ffn_halfstep",
    )(x, *extras, mods, norm_g2, wgu, wd)
    return outs if n_out == 2 else (outs[0], None)


def _head_sum(x):
    row = lax.broadcasted_iota(jnp.int32, (LANES, LANES), 0) // RW_HEAD
    col = lax.broadcasted_iota(jnp.int32, (LANES, LANES), 1) // RW_HEAD
    ones_bd = jnp.where(row == col, 1.0, 0.0).astype(BF16)
    hi = x.astype(BF16)
    lo = (x - hi.astype(F32)).astype(BF16)
    outs = []
    for j in range(x.shape[1] // LANES):
        sl = slice(j * LANES, (j + 1) * LANES)
        outs.append(jnp.dot(hi[:, sl], ones_bd, preferred_element_type=F32)
                    + jnp.dot(lo[:, sl], ones_bd, preferred_element_type=F32))
    return jnp.concatenate(outs, axis=1)


def _rwkv_prep_kernel(h_ref, hp_ref, hn_ref, mix_ref, wrkv_ref, w1_ref, w2_ref, a1_ref, a2_ref,
                      g1_ref, g2_ref, vec_ref,
                      k_ref, v_ref, kk_ref, r_ref, lw0_ref, lw1_ref, a0_ref, a1o_ref, bonus_ref, g_ref,
                      *, seq_len, grid_w, halo):
    tm, d = h_ref.shape
    q = d // 4
    hext_all = jnp.concatenate([hp_ref[...], h_ref[...], hn_ref[...]], axis=0)
    offs = (-1, 1, -1, 1) if grid_w is None else (-1, 1, -grid_w, grid_w)
    mix = mix_ref[...]
    vec = vec_ref[...]
    w0c, a0c = vec[0:1], vec[1:2]
    k_k, k_a, r_k = vec[2:3, :d], vec[3:4, :d], vec[4:5, :d]
    nh = 2 if tm % 16 == 0 else 1
    th = tm // nh

    def shift_mix(s):
        i = s["i"]
        hext = hext_all[i * th:i * th + th + 2 * halo]
        h = hext[halo:halo + th]
        t = (pl.program_id(0) * tm + i * th + lax.broadcasted_iota(jnp.int32, (th, 1), 0)) % seq_len
        parts = []
        for qi, o in enumerate(offs):
            piece = hext[halo + o:halo + o + th, qi * q:(qi + 1) * q]
            ok = (t + o >= 0) & (t + o < seq_len)
            if grid_w is not None and abs(o) == 1:
                colp = t % grid_w + o
                ok = ok & (colp >= 0) & (colp < grid_w)
            parts.append(jnp.where(ok, piece, 0.0))
        dx = jnp.concatenate(parts, axis=1) - h
        s["x"] = [(h + dx * mix[j:j + 1]).astype(BF16) for j in range(6)]

    def project(s):
        xr, xw, xk, xv, xa, xg = s.pop("x")
        s["r"] = jnp.dot(xr, wrkv_ref[0], preferred_element_type=F32)
        s["k"] = jnp.dot(xk, wrkv_ref[1], preferred_element_type=F32)
        s["v"] = jnp.dot(xv, wrkv_ref[2], preferred_element_type=F32)
        s["wl"] = jnp.dot(xw, w1_ref[...], preferred_element_type=F32)
        s["al"] = jnp.dot(xa, a1_ref[...], preferred_element_type=F32)
        s["gl"] = jnp.dot(xg, g1_ref[...], preferred_element_type=F32)

    def lora_act(s):
        s["wl"] = jnp.tanh(s["wl"]).astype(BF16)
        s["al"] = s["al"].astype(BF16)
        s["gl"] = jax.nn.sigmoid(s["gl"]).astype(BF16)
        s["kk"] = s["k"] * k_k

    def lora_out(s):
        s["w_pre"] = w0c + jnp.dot(s.pop("wl"), w2_ref[...], preferred_element_type=F32)
        s["a_pre"] = a0c + jnp.dot(s.pop("al"), a2_ref[...], preferred_element_type=F32)
        s["g"] = jnp.dot(s.pop("gl"), g2_ref[...], preferred_element_type=F32)
        s["ss"] = _head_sum(s["kk"] * s["kk"])

    def gates(s):
        s["lw"] = (-math.exp(-0.5)) * jax.nn.sigmoid(s.pop("w_pre"))
        a = jax.nn.sigmoid(s.pop("a_pre"))
        s["a_f"], s["a_b"] = a[:, :d], a[:, d:]
        s["kk"] = s["kk"] * lax.rsqrt(jnp.maximum(s.pop("ss"), 1e-24))
        k_bonus = s["k"] * (1.0 + (0.5 * (s["a_f"] + s["a_b"]) - 1.0) * k_a)
        s["rk"] = s["r"] * k_bonus * r_k

    def bonus_sum(s):
        s["rk"] = _head_sum(s["rk"])

    def store(s):
        rows = slice(s["i"] * th, (s["i"] + 1) * th)
        k_ref[rows, :] = s["k"]
        v_ref[rows, :] = s["v"]
        kk_ref[rows, :] = s["kk"]
        r_ref[rows, :] = s["r"]
        lw0_ref[rows, :] = s["lw"][:, :d]
        lw1_ref[rows, :] = s["lw"][:, d:]
        a0_ref[rows, :] = s["a_f"]
        a1o_ref[rows, :] = s["a_b"]
        bonus_ref[rows, :] = s["rk"] * s["v"]
        g_ref[rows, :] = s["g"].astype(g_ref.dtype)

    stages = (shift_mix, project, lora_act, lora_out, gates, bonus_sum, store)
    halves = [{"i": i} for i in range(nh)]
    for step in range(len(stages) + nh - 1):
        for i, s in enumerate(halves):
            if 0 <= step - i < len(stages):
                stages[step - i](s)


def _rwkv_prep(h, seq_len, grid_w, p, tm):
    (mix, w_rkv, w1c, w2bd, a1c, a2bd, g1p, g2p, vec) = p
    m, d = h.shape
    halo = GRID_W
    nb = tm // halo
    last = m // halo - 1
    full = lambda a: pl.BlockSpec(a.shape, lambda i: (0,) * a.ndim, pipeline_mode=pl.Buffered(1))
    tok = pl.BlockSpec((tm, d), lambda i: (i, 0))
    kern = functools.partial(_rwkv_prep_kernel, seq_len=seq_len, grid_w=grid_w, halo=halo)
    return pl.pallas_call(
        kern,
        grid=(m // tm,),
        in_specs=[tok,
                  pl.BlockSpec((halo, d), lambda i: (jnp.maximum(i * nb - 1, 0), 0)),
                  pl.BlockSpec((halo, d), lambda i: (jnp.minimum((i + 1) * nb, last), 0)),
                  full(mix), full(w_rkv), full(w1c), full(w2bd), full(a1c), full(a2bd),
                  full(g1p), full(g2p), full(vec)],
        out_specs=[tok] * 10,
        out_shape=[jax.ShapeDtypeStruct((m, d), F32)] * 9 + [jax.ShapeDtypeStruct((m, d), BF16)],
        compiler_params=_cparams("parallel"),
        name="rwkv_prep",
    )(h, h, h, mix, w_rkv, w1c, w2bd, a1c, a2bd, g1p, g2p, vec)


def _stack_heads(x):
    lane = lax.broadcasted_iota(jnp.int32, x.shape, 1)
    first = lane < RW_HEAD
    return jnp.concatenate([jnp.where(first, x, 0.0), jnp.where(first, 0.0, x)], axis=0)


def _each(f, *lists):
    return [f(*args) for args in zip(*lists)]


def _mm(a, b):
    return jnp.dot(a, b, preferred_element_type=F32)


def _mm_nt(a, b):
    return lax.dot_general(a, b, (((1,), (1,)), ((), ())), preferred_element_type=F32)


def _mm_tn(a, b):
    return lax.dot_general(a, b, (((0,), (0,)), ((), ())), preferred_element_type=F32)


def _to_bf16(x):
    return x.astype(BF16)


INV_BASE = 16


def _row_blocks(x, size, parity):
    return jnp.concatenate([x[b * size:(b + 1) * size] for b in range(x.shape[0] // size) if b % 2 == parity],
                           axis=0)


def _put_row_blocks(xc, size, parity, base):
    out, k = [], 0
    for b in range(2 * xc.shape[0] // size):
        if b % 2 == parity:
            out.append(xc[k * size:(k + 1) * size])
            k += 1
        else:
            out.append(jnp.zeros((size, xc.shape[1]), xc.dtype) if base is None else base[b * size:(b + 1) * size])
    return jnp.concatenate(out, axis=0)


def _unit_tri_inverse(l_mats, ri, ci, reverse):
    nb = INV_BASE
    n = l_mats[0].shape[0]
    b16 = (ri // nb) == (ci // nb)
    b32 = (ri // (2 * nb)) == (ci // (2 * nb))
    sr = lax.broadcasted_iota(jnp.int32, (nb, n), 0)
    sc = lax.broadcasted_iota(jnp.int32, (nb, n), 1)
    eye_s = jnp.where(sr == sc % nb, 1.0, 0.0)
    strip = lambda m: functools.reduce(lambda a, b: a + b, [m[b * nb:(b + 1) * nb] for b in range(n // nb)])
    block_diag = lambda s: jnp.where(b16, jnp.concatenate([s] * (n // nb), axis=0), 0.0)

    l16 = _each(lambda l: jnp.where(b16, l, 0.0), l_mats)
    l16s = _each(strip, l16)
    xs = _each(lambda s: eye_s - s, l16s)
    ps = _each(lambda s, m: _mm(s.astype(BF16), m.astype(BF16)), l16s, l16)
    for it in range(3):
        pbd = _each(lambda s: block_diag(s).astype(BF16), ps)
        if it < 2:
            both = _each(lambda x_, s, p_: _mm(jnp.concatenate([x_, s], axis=0).astype(BF16), p_), xs, ps, pbd)
            xs = _each(lambda x_, b_: x_ + b_[:nb], xs, both)
            ps = _each(lambda b_: b_[nb:], both)
        else:
            xs = _each(lambda x_, p_: x_ + _mm(x_.astype(BF16), p_), xs, pbd)
    x = _each(block_diag, xs)

    par = 0 if reverse else 1
    for size, off_mask in ((nb, b32 & (~b16)), (2 * nb, ~b32)):
        lc = _each(lambda l: _row_blocks(jnp.where(off_mask, l, 0.0), size, par).astype(BF16), l_mats)
        xb = _each(_to_bf16, x)
        t = _each(lambda l_, x_: _put_row_blocks(_mm(l_, x_).astype(BF16), size, par, None), lc, xb)
        xc = _each(lambda x_: _row_blocks(x_, size, par), x)
        r = _each(lambda x_, t_: x_ - _mm(x_.astype(BF16), t_), xc, t)
        x = _each(lambda r_, x_: _put_row_blocks(r_, size, par, x_), r, x)
    return x


def _chunk_cumsum(x, reverse):
    c = x.shape[0]
    row = lax.broadcasted_iota(jnp.int32, x.shape, 0)
    s = 1
    while s < c:
        if reverse:
            x = x + jnp.where(row < c - s, pltpu.roll(x, c - s, axis=0), 0.0)
        else:
            x = x + jnp.where(row >= s, pltpu.roll(x, s, axis=0), 0.0)
        s *= 2
    return x


def _chunk_local(lw, k, v, kk, r, a, k_a, reverse):
    c = lw[0].shape[0]
    n = 2 * c
    kd = _each(lambda k_, a_, ka_: k_ * (1.0 + (a_ - 1.0) * ka_), k, a, k_a)
    ka = _each(lambda kk_, a_: kk_ * a_, kk, a)
    cum = _each(lambda x: _chunk_cumsum(x, reverse), lw)
    tot = _each(lambda x: x[0:1] if reverse else x[c - 1:c], cum)
    e_neg = _each(lambda x: jnp.exp(-x), cum)
    e_end = _each(lambda t_, x: jnp.exp(t_ - x), tot, cum)
    stack_b = lambda x: _stack_heads(x).astype(BF16)
    alpha = _each(lambda kk_, c_, l_: stack_b(kk_ * jnp.exp(c_ - l_)), kk, cum, lw)
    beta = _each(lambda x, e: stack_b(x * e), ka, e_neg)
    kappa = _each(lambda x, e: stack_b(x * e), kd, e_neg)
    rho = _each(lambda r_, c_: _stack_heads(r_ * jnp.exp(c_)), r, cum)
    kappa_e = _each(lambda x, e: stack_b(x * e), kd, e_end)
    beta_e = _each(lambda x, e: stack_b(x * e), ka, e_end)
    v_st = _each(stack_b, v)
    w_end = _each(jnp.exp, tot)

    sc = _each(lambda al, rh, be, kp: _mm_nt(jnp.concatenate([al, rh.astype(BF16)], axis=0),
                                             jnp.concatenate([be, kp], axis=0)), alpha, rho, beta, kappa)
    ri = lax.broadcasted_iota(jnp.int32, (n, n), 0)
    ci = lax.broadcasted_iota(jnp.int32, (n, n), 1)
    strict = (ci > ri) if reverse else (ci < ri)
    incl = (ci >= ri) if reverse else (ci <= ri)
    l_mat = _each(lambda x: jnp.where(strict, x[:n, :n], 0.0), sc)
    apk = _each(lambda x: jnp.concatenate([jnp.where(strict, x[:n, n:], 0.0), jnp.where(incl, x[n:, n:], 0.0)],
                                          axis=0).astype(BF16), sc)
    pb = _each(lambda x: jnp.where(incl, x[n:, :n], 0.0).astype(BF16), sc)

    apkv = _each(_mm, apk, v_st)
    x = _each(_to_bf16, _unit_tri_inverse(l_mat, ri, ci, reverse))
    uwb = _each(lambda x_, kv, al: _mm(x_, jnp.concatenate([kv[:n].astype(BF16), al], axis=1)).astype(BF16),
                x, apkv, alpha)
    yr = _each(lambda kv, rh, pb_, uw: jnp.concatenate([kv[n:], rh], axis=1) - _mm(pb_, uw), apkv, rho, pb, uwb)
    uq = _each(_mm_tn, uwb, beta_e)
    n_t = _each(lambda v_, ke, uq_: _mm_tn(v_, ke) - uq_[:n], v_st, kappa_e, uq)
    q = _each(lambda uq_: uq_[n:].astype(BF16), uq)
    return list(zip(yr, n_t, q, w_end))


def _chunk_state(s, local):
    n = local[0][0].shape[0]
    c = n // 2
    sb = _each(_to_bf16, s)
    y_st = _each(lambda lc, sb_: lc[0][:, :n] + _mm_nt(lc[0][:, n:].astype(BF16), sb_), local, sb)
    y = _each(lambda x_: x_[:c] + x_[c:], y_st)
    s_new = _each(lambda s_, sb_, lc: s_ * lc[3] - _mm(sb_, lc[2]) + lc[1], s, sb, local)
    return s_new, y


SCAN_SLABS = 8
SCAN_PAIR = 4


def _scan_kernel(k_ref, v_ref, kk_ref, r_ref, lw_ref, a_ref, ka_ref, s0_ref, y_ref, sout_ref, s_scr,
                 *, reverse, chunk):
    tb = k_ref.shape[1]
    nslab = s_scr.shape[0]
    nchunk = tb // chunk
    step = pl.program_id(2)

    @pl.when(step == 0)
    def _():
        s_scr[...] = s0_ref[0]

    lanes = [slice(g * LANES, (g + 1) * LANES) for g in range(nslab)]
    pair = math.gcd(SCAN_PAIR, nchunk)

    def body(i, carry):
        cis = [(nchunk - 1 - (i * pair + p)) if reverse else (i * pair + p) for p in range(pair)]
        sls = [pl.ds(pl.multiple_of(ci * chunk, chunk), chunk) for ci in cis]
        get = lambda ref: [ref[0, sl, ln] for sl in sls for ln in lanes]
        local = _chunk_local(get(lw_ref), get(k_ref), get(v_ref), get(kk_ref), get(r_ref), get(a_ref),
                             [ka_ref[:, ln] for _ in sls for ln in lanes], reverse)
        s = [s_scr[g] for g in range(nslab)]
        for p, sl in enumerate(sls):
            s, y = _chunk_state(s, local[p * nslab:(p + 1) * nslab])
            for g in range(nslab):
                y_ref[0, sl, lanes[g]] = y[g]
        for g in range(nslab):
            s_scr[g] = s[g]
        return carry

    lax.fori_loop(0, nchunk // pair, body, 0)

    @pl.when(step == pl.num_programs(2) - 1)
    def _():
        sout_ref[0] = s_scr[...]


def _rwkv_scan(k, v, kk, r, lw, a, k_a, s0, reverse, tb):
    b, t, d = k.shape
    g = SCAN_SLABS
    gw = g * LANES
    nt = t // tb
    tmap = (lambda bi, j, c: (bi, nt - 1 - c, j)) if reverse else (lambda bi, j, c: (bi, c, j))
    tok = pl.BlockSpec((1, tb, gw), tmap)
    st = pl.BlockSpec((1, g, LANES, LANES), lambda bi, j, c: (bi, j, 0, 0))
    kern = functools.partial(_scan_kernel, reverse=reverse, chunk=CHUNK)
    return pl.pallas_call(
        kern,
        grid=(b, d // gw, nt),
        in_specs=[tok] * 6 + [pl.BlockSpec((1, gw), lambda bi, j, c: (0, j)), st],
        out_specs=[tok, st],
        out_shape=[jax.ShapeDtypeStruct((b, t, d), F32),
                   jax.ShapeDtypeStruct((b, d // LANES, LANES, LANES), F32)],
        scratch_shapes=[pltpu.VMEM((g, LANES, LANES), F32)],
        compiler_params=_cparams("parallel", "parallel", "arbitrary"),
        name="rwkv_scan_bwd" if reverse else "rwkv_scan_fwd",
    )(k, v, kk, r, lw, a, k_a, s0)


def _hy_in_kernel(h_ref, hp_ref, hn_ref, w_ref, cw_ref, cb_ref, x1_ref, x2_ref, v_ref, *, seq_len):
    tm, d = h_ref.shape
    halo = hp_ref.shape[0]
    cw = cw_ref[...]
    cb = cb_ref[...]
    nh = 2 if tm % (4 * halo) == 0 else 1
    th = tm // nh
    pes = []
    for i in range(nh):
        before = hp_ref[...] if i == 0 else h_ref[i * th - halo:i * th, :]
        after = hn_ref[...] if i == nh - 1 else h_ref[(i + 1) * th:(i + 1) * th + halo, :]
        hext = jnp.concatenate([before, h_ref[i * th:(i + 1) * th, :], after], axis=0).astype(BF16)
        pes.append(jnp.dot(hext, w_ref[...], preferred_element_type=F32))
    for i, pe in enumerate(pes):
        t = (pl.program_id(0) * tm + i * th + lax.broadcasted_iota(jnp.int32, (th, 1), 0)) % seq_len
        prev = jnp.where(t >= 1, pe[halo - 1:halo - 1 + th], 0.0)
        nxt = jnp.where(t + 1 < seq_len, pe[halo + 1:halo + 1 + th], 0.0)
        u = prev * cw[0:1] + pe[halo:halo + th] * cw[1:2] + nxt * cw[2:3] + cb
        rows = slice(i * th, (i + 1) * th)
        x1_ref[rows, :] = u[:, :d]
        x2_ref[rows, :] = u[:, d:2 * d]
        v_ref[rows, :] = u[:, 2 * d:]


def _hy_in(h, seq_len, w_in, conv_w, conv_b, tm):
    m, d = h.shape
    halo = 8
    nb = tm // halo
    last = m // halo - 1
    tok = pl.BlockSpec((tm, d), lambda i: (i, 0))
    full = lambda a: pl.BlockSpec(a.shape, lambda i: (0,) * a.ndim, pipeline_mode=pl.Buffered(1))
    return pl.pallas_call(
        functools.partial(_hy_in_kernel, seq_len=seq_len),
        grid=(m // tm,),
        in_specs=[tok,
                  pl.BlockSpec((halo, d), lambda i: (jnp.maximum(i * nb - 1, 0), 0)),
                  pl.BlockSpec((halo, d), lambda i: (jnp.minimum((i + 1) * nb, last), 0)),
                  full(w_in), full(conv_w), full(conv_b)],
        out_specs=[tok] * 3,
        out_shape=[jax.ShapeDtypeStruct((m, d), F32)] * 3,
        compiler_params=_cparams("parallel"),
        name="hyena_in",
    )(h, h, h, w_in, conv_w, conv_b)


def _hy_filter_kernel(z_ref, w1_ref, b1_ref, fr_ref, w2_ref, b2_ref, w3_ref, dl_ref, o_ref):
    hp = lax.Precision.HIGHEST
    z = z_ref[...]
    fr = fr_ref[...]
    hid = jnp.sin(fr[0:1] * (jnp.dot(z, w1_ref[...], precision=hp, preferred_element_type=F32) + b1_ref[...]))
    hid = jnp.sin(fr[1:2] * (jnp.dot(hid, w2_ref[...], precision=hp, preferred_element_type=F32) + b2_ref[...]))
    filt = jnp.dot(hid, w3_ref[...], precision=hp, preferred_element_type=F32)
    o_ref[...] = filt * jnp.exp(-z[:, 0:1] * jnp.abs(dl_ref[...]))


def _hy_filters(seq_len, f_w1, f_b1, f_freq, f_w2, f_b2, f_w3, deltas):
    t = jnp.linspace(0.0, 1.0, seq_len, dtype=F32)[:, None]
    ang = ((2 * math.pi / seq_len) * jnp.arange(seq_len, dtype=F32)[:, None]
           * jnp.linspace(1e-4, HY_BANDS - 1, HY_BANDS, dtype=F32)[None])
    z = jnp.concatenate([t, jnp.cos(ang), -jnp.sin(ang)], axis=-1)
    emb, hid = f_w1.shape
    z = jnp.pad(z, ((0, 0), (0, LANES - emb)))
    w1 = jnp.pad(f_w1, ((0, LANES - emb), (0, 0)))
    n_out = f_w3.shape[1]
    tl = min(seq_len, 512)
    full = lambda a: pl.BlockSpec(a.shape, lambda i: (0,) * a.ndim)
    args = (w1, f_b1.reshape(1, hid), f_freq, f_w2, f_b2.reshape(1, hid), f_w3, deltas.reshape(1, n_out))
    return pl.pallas_call(
        _hy_filter_kernel,
        grid=(seq_len // tl,),
        in_specs=[pl.BlockSpec((tl, LANES), lambda i: (i, 0))] + [full(a) for a in args],
        out_specs=pl.BlockSpec((tl, n_out), lambda i: (i, 0)),
        out_shape=jax.ShapeDtypeStruct((seq_len, n_out), F32),
        compiler_params=_cparams("parallel"),
        name="hyena_filters",
    )(z, *args)


DFT_UNROLL = 8


def _rdft_tables(seq_len):
    big = 2 * seq_len
    n = int(round(math.sqrt(big)))
    assert n * n == big and n % 32 == 0, "sequence length must give a square, tile-aligned DFT"
    kh = n // 2 + 1
    kp = -(-kh // 8) * 8
    k1 = np.arange(kp)[None, :, None]
    n1 = np.arange(n // 2)[None, None, :]
    n2 = np.arange(n)[:, None, None]
    ang = -2.0 * np.pi * ((k1 * (n * n1 + n2)) % big) / big
    live = (k1 < kh).astype(np.float64)
    e_in = np.concatenate([np.cos(ang) * live, np.sin(ang) * live], axis=1)
    wgt = np.where((k1 == 0) | (k1 == n // 2), 1.0, 2.0) * live
    e_out = np.transpose(np.concatenate([np.cos(ang) * wgt, np.sin(ang) * wgt], axis=1), (0, 2, 1))
    a2 = -2.0 * np.pi * ((np.arange(n)[:, None] * np.arange(n)[None, :]) % n) / n
    fr, fi = np.cos(a2), np.sin(a2)
    f_fwd = np.block([[fr, -fi], [fi, fr]])
    f_inv = np.block([[fr, fi], [-fi, fr]])
    kg = max(g for g in range(1, 21) if kh % g == 0)
    cast = lambda a: jnp.asarray(a, dtype=F32).astype(BF16)
    return dict(n=n, kh=kh, kp=kp, kg=kg, e_in=cast(e_in), e_out=cast(e_out), f_fwd=cast(f_fwd), f_inv=cast(f_inv))


ROW_PAD = 8


def _rdft_first_stage(u_ref, e_in_ref, x_pad, a_scr, n, kp):
    pitch = n + ROW_PAD
    for n1 in range(n // 2):
        x_pad[n1 * pitch:n1 * pitch + n, :] = u_ref[0, n1 * n:(n1 + 1) * n, :]

    def body(i, carry):
        n2s = [i * DFT_UNROLL + j for j in range(DFT_UNROLL)]
        xs = [x_pad[pl.ds(n2, n // 2, stride=pitch), :].astype(BF16) for n2 in n2s]
        acc = [_mm(e_in_ref[n2], x) for n2, x in zip(n2s, xs)]
        for n2, a in zip(n2s, acc):
            a_scr[0, pl.ds(n2, kp, stride=pitch), :] = a[:kp]
            a_scr[1, pl.ds(n2, kp, stride=pitch), :] = a[kp:]
        return carry

    lax.fori_loop(0, n // DFT_UNROLL, body, 0)


def _rdft_second_stage(a_scr, k1s, n, f_fwd):
    pitch = n + ROW_PAD
    cols = []
    for k1 in k1s:
        rows = pl.ds(pl.multiple_of(k1 * pitch, 8), n)
        cols.append(jnp.concatenate([a_scr[0, rows, :], a_scr[1, rows, :]], axis=0).astype(BF16))
    return [_mm(f_fwd, c) for c in cols]


def _hy_conv_kernel(u_ref, gate_ref, h_ref, ein_ref, eout_ref, ff_ref, fi_ref, bias_ref, o_ref, x_pad, a_scr,
                    *, n, kp, kh, mg):
    _rdft_first_stage(u_ref, ein_ref, x_pad, a_scr, n, kp)

    f_fwd, f_inv = ff_ref[...], fi_ref[...]

    pitch = n + ROW_PAD

    def mid(i, carry):
        for j0 in range(0, mg, DFT_UNROLL):
            k1s = [i * mg + j for j in range(j0, min(j0 + DFT_UNROLL, mg))]
            xs = _rdft_second_stage(a_scr, k1s, n, f_fwd)
            ys = []
            for k1, x in zip(k1s, xs):
                xr, xi = x[:n], x[n:]
                hr, hi = h_ref[0, k1], h_ref[1, k1]
                ys.append(jnp.concatenate([xr * hr - xi * hi, xr * hi + xi * hr], axis=0).astype(BF16))
            zs = [_mm(f_inv, y) for y in ys]
            for k1, z in zip(k1s, zs):
                rows = pl.ds(pl.multiple_of(k1 * pitch, 8), n)
                a_scr[0, rows, :] = z[:n]
                a_scr[1, rows, :] = z[n:]
        return carry

    lax.fori_loop(0, kh // mg, mid, 0)

    inv_n = 1.0 / (n * n)
    bias = bias_ref[...]

    def last(i, carry):
        n2s = [i * DFT_UNROLL + j for j in range(DFT_UNROLL)]
        zc = []
        for n2 in n2s:
            rows = pl.ds(n2, kp, stride=pitch)
            zc.append(jnp.concatenate([a_scr[0, rows, :], a_scr[1, rows, :]], axis=0).astype(BF16))
        ys = [_mm(eout_ref[n2], z) * inv_n for n2, z in zip(n2s, zc)]
        for n2, y in zip(n2s, ys):
            x_pad[pl.ds(n2, n // 2, stride=pitch), :] = y
        return carry

    lax.fori_loop(0, n // DFT_UNROLL, last, 0)
    for n1 in range(n // 2):
        rows = slice(n1 * n, (n1 + 1) * n)
        y = x_pad[n1 * pitch:n1 * pitch + n, :]
        o_ref[0, rows, :] = gate_ref[0, rows, :] * (y + u_ref[0, rows, :] * bias)


def _hy_conv(u, gate, spec, f_idx, bias, tb):
    b, l, d = u.shape
    n, kh, kp, mg = tb["n"], tb["kh"], tb["kp"], tb["kg"]
    nd = d // LANES
    once = dict(pipeline_mode=pl.Buffered(1))
    tok = lambda bufs: pl.BlockSpec((1, l, LANES), lambda c, bi: (bi, 0, c), **bufs)
    full = lambda a: pl.BlockSpec(a.shape, lambda c, bi: (0,) * a.ndim, **once)
    return pl.pallas_call(
        functools.partial(_hy_conv_kernel, n=n, kp=kp, kh=kh, mg=mg),
        grid=(nd, b),
        in_specs=[tok({}), tok(once),
                  pl.BlockSpec((2, kh, n, LANES), lambda c, bi: (0, 0, 0, f_idx * nd + c), **once),
                  full(tb["e_in"]), full(tb["e_out"]), full(tb["f_fwd"]), full(tb["f_inv"]),
                  pl.BlockSpec((1, LANES), lambda c, bi: (0, c))],
        out_specs=tok({}),
        out_shape=jax.ShapeDtypeStruct((b, l, d), F32),
        scratch_shapes=[pltpu.VMEM((n // 2 * (n + ROW_PAD), LANES), F32), pltpu.VMEM((2, kp * (n + ROW_PAD), LANES), F32)],
        compiler_params=_cparams("parallel", "parallel"),
        name="hyena_conv",
    )(u, gate, spec, tb["e_in"], tb["e_out"], tb["f_fwd"], tb["f_inv"], bias.reshape(1, d))


def _hy_spec_kernel(u_ref, ein_ref, ff_ref, h_ref, x_pad, a_scr, *, n, kp, kg):
    step = pl.program_id(1)

    @pl.when(step == 0)
    def _():
        _rdft_first_stage(u_ref, ein_ref, x_pad, a_scr, n, kp)

    f_fwd = ff_ref[...]
    for j0 in range(0, kg, DFT_UNROLL):
        js = list(range(j0, min(j0 + DFT_UNROLL, kg)))
        xs = _rdft_second_stage(a_scr, [step * kg + j for j in js], n, f_fwd)
        for j, x in zip(js, xs):
            h_ref[0, j] = x[:n]
            h_ref[1, j] = x[n:]


def _hy_spectrum(filt, tb):
    _, l, dx = filt.shape
    n, kh, kp, kg = tb["n"], tb["kh"], tb["kp"], tb["kg"]
    full = lambda a: pl.BlockSpec(a.shape, lambda c, g: (0,) * a.ndim, pipeline_mode=pl.Buffered(1))
    return pl.pallas_call(
        functools.partial(_hy_spec_kernel, n=n, kp=kp, kg=kg),
        grid=(dx // LANES, kh // kg),
        in_specs=[pl.BlockSpec((1, l, LANES), lambda c, g: (0, 0, c)), full(tb["e_in"]), full(tb["f_fwd"])],
        out_specs=pl.BlockSpec((2, kg, n, LANES), lambda c, g: (0, g, 0, c)),
        out_shape=jax.ShapeDtypeStruct((2, kh, n, dx), F32),
        scratch_shapes=[pltpu.VMEM((n // 2 * (n + ROW_PAD), LANES), F32), pltpu.VMEM((2, kp * (n + ROW_PAD), LANES), F32)],
        compiler_params=_cparams("parallel", "arbitrary"),
        name="hyena_filter_spectrum",
    )(filt, tb["e_in"], tb["f_fwd"])


def _hyena_mixer(hl, batch, hp, tm):
    (w_in, conv_w, conv_b, f_w1, f_b1, f_freq, f_w2, f_b2, f_w3, deltas, bias) = hp
    m, d = hl.shape
    seq_len = m // batch
    x1, x2, v = _hy_in(hl, seq_len, w_in.astype(BF16), conv_w, conv_b.reshape(1, -1), tm)
    three = lambda a: a.reshape(batch, seq_len, d)
    x1, x2, v = three(x1), three(x2), three(v)
    tables = _rdft_tables(seq_len)
    filt = _hy_filters(seq_len, f_w1, f_b1, f_freq, f_w2, f_b2, f_w3, deltas)
    spec = _hy_spectrum(filt[None], tables)
    z = _hy_conv(v, x1, spec, 0, bias[0], tables)
    z = _hy_conv(z, x2, spec, 1, bias[1], tables)
    return z.reshape(m, d)


def _rwkv_weights(mix, w_rkv, w0, w1, w2, a0, a1, a2, g1, g2, k_k, k_a, r_k):
    d = mix.shape[-1]
    lora = w1.shape[-1]
    zero = jnp.zeros((lora, d), F32)
    blockdiag = lambda m: jnp.concatenate([jnp.concatenate([m[0], zero], axis=1),
                                           jnp.concatenate([zero, m[1]], axis=1)], axis=0)
    gl = g1.shape[-1]
    glp = -(-gl // LANES) * LANES
    vec = jnp.zeros((8, 2 * d), F32)
    vec = vec.at[0].set(w0.reshape(-1)).at[1].set(a0.reshape(-1))
    vec = vec.at[2, :d].set(k_k).at[3, :d].set(k_a).at[4, :d].set(r_k.reshape(-1))
    return (mix, w_rkv.astype(BF16),
            jnp.concatenate([w1[0], w1[1]], axis=1).astype(BF16), blockdiag(w2).astype(BF16),
            jnp.concatenate([a1[0], a1[1]], axis=1).astype(BF16), blockdiag(a2).astype(BF16),
            jnp.pad(g1, ((0, 0), (0, glp - gl))).astype(BF16),
            jnp.pad(g2, ((0, glp - gl), (0, 0))).astype(BF16), vec)


def _rwkv_mixer(hl, hc, batch, wts, k_a, tm, tb):
    d = hl.shape[1]
    t_lat, t_ctx = hl.shape[0] // batch, hc.shape[0] // batch
    k_a = k_a.reshape(1, d)
    s0 = jnp.zeros((batch, d // LANES, LANES, LANES), F32)
    three = lambda arrs, t: [a.reshape(batch, t, d) for a in arrs]
    kc, vc, kkc, rc, lw0c, lw1c, a0c, a1c, _, _ = _rwkv_prep(hc, t_ctx, None, wts, min(tm, t_ctx))
    kc, vc, kkc, rc, lw0c, lw1c, a0c, a1c = three((kc, vc, kkc, rc, lw0c, lw1c, a0c, a1c), t_ctx)
    tbc = min(tb, t_ctx)
    _, s_f = _rwkv_scan(kc, vc, kkc, rc, lw0c, a0c, k_a, s0, False, tbc)
    _, s_b = _rwkv_scan(kc, vc, kkc, rc, lw1c, a1c, k_a, s0, True, tbc)
    k, v, kk, r, lw0, lw1, a0, a1, bonus, g = _rwkv_prep(hl, t_lat, GRID_W, wts, tm)
    k, v, kk, r, lw0, lw1, a0, a1 = three((k, v, kk, r, lw0, lw1, a0, a1), t_lat)
    yf, _ = _rwkv_scan(k, v, kk, r, lw0, a0, k_a, s_f, False, tb)
    yb, _ = _rwkv_scan(k, v, kk, r, lw1, a1, k_a, s_b, True, tb)
    return yf.reshape(-1, d), yb.reshape(-1, d), bonus, g


TOKEN_TILE = 512
PREP_TILE = 256
SCAN_BLOCK = 512


def kernel(x, c, ctx, c_ctx, mod_w, mod_b, norm_g, ffn_w_gu, ffn_w_down, rw_mix, rw_w_rkv, rw_w_o, rw_w0, rw_w1, rw_w2, rw_a0, rw_a1, rw_a2, rw_g1, rw_g2, rw_k_k, rw_k_a, rw_r_k, rw_ln_g, rw_ln_b, hy_w_in, hy_conv_w, hy_conv_b, hy_f_w1, hy_f_b1, hy_f_freq, hy_f_w2, hy_f_b2, hy_f_w3, hy_deltas, hy_bias, hy_w_out, final_g):
    batch, seq, d = x.shape
    t_ctx = ctx.shape[1]
    depth = mod_w.shape[0]
    assert depth == 2 and rw_mix.shape[0] == 1 and hy_w_in.shape[0] == 1
    assert seq % TOKEN_TILE == 0 and seq % GRID_W == 0 and t_ctx % CHUNK == 0 and batch + 1 <= 8

    cc = jnp.concatenate([c, c_ctx[None], jnp.zeros((8 - batch - 1, d), F32)], axis=0)
    mods0 = _modulation(cc, mod_w, mod_b, 0)
    mods1 = _modulation(cc, mod_w, mod_b, 1)
    wgu = ffn_w_gu.astype(BF16)
    wd = ffn_w_down.astype(BF16)
    xl = x.reshape(batch * seq, d)
    xc = ctx.reshape(batch * t_ctx, d)
    tm = TOKEN_TILE
    tmc = min(tm, t_ctx)

    xl, hl = _ffn(xl, mods0, 0, seq, norm_g[0, 0:2], wgu, wd, 0, 0, 0, 3, tm)
    _, hc = _ffn(xc, mods0, batch, batch * t_ctx, norm_g[0, 0:2], wgu, wd, 0, 0, 0, 3, tmc)
    wts = _rwkv_weights(rw_mix[0], rw_w_rkv[0], rw_w0[0], rw_w1[0], rw_w2[0], rw_a0[0], rw_a1[0], rw_a2[0],
                        rw_g1[0], rw_g2[0], rw_k_k[0], rw_k_a[0], rw_r_k[0])
    yf, yb, bonus, g = _rwkv_mixer(hl, hc, batch, wts, rw_k_a[0], PREP_TILE, SCAN_BLOCK)
    rwkv_res = ("rwkv", yf, yb, bonus, g, jnp.stack([rw_ln_g[0], rw_ln_b[0]]), rw_w_o[0].astype(BF16))
    xl, _ = _ffn(xl, mods0, 0, seq, jnp.stack([norm_g[0, 2], final_g]), wgu, wd, 0, 1, 6, None, tm,
                 pre=rwkv_res)

    xl, hl = _ffn(xl, mods1, 0, seq, norm_g[1, 0:2], wgu, wd, 1, 0, 0, 3, tm)
    hp = (hy_w_in[0], hy_conv_w[0], hy_conv_b[0], hy_f_w1[0], hy_f_b1[0], hy_f_freq[0], hy_f_w2[0],
          hy_f_b2[0], hy_f_w3[0], hy_deltas[0], hy_bias[0])
    z = _hyena_mixer(hl, batch, hp, tm)
    out, _ = _ffn(xl, mods1, 0, seq, jnp.stack([norm_g[1, 2], final_g]), wgu, wd, 1, 1, 6, "final", tm,
                  pre=("proj", z, hy_w_out[0].astype(BF16)))
    return out.reshape(batch, seq, d)
```

```python
import functools
import math

import jax
import jax.numpy as jnp
import numpy as np
from jax import lax
from jax.experimental import pallas as pl
from jax.experimental.pallas import tpu as pltpu

F32 = jnp.float32
BF16 = jnp.bfloat16

GRID_W = 64
RW_HEAD = 64
NORM_EPS = 1e-6
GN_EPS = 64e-5
HY_BANDS = 16

LANES = 128
CHUNK = 64
VMEM_LIMIT = 56 * 1024 * 1024


def _cparams(*sem):
    return pltpu.CompilerParams(dimension_semantics=sem, vmem_limit_bytes=VMEM_LIMIT)


def _bdot(a, b):
    return jnp.dot(a.astype(BF16), b.astype(BF16), preferred_element_type=F32)


def _rms_mod(x, g, shift, scale):
    ms = jnp.mean(x * x, axis=-1, keepdims=True)
    return (x * lax.rsqrt(ms + NORM_EPS) * g) * (1.0 + scale) + shift


def _mod_kernel(c_ref, w_ref, b_ref, o_ref):
    c = c_ref[...]
    o_ref[...] = _bdot(c * jax.nn.sigmoid(c), w_ref[...]) + b_ref[...]


def _modulation(cc, w, b, layer):
    m, d = cc.shape
    depth, _, n = w.shape
    tn = 1152
    out = pl.pallas_call(
        _mod_kernel,
        grid=(n // tn,),
        in_specs=[pl.BlockSpec((m, d), lambda j: (0, 0)),
                  pl.BlockSpec((None, d, tn), lambda j: (layer, 0, j)),
                  pl.BlockSpec((None, 1, tn), lambda j: (layer, 0, j))],
        out_specs=pl.BlockSpec((m, tn), lambda j: (0, j)),
        out_shape=jax.ShapeDtypeStruct((m, n), F32),
        compiler_params=_cparams("parallel"),
        name="modulation",
    )(cc, w, b.reshape(depth, 1, n))
    return out.reshape(m, 9, d)


def _mixer_residual(pre, extras, rows, x, gate):
    if pre == "proj":
        z_ref, wo_ref = extras
        z = z_ref[rows, :].astype(BF16)
    else:
        yf_ref, yb_ref, bonus_ref, gt_ref, ln_ref, wo_ref = extras
        y = yf_ref[rows, :] + yb_ref[rows, :]
        ln = ln_ref[...]
        inv_n = 1.0 / RW_HEAD
        yc = y - _head_sum(y) * inv_n
        var = _head_sum(yc * yc) * inv_n
        yn = yc * lax.rsqrt(var + GN_EPS) * ln[0:1] + ln[1:2]
        z = ((yn + bonus_ref[rows, :]) * gt_ref[rows, :]).astype(BF16)
    return x + gate * jnp.dot(z, wo_ref[...], preferred_element_type=F32)


def _ffn_kernel(*refs, mod_off, f_chunk, next_off, pre):
    n_extra = {None: 0, "proj": 2, "rwkv": 6}[pre]
    x_ref, extras = refs[0], refs[1:1 + n_extra]
    mod_ref, g_ref, wgu_ref, wd_ref = refs[1 + n_extra:5 + n_extra]
    out_refs = refs[5 + n_extra:]
    tm = x_ref.shape[0]
    mod = mod_ref[0]
    g = g_ref[...]
    f = wd_ref.shape[0]
    nh = 2 if tm % 32 == 0 else 1
    rows = [slice(i * tm // nh, (i + 1) * tm // nh) for i in range(nh)]
    xs, hs, gu, accs = [None] * nh, [None] * nh, [None] * nh, [None] * nh
    for c0 in range(0, f, f_chunk):
        for i in range(nh):
            if c0 == 0:
                xs[i] = x_ref[rows[i], :]
                if pre is not None:
                    xs[i] = _mixer_residual(pre, extras, rows[i], xs[i], mod[5:6])
                hs[i] = _rms_mod(xs[i], g[0:1], mod[mod_off:mod_off + 1], mod[mod_off + 1:mod_off + 2]).astype(BF16)
            gu[i] = (jnp.dot(hs[i], wgu_ref[:, c0:c0 + f_chunk], preferred_element_type=F32),
                     jnp.dot(hs[i], wgu_ref[:, f + c0:f + c0 + f_chunk], preferred_element_type=F32))
        for i in range(nh):
            gate, up = gu[i]
            a = (gate * jax.nn.sigmoid(gate) * up).astype(BF16)
            down = jnp.dot(a, wd_ref[c0:c0 + f_chunk, :], preferred_element_type=F32)
            accs[i] = down if c0 == 0 else accs[i] + down
    for i in range(nh):
        xn = xs[i] + (0.5 * mod[mod_off + 2:mod_off + 3]) * accs[i]
        if next_off == "final":
            ms = jnp.mean(xn * xn, axis=-1, keepdims=True)
            out_refs[0][rows[i], :] = xn * lax.rsqrt(ms + NORM_EPS) * g[1:2]
            continue
        out_refs[0][rows[i], :] = xn
        if next_off is not None:
            out_refs[1][rows[i], :] = _rms_mod(xn, g[1:2], mod[next_off:next_off + 1],
                                               mod[next_off + 1:next_off + 2])


def _ffn(x, mods, mod_row0, rows_per_mod, norm_g2, wgu, wd, layer, which, mod_off, next_off, tm, pre=None):
    m, d = x.shape
    f = wd.shape[2]
    f_chunk = f // 2 if (f // 2) % LANES == 0 else f
    n_out = 2 if isinstance(next_off, int) else 1
    kind = None if pre is None else pre[0]
    kern = functools.partial(_ffn_kernel, mod_off=mod_off, f_chunk=f_chunk, next_off=next_off, pre=kind)
    tok = pl.BlockSpec((tm, d), lambda i: (i, 0))
    extras, extra_specs = [], []
    for a in (() if pre is None else pre[1:]):
        extras.append(a)
        whole = pl.BlockSpec(a.shape, lambda i: (0, 0), pipeline_mode=pl.Buffered(1))
        extra_specs.append(tok if a.shape == (m, d) else whole)
    outs = pl.pallas_call(
        kern,
        grid=(m // tm,),
        in_specs=[tok] + extra_specs + [
                  pl.BlockSpec((1, 9, d), lambda i: (mod_row0 + (i * tm) // rows_per_mod, 0, 0)),
                  pl.BlockSpec((2, d), lambda i: (0, 0)),
                  pl.BlockSpec((None, None) + wgu.shape[2:], lambda i: (layer, which, 0, 0),
                               pipeline_mode=pl.Buffered(1)),
                  pl.BlockSpec((None, None) + wd.shape[2:], lambda i: (layer, which, 0, 0),
                               pipeline_mode=pl.Buffered(1))],
        out_specs=[tok] * n_out,
        out_shape=[jax.ShapeDtypeStruct((m, d), F32)] * n_out,
        compiler_params=_cparams("parallel"),
        name="ffn_halfstep",
    )(x, *extras, mods, norm_g2, wgu, wd)
    return outs if n_out == 2 else (outs[0], None)


def _head_sum(x):
    row = lax.broadcasted_iota(jnp.int32, (LANES, LANES), 0) // RW_HEAD
    col = lax.broadcasted_iota(jnp.int32, (LANES, LANES), 1) // RW_HEAD
    ones_bd = jnp.where(row == col, 1.0, 0.0).astype(BF16)
    hi = x.astype(BF16)
    lo = (x - hi.astype(F32)).astype(BF16)
    outs = []
    for j in range(x.shape[1] // LANES):
        sl = slice(j * LANES, (j + 1) * LANES)
        outs.append(jnp.dot(hi[:, sl], ones_bd, preferred_element_type=F32)
                    + jnp.dot(lo[:, sl], ones_bd, preferred_element_type=F32))
    return jnp.concatenate(outs, axis=1)


def _rwkv_prep_kernel(h_ref, hp_ref, hn_ref, mix_ref, wrkv_ref, w1_ref, w2_ref, a1_ref, a2_ref,
                      g1_ref, g2_ref, vec_ref,
                      k_ref, v_ref, kk_ref, r_ref, lw0_ref, lw1_ref, a0_ref, a1o_ref, bonus_ref, g_ref,
                      *, seq_len, grid_w, halo):
    tm, d = h_ref.shape
    q = d // 4
    hext_all = jnp.concatenate([hp_ref[...], h_ref[...], hn_ref[...]], axis=0)
    offs = (-1, 1, -1, 1) if grid_w is None else (-1, 1, -grid_w, grid_w)
    mix = mix_ref[...]
    vec = vec_ref[...]
    w0c, a0c = vec[0:1], vec[1:2]
    k_k, k_a, r_k = vec[2:3, :d], vec[3:4, :d], vec[4:5, :d]
    nh = 2 if tm % 16 == 0 else 1
    th = tm // nh

    def shift_mix(s):
        i = s["i"]
        hext = hext_all[i * th:i * th + th + 2 * halo]
        h = hext[halo:halo + th]
        t = (pl.program_id(0) * tm + i * th + lax.broadcasted_iota(jnp.int32, (th, 1), 0)) % seq_len
        parts = []
        for qi, o in enumerate(offs):
            piece = hext[halo + o:halo + o + th, qi * q:(qi + 1) * q]
            ok = (t + o >= 0) & (t + o < seq_len)
            if grid_w is not None and abs(o) == 1:
                colp = t % grid_w + o
                ok = ok & (colp >= 0) & (colp < grid_w)
            parts.append(jnp.where(ok, piece, 0.0))
        dx = jnp.concatenate(parts, axis=1) - h
        s["x"] = [(h + dx * mix[j:j + 1]).astype(BF16) for j in range(6)]

    def project(s):
        xr, xw, xk, xv, xa, xg = s.pop("x")
        s["r"] = jnp.dot(xr, wrkv_ref[0], preferred_element_type=F32)
        s["k"] = jnp.dot(xk, wrkv_ref[1], preferred_element_type=F32)
        s["v"] = jnp.dot(xv, wrkv_ref[2], preferred_element_type=F32)
        s["wl"] = jnp.dot(xw, w1_ref[...], preferred_element_type=F32)
        s["al"] = jnp.dot(xa, a1_ref[...], preferred_element_type=F32)
        s["gl"] = jnp.dot(xg, g1_ref[...], preferred_element_type=F32)

    def lora_act(s):
        s["wl"] = jnp.tanh(s["wl"]).astype(BF16)
        s["al"] = s["al"].astype(BF16)
        s["gl"] = jax.nn.sigmoid(s["gl"]).astype(BF16)
        s["kk"] = s["k"] * k_k

    def lora_out(s):
        s["w_pre"] = w0c + jnp.dot(s.pop("wl"), w2_ref[...], preferred_element_type=F32)
        s["a_pre"] = a0c + jnp.dot(s.pop("al"), a2_ref[...], preferred_element_type=F32)
        s["g"] = jnp.dot(s.pop("gl"), g2_ref[...], preferred_element_type=F32)
        s["ss"] = _head_sum(s["kk"] * s["kk"])

    def gates(s):
        s["lw"] = (-math.exp(-0.5)) * jax.nn.sigmoid(s.pop("w_pre"))
        a = jax.nn.sigmoid(s.pop("a_pre"))
        s["a_f"], s["a_b"] = a[:, :d], a[:, d:]
        s["kk"] = s["kk"] * lax.rsqrt(jnp.maximum(s.pop("ss"), 1e-24))
        k_bonus = s["k"] * (1.0 + (0.5 * (s["a_f"] + s["a_b"]) - 1.0) * k_a)
        s["rk"] = s["r"] * k_bonus * r_k

    def bonus_sum(s):
        s["rk"] = _head_sum(s["rk"])

    def store(s):
        rows = slice(s["i"] * th, (s["i"] + 1) * th)
        k_ref[rows, :] = s["k"].astype(k_ref.dtype)
        v_ref[rows, :] = s["v"].astype(v_ref.dtype)
        kk_ref[rows, :] = s["kk"].astype(kk_ref.dtype)
        r_ref[rows, :] = s["r"].astype(r_ref.dtype)
        lw0_ref[rows, :] = s["lw"][:, :d]
        lw1_ref[rows, :] = s["lw"][:, d:]
        a0_ref[rows, :] = s["a_f"].astype(a0_ref.dtype)
        a1o_ref[rows, :] = s["a_b"].astype(a1o_ref.dtype)
        bonus_ref[rows, :] = s["rk"] * s["v"]
        g_ref[rows, :] = s["g"].astype(g_ref.dtype)

    stages = (shift_mix, project, lora_act, lora_out, gates, bonus_sum, store)
    halves = [{"i": i} for i in range(nh)]
    for step in range(len(stages) + nh - 1):
        for i, s in enumerate(halves):
            if 0 <= step - i < len(stages):
                stages[step - i](s)


def _rwkv_prep(h, seq_len, grid_w, p, tm):
    (mix, w_rkv, w1c, w2bd, a1c, a2bd, g1p, g2p, vec) = p
    m, d = h.shape
    halo = GRID_W
    nb = tm // halo
    last = m // halo - 1
    full = lambda a: pl.BlockSpec(a.shape, lambda i: (0,) * a.ndim, pipeline_mode=pl.Buffered(1))
    tok = pl.BlockSpec((tm, d), lambda i: (i, 0))
    kern = functools.partial(_rwkv_prep_kernel, seq_len=seq_len, grid_w=grid_w, halo=halo)
    return pl.pallas_call(
        kern,
        grid=(m // tm,),
        in_specs=[tok,
                  pl.BlockSpec((halo, d), lambda i: (jnp.maximum(i * nb - 1, 0), 0)),
                  pl.BlockSpec((halo, d), lambda i: (jnp.minimum((i + 1) * nb, last), 0)),
                  full(mix), full(w_rkv), full(w1c), full(w2bd), full(a1c), full(a2bd),
                  full(g1p), full(g2p), full(vec)],
        out_specs=[tok] * 10,
        out_shape=[jax.ShapeDtypeStruct((m, d), dt) for dt in (BF16,) * 4 + (F32,) * 2 + (BF16,) * 2 + (F32, BF16)],
        compiler_params=_cparams("parallel"),
        name="rwkv_prep",
    )(h, h, h, mix, w_rkv, w1c, w2bd, a1c, a2bd, g1p, g2p, vec)


def _stack_heads(x):
    lane = lax.broadcasted_iota(jnp.int32, x.shape, 1)
    first = lane < RW_HEAD
    return jnp.concatenate([jnp.where(first, x, 0.0), jnp.where(first, 0.0, x)], axis=0)


def _each(f, *lists):
    return [f(*args) for args in zip(*lists)]


def _mm(a, b):
    return jnp.dot(a, b, preferred_element_type=F32)


def _mm_nt(a, b):
    return lax.dot_general(a, b, (((1,), (1,)), ((), ())), preferred_element_type=F32)


def _mm_tn(a, b):
    return lax.dot_general(a, b, (((0,), (0,)), ((), ())), preferred_element_type=F32)


def _to_bf16(x):
    return x.astype(BF16)


INV_BASE = 16


def _row_blocks(x, size, parity):
    return jnp.concatenate([x[b * size:(b + 1) * size] for b in range(x.shape[0] // size) if b % 2 == parity],
                           axis=0)


def _put_row_blocks(xc, size, parity, base):
    out, k = [], 0
    for b in range(2 * xc.shape[0] // size):
        if b % 2 == parity:
            out.append(xc[k * size:(k + 1) * size])
            k += 1
        else:
            out.append(jnp.zeros((size, xc.shape[1]), xc.dtype) if base is None else base[b * size:(b + 1) * size])
    return jnp.concatenate(out, axis=0)


def _unit_tri_inverse(l_mats, ri, ci, reverse):
    nb = INV_BASE
    n = l_mats[0].shape[0]
    b16 = (ri // nb) == (ci // nb)
    b32 = (ri // (2 * nb)) == (ci // (2 * nb))
    sr = lax.broadcasted_iota(jnp.int32, (nb, n), 0)
    sc = lax.broadcasted_iota(jnp.int32, (nb, n), 1)
    eye_s = jnp.where(sr == sc % nb, 1.0, 0.0)
    strip = lambda m: functools.reduce(lambda a, b: a + b, [m[b * nb:(b + 1) * nb] for b in range(n // nb)])
    block_diag = lambda s: jnp.where(b16, jnp.concatenate([s] * (n // nb), axis=0), 0.0)

    l16 = _each(lambda l: jnp.where(b16, l, 0.0), l_mats)
    l16s = _each(strip, l16)
    xs = _each(lambda s: eye_s - s, l16s)
    ps = _each(lambda s, m: _mm(s.astype(BF16), m.astype(BF16)), l16s, l16)
    for it in range(3):
        pbd = _each(lambda s: block_diag(s).astype(BF16), ps)
        if it < 2:
            both = _each(lambda x_, s, p_: _mm(jnp.concatenate([x_, s], axis=0).astype(BF16), p_), xs, ps, pbd)
            xs = _each(lambda x_, b_: x_ + b_[:nb], xs, both)
            ps = _each(lambda b_: b_[nb:], both)
        else:
            xs = _each(lambda x_, p_: x_ + _mm(x_.astype(BF16), p_), xs, pbd)
    x = _each(block_diag, xs)

    par = 0 if reverse else 1
    for size, off_mask in ((nb, b32 & (~b16)), (2 * nb, ~b32)):
        lc = _each(lambda l: _row_blocks(jnp.where(off_mask, l, 0.0), size, par).astype(BF16), l_mats)
        xb = _each(_to_bf16, x)
        t = _each(lambda l_, x_: _put_row_blocks(_mm(l_, x_).astype(BF16), size, par, None), lc, xb)
        xc = _each(lambda x_: _row_blocks(x_, size, par), x)
        r = _each(lambda x_, t_: x_ - _mm(x_.astype(BF16), t_), xc, t)
        x = _each(lambda r_, x_: _put_row_blocks(r_, size, par, x_), r, x)
    return x


def _chunk_cumsum(x, reverse):
    c = x.shape[0]
    row = lax.broadcasted_iota(jnp.int32, x.shape, 0)
    s = 1
    while s < c:
        if reverse:
            x = x + jnp.where(row < c - s, pltpu.roll(x, c - s, axis=0), 0.0)
        else:
            x = x + jnp.where(row >= s, pltpu.roll(x, s, axis=0), 0.0)
        s *= 2
    return x


def _chunk_local(lw, k, v, kk, r, a, k_a, reverse):
    c = lw[0].shape[0]
    n = 2 * c
    kd = _each(lambda k_, a_, ka_: k_ * (1.0 + (a_ - 1.0) * ka_), k, a, k_a)
    ka = _each(lambda kk_, a_: kk_ * a_, kk, a)
    cum = _each(lambda x: _chunk_cumsum(x, reverse), lw)
    tot = _each(lambda x: x[0:1] if reverse else x[c - 1:c], cum)
    e_neg = _each(lambda x: jnp.exp(-x), cum)
    e_end = _each(lambda t_, x: jnp.exp(t_ - x), tot, cum)
    stack_b = lambda x: _stack_heads(x).astype(BF16)
    alpha = _each(lambda kk_, c_, l_: stack_b(kk_ * jnp.exp(c_ - l_)), kk, cum, lw)
    beta = _each(lambda x, e: stack_b(x * e), ka, e_neg)
    kappa = _each(lambda x, e: stack_b(x * e), kd, e_neg)
    rho = _each(lambda r_, c_: _stack_heads(r_ * jnp.exp(c_)), r, cum)
    kappa_e = _each(lambda x, e: stack_b(x * e), kd, e_end)
    beta_e = _each(lambda x, e: stack_b(x * e), ka, e_end)
    v_st = _each(stack_b, v)
    w_end = _each(jnp.exp, tot)

    sc = _each(lambda al, rh, be, kp: _mm_nt(jnp.concatenate([al, rh.astype(BF16)], axis=0),
                                             jnp.concatenate([be, kp], axis=0)), alpha, rho, beta, kappa)
    ri = lax.broadcasted_iota(jnp.int32, (n, n), 0)
    ci = lax.broadcasted_iota(jnp.int32, (n, n), 1)
    strict = (ci > ri) if reverse else (ci < ri)
    incl = (ci >= ri) if reverse else (ci <= ri)
    l_mat = _each(lambda x: jnp.where(strict, x[:n, :n], 0.0), sc)
    apk = _each(lambda x: jnp.concatenate([jnp.where(strict, x[:n, n:], 0.0), jnp.where(incl, x[n:, n:], 0.0)],
                                          axis=0).astype(BF16), sc)
    pb = _each(lambda x: jnp.where(incl, x[n:, :n], 0.0).astype(BF16), sc)

    apkv = _each(_mm, apk, v_st)
    x = _each(_to_bf16, _unit_tri_inverse(l_mat, ri, ci, reverse))
    uwb = _each(lambda x_, kv, al: _mm(x_, jnp.concatenate([kv[:n].astype(BF16), al], axis=1)).astype(BF16),
                x, apkv, alpha)
    yr = _each(lambda kv, rh, pb_, uw: jnp.concatenate([kv[n:], rh], axis=1) - _mm(pb_, uw), apkv, rho, pb, uwb)
    uq = _each(_mm_tn, uwb, beta_e)
    n_t = _each(lambda v_, ke, uq_: _mm_tn(v_, ke) - uq_[:n], v_st, kappa_e, uq)
    q = _each(lambda uq_: uq_[n:].astype(BF16), uq)
    return list(zip(yr, n_t, q, w_end))


def _chunk_state(s, local):
    n = local[0][0].shape[0]
    c = n // 2
    sb = _each(_to_bf16, s)
    y_st = _each(lambda lc, sb_: lc[0][:, :n] + _mm_nt(lc[0][:, n:].astype(BF16), sb_), local, sb)
    y = _each(lambda x_: x_[:c] + x_[c:], y_st)
    s_new = _each(lambda s_, sb_, lc: s_ * lc[3] - _mm(sb_, lc[2]) + lc[1], s, sb, local)
    return s_new, y


SCAN_SLABS = 8
SCAN_PAIR = 4


def _scan_kernel(k_ref, v_ref, kk_ref, r_ref, lw_ref, a_ref, ka_ref, s0_ref, y_ref, sout_ref, s_scr,
                 *, reverse, chunk):
    tb = k_ref.shape[1]
    nslab = s_scr.shape[0]
    nchunk = tb // chunk
    step = pl.program_id(2)

    @pl.when(step == 0)
    def _():
        s_scr[...] = s0_ref[0]

    lanes = [slice(g * LANES, (g + 1) * LANES) for g in range(nslab)]
    pair = math.gcd(SCAN_PAIR, nchunk)

    def body(i, carry):
        cis = [(nchunk - 1 - (i * pair + p)) if reverse else (i * pair + p) for p in range(pair)]
        sls = [pl.ds(pl.multiple_of(ci * chunk, chunk), chunk) for ci in cis]
        get = lambda ref: [ref[0, sl, ln].astype(F32) for sl in sls for ln in lanes]
        local = _chunk_local(get(lw_ref), get(k_ref), get(v_ref), get(kk_ref), get(r_ref), get(a_ref),
                             [ka_ref[:, ln] for _ in sls for ln in lanes], reverse)
        s = [s_scr[g] for g in range(nslab)]
        for p, sl in enumerate(sls):
            s, y = _chunk_state(s, local[p * nslab:(p + 1) * nslab])
            for g in range(nslab):
                y_ref[0, sl, lanes[g]] = y[g]
        for g in range(nslab):
            s_scr[g] = s[g]
        return carry

    lax.fori_loop(0, nchunk // pair, body, 0)

    @pl.when(step == pl.num_programs(2) - 1)
    def _():
        sout_ref[0] = s_scr[...]


def _rwkv_scan(k, v, kk, r, lw, a, k_a, s0, reverse, tb):
    b, t, d = k.shape
    g = SCAN_SLABS
    gw = g * LANES
    nt = t // tb
    tmap = (lambda bi, j, c: (bi, nt - 1 - c, j)) if reverse else (lambda bi, j, c: (bi, c, j))
    tok = pl.BlockSpec((1, tb, gw), tmap)
    st = pl.BlockSpec((1, g, LANES, LANES), lambda bi, j, c: (bi, j, 0, 0))
    kern = functools.partial(_scan_kernel, reverse=reverse, chunk=CHUNK)
    return pl.pallas_call(
        kern,
        grid=(b, d // gw, nt),
        in_specs=[tok] * 6 + [pl.BlockSpec((1, gw), lambda bi, j, c: (0, j)), st],
        out_specs=[tok, st],
        out_shape=[jax.ShapeDtypeStruct((b, t, d), F32),
                   jax.ShapeDtypeStruct((b, d // LANES, LANES, LANES), F32)],
        scratch_shapes=[pltpu.VMEM((g, LANES, LANES), F32)],
        compiler_params=_cparams("parallel", "parallel", "arbitrary"),
        name="rwkv_scan_bwd" if reverse else "rwkv_scan_fwd",
    )(k, v, kk, r, lw, a, k_a, s0)


def _hy_in_kernel(h_ref, hp_ref, hn_ref, w_ref, cw_ref, cb_ref, x1_ref, x2_ref, v_ref, *, seq_len):
    tm, d = h_ref.shape
    halo = hp_ref.shape[0]
    cw = cw_ref[...]
    cb = cb_ref[...]
    nh = 2 if tm % (4 * halo) == 0 else 1
    th = tm // nh
    pes = []
    for i in range(nh):
        before = hp_ref[...] if i == 0 else h_ref[i * th - halo:i * th, :]
        after = hn_ref[...] if i == nh - 1 else h_ref[(i + 1) * th:(i + 1) * th + halo, :]
        hext = jnp.concatenate([before, h_ref[i * th:(i + 1) * th, :], after], axis=0).astype(BF16)
        pes.append(jnp.dot(hext, w_ref[...], preferred_element_type=F32))
    for i, pe in enumerate(pes):
        t = (pl.program_id(0) * tm + i * th + lax.broadcasted_iota(jnp.int32, (th, 1), 0)) % seq_len
        prev = jnp.where(t >= 1, pe[halo - 1:halo - 1 + th], 0.0)
        nxt = jnp.where(t + 1 < seq_len, pe[halo + 1:halo + 1 + th], 0.0)
        u = prev * cw[0:1] + pe[halo:halo + th] * cw[1:2] + nxt * cw[2:3] + cb
        rows = slice(i * th, (i + 1) * th)
        x1_ref[rows, :] = u[:, :d]
        x2_ref[rows, :] = u[:, d:2 * d]
        v_ref[rows, :] = u[:, 2 * d:]


def _hy_in(h, seq_len, w_in, conv_w, conv_b, tm):
    m, d = h.shape
    halo = 8
    nb = tm // halo
    last = m // halo - 1
    tok = pl.BlockSpec((tm, d), lambda i: (i, 0))
    full = lambda a: pl.BlockSpec(a.shape, lambda i: (0,) * a.ndim, pipeline_mode=pl.Buffered(1))
    return pl.pallas_call(
        functools.partial(_hy_in_kernel, seq_len=seq_len),
        grid=(m // tm,),
        in_specs=[tok,
                  pl.BlockSpec((halo, d), lambda i: (jnp.maximum(i * nb - 1, 0), 0)),
                  pl.BlockSpec((halo, d), lambda i: (jnp.minimum((i + 1) * nb, last), 0)),
                  full(w_in), full(conv_w), full(conv_b)],
        out_specs=[tok] * 3,
        out_shape=[jax.ShapeDtypeStruct((m, d), F32)] * 3,
        compiler_params=_cparams("parallel"),
        name="hyena_in",
    )(h, h, h, w_in, conv_w, conv_b)


def _hy_filter_kernel(z_ref, w1_ref, b1_ref, fr_ref, w2_ref, b2_ref, w3_ref, dl_ref, o_ref):
    hp = lax.Precision.HIGHEST
    z = z_ref[...]
    fr = fr_ref[...]
    hid = jnp.sin(fr[0:1] * (jnp.dot(z, w1_ref[...], precision=hp, preferred_element_type=F32) + b1_ref[...]))
    hid = jnp.sin(fr[1:2] * (jnp.dot(hid, w2_ref[...], precision=hp, preferred_element_type=F32) + b2_ref[...]))
    filt = jnp.dot(hid, w3_ref[...], precision=hp, preferred_element_type=F32)
    o_ref[...] = filt * jnp.exp(-z[:, 0:1] * jnp.abs(dl_ref[...]))


def _hy_filters(seq_len, f_w1, f_b1, f_freq, f_w2, f_b2, f_w3, deltas):
    t = jnp.linspace(0.0, 1.0, seq_len, dtype=F32)[:, None]
    ang = ((2 * math.pi / seq_len) * jnp.arange(seq_len, dtype=F32)[:, None]
           * jnp.linspace(1e-4, HY_BANDS - 1, HY_BANDS, dtype=F32)[None])
    z = jnp.concatenate([t, jnp.cos(ang), -jnp.sin(ang)], axis=-1)
    emb, hid = f_w1.shape
    z = jnp.pad(z, ((0, 0), (0, LANES - emb)))
    w1 = jnp.pad(f_w1, ((0, LANES - emb), (0, 0)))
    n_out = f_w3.shape[1]
    tl = min(seq_len, 512)
    full = lambda a: pl.BlockSpec(a.shape, lambda i: (0,) * a.ndim)
    args = (w1, f_b1.reshape(1, hid), f_freq, f_w2, f_b2.reshape(1, hid), f_w3, deltas.reshape(1, n_out))
    return pl.pallas_call(
        _hy_filter_kernel,
        grid=(seq_len // tl,),
        in_specs=[pl.BlockSpec((tl, LANES), lambda i: (i, 0))] + [full(a) for a in args],
        out_specs=pl.BlockSpec((tl, n_out), lambda i: (i, 0)),
        out_shape=jax.ShapeDtypeStruct((seq_len, n_out), F32),
        compiler_params=_cparams("parallel"),
        name="hyena_filters",
    )(z, *args)


DFT_UNROLL = 8


def _rdft_tables(seq_len):
    big = 2 * seq_len
    n = int(round(math.sqrt(big)))
    assert n * n == big and n % 32 == 0, "sequence length must give a square, tile-aligned DFT"
    kh = n // 2 + 1
    kp = -(-kh // 8) * 8
    k1 = np.arange(kp)[None, :, None]
    n1 = np.arange(n // 2)[None, None, :]
    n2 = np.arange(n)[:, None, None]
    ang = -2.0 * np.pi * ((k1 * (n * n1 + n2)) % big) / big
    live = (k1 < kh).astype(np.float64)
    e_in = np.concatenate([np.cos(ang) * live, np.sin(ang) * live], axis=1)
    wgt = np.where((k1 == 0) | (k1 == n // 2), 1.0, 2.0) * live
    e_out = np.transpose(np.concatenate([np.cos(ang) * wgt, np.sin(ang) * wgt], axis=1), (0, 2, 1))
    a2 = -2.0 * np.pi * ((np.arange(n)[:, None] * np.arange(n)[None, :]) % n) / n
    fr, fi = np.cos(a2), np.sin(a2)
    f_fwd = np.block([[fr, -fi], [fi, fr]])
    f_inv = np.block([[fr, fi], [-fi, fr]])
    kg = max(g for g in range(1, 21) if kh % g == 0)
    cast = lambda a: jnp.asarray(a, dtype=F32).astype(BF16)
    return dict(n=n, kh=kh, kp=kp, kg=kg, e_in=cast(e_in), e_out=cast(e_out), f_fwd=cast(f_fwd), f_inv=cast(f_inv))


ROW_PAD = 8


def _rdft_first_stage(u_ref, e_in_ref, x_pad, a_scr, n, kp):
    pitch = n + ROW_PAD
    for n1 in range(n // 2):
        x_pad[n1 * pitch:n1 * pitch + n, :] = u_ref[0, n1 * n:(n1 + 1) * n, :]

    def body(i, carry):
        n2s = [i * DFT_UNROLL + j for j in range(DFT_UNROLL)]
        xs = [x_pad[pl.ds(n2, n // 2, stride=pitch), :].astype(BF16) for n2 in n2s]
        acc = [_mm(e_in_ref[n2], x) for n2, x in zip(n2s, xs)]
        for n2, a in zip(n2s, acc):
            a_scr[0, pl.ds(n2, kp, stride=pitch), :] = a[:kp]
            a_scr[1, pl.ds(n2, kp, stride=pitch), :] = a[kp:]
        return carry

    lax.fori_loop(0, n // DFT_UNROLL, body, 0)


def _rdft_second_stage(a_scr, k1s, n, f_fwd):
    pitch = n + ROW_PAD
    cols = []
    for k1 in k1s:
        rows = pl.ds(pl.multiple_of(k1 * pitch, 8), n)
        cols.append(jnp.concatenate([a_scr[0, rows, :], a_scr[1, rows, :]], axis=0).astype(BF16))
    return [_mm(f_fwd, c) for c in cols]


def _hy_conv_kernel(u_ref, gate_ref, h_ref, ein_ref, eout_ref, ff_ref, fi_ref, bias_ref, o_ref, x_pad, a_scr,
                    *, n, kp, kh, mg):
    _rdft_first_stage(u_ref, ein_ref, x_pad, a_scr, n, kp)

    f_fwd, f_inv = ff_ref[...], fi_ref[...]

    pitch = n + ROW_PAD

    def mid(i, carry):
        for j0 in range(0, mg, DFT_UNROLL):
            k1s = [i * mg + j for j in range(j0, min(j0 + DFT_UNROLL, mg))]
            xs = _rdft_second_stage(a_scr, k1s, n, f_fwd)
            ys = []
            for k1, x in zip(k1s, xs):
                xr, xi = x[:n], x[n:]
                hr, hi = h_ref[0, k1], h_ref[1, k1]
                ys.append(jnp.concatenate([xr * hr - xi * hi, xr * hi + xi * hr], axis=0).astype(BF16))
            zs = [_mm(f_inv, y) for y in ys]
            for k1, z in zip(k1s, zs):
                rows = pl.ds(pl.multiple_of(k1 * pitch, 8), n)
                a_scr[0, rows, :] = z[:n]
                a_scr[1, rows, :] = z[n:]
        return carry

    lax.fori_loop(0, kh // mg, mid, 0)

    inv_n = 1.0 / (n * n)
    bias = bias_ref[...]

    def last(i, carry):
        n2s = [i * DFT_UNROLL + j for j in range(DFT_UNROLL)]
        zc = []
        for n2 in n2s:
            rows = pl.ds(n2, kp, stride=pitch)
            zc.append(jnp.concatenate([a_scr[0, rows, :], a_scr[1, rows, :]], axis=0).astype(BF16))
        ys = [_mm(eout_ref[n2], z) * inv_n for n2, z in zip(n2s, zc)]
        for n2, y in zip(n2s, ys):
            x_pad[pl.ds(n2, n // 2, stride=pitch), :] = y
        return carry

    lax.fori_loop(0, n // DFT_UNROLL, last, 0)
    for n1 in range(n // 2):
        rows = slice(n1 * n, (n1 + 1) * n)
        y = x_pad[n1 * pitch:n1 * pitch + n, :]
        o_ref[0, rows, :] = gate_ref[0, rows, :] * (y + u_ref[0, rows, :] * bias)


def _hy_conv(u, gate, spec, f_idx, bias, tb):
    b, l, d = u.shape
    n, kh, kp, mg = tb["n"], tb["kh"], tb["kp"], tb["kg"]
    nd = d // LANES
    once = dict(pipeline_mode=pl.Buffered(1))
    tok = lambda bufs: pl.BlockSpec((1, l, LANES), lambda c, bi: (bi, 0, c), **bufs)
    full = lambda a: pl.BlockSpec(a.shape, lambda c, bi: (0,) * a.ndim, **once)
    return pl.pallas_call(
        functools.partial(_hy_conv_kernel, n=n, kp=kp, kh=kh, mg=mg),
        grid=(nd, b),
        in_specs=[tok({}), tok(once),
                  pl.BlockSpec((2, kh, n, LANES), lambda c, bi: (0, 0, 0, f_idx * nd + c), **once),
                  full(tb["e_in"]), full(tb["e_out"]), full(tb["f_fwd"]), full(tb["f_inv"]),
                  pl.BlockSpec((1, LANES), lambda c, bi: (0, c))],
        out_specs=tok({}),
        out_shape=jax.ShapeDtypeStruct((b, l, d), F32),
        scratch_shapes=[pltpu.VMEM((n // 2 * (n + ROW_PAD), LANES), F32), pltpu.VMEM((2, kp * (n + ROW_PAD), LANES), F32)],
        compiler_params=_cparams("parallel", "parallel"),
        name="hyena_conv",
    )(u, gate, spec, tb["e_in"], tb["e_out"], tb["f_fwd"], tb["f_inv"], bias.reshape(1, d))


def _hy_spec_kernel(u_ref, ein_ref, ff_ref, h_ref, x_pad, a_scr, *, n, kp, kg):
    step = pl.program_id(1)

    @pl.when(step == 0)
    def _():
        _rdft_first_stage(u_ref, ein_ref, x_pad, a_scr, n, kp)

    f_fwd = ff_ref[...]
    for j0 in range(0, kg, DFT_UNROLL):
        js = list(range(j0, min(j0 + DFT_UNROLL, kg)))
        xs = _rdft_second_stage(a_scr, [step * kg + j for j in js], n, f_fwd)
        for j, x in zip(js, xs):
            h_ref[0, j] = x[:n]
            h_ref[1, j] = x[n:]


def _hy_spectrum(filt, tb):
    _, l, dx = filt.shape
    n, kh, kp, kg = tb["n"], tb["kh"], tb["kp"], tb["kg"]
    full = lambda a: pl.BlockSpec(a.shape, lambda c, g: (0,) * a.ndim, pipeline_mode=pl.Buffered(1))
    return pl.pallas_call(
        functools.partial(_hy_spec_kernel, n=n, kp=kp, kg=kg),
        grid=(dx // LANES, kh // kg),
        in_specs=[pl.BlockSpec((1, l, LANES), lambda c, g: (0, 0, c)), full(tb["e_in"]), full(tb["f_fwd"])],
        out_specs=pl.BlockSpec((2, kg, n, LANES), lambda c, g: (0, g, 0, c)),
        out_shape=jax.ShapeDtypeStruct((2, kh, n, dx), F32),
        scratch_shapes=[pltpu.VMEM((n // 2 * (n + ROW_PAD), LANES), F32), pltpu.VMEM((2, kp * (n + ROW_PAD), LANES), F32)],
        compiler_params=_cparams("parallel", "arbitrary"),
        name="hyena_filter_spectrum",
    )(filt, tb["e_in"], tb["f_fwd"])


def _hyena_mixer(hl, batch, hp, tm):
    (w_in, conv_w, conv_b, f_w1, f_b1, f_freq, f_w2, f_b2, f_w3, deltas, bias) = hp
    m, d = hl.shape
    seq_len = m // batch
    x1, x2, v = _hy_in(hl, seq_len, w_in.astype(BF16), conv_w, conv_b.reshape(1, -1), tm)
    three = lambda a: a.reshape(batch, seq_len, d)
    x1, x2, v = three(x1), three(x2), three(v)
    tables = _rdft_tables(seq_len)
    filt = _hy_filters(seq_len, f_w1, f_b1, f_freq, f_w2, f_b2, f_w3, deltas)
    spec = _hy_spectrum(filt[None], tables)
    z = _hy_conv(v, x1, spec, 0, bias[0], tables)
    z = _hy_conv(z, x2, spec, 1, bias[1], tables)
    return z.reshape(m, d)


def _rwkv_weights(mix, w_rkv, w0, w1, w2, a0, a1, a2, g1, g2, k_k, k_a, r_k):
    d = mix.shape[-1]
    lora = w1.shape[-1]
    zero = jnp.zeros((lora, d), F32)
    blockdiag = lambda m: jnp.concatenate([jnp.concatenate([m[0], zero], axis=1),
                                           jnp.concatenate([zero, m[1]], axis=1)], axis=0)
    gl = g1.shape[-1]
    glp = -(-gl // LANES) * LANES
    vec = jnp.zeros((8, 2 * d), F32)
    vec = vec.at[0].set(w0.reshape(-1)).at[1].set(a0.reshape(-1))
    vec = vec.at[2, :d].set(k_k).at[3, :d].set(k_a).at[4, :d].set(r_k.reshape(-1))
    return (mix, w_rkv.astype(BF16),
            jnp.concatenate([w1[0], w1[1]], axis=1).astype(BF16), blockdiag(w2).astype(BF16),
            jnp.concatenate([a1[0], a1[1]], axis=1).astype(BF16), blockdiag(a2).astype(BF16),
            jnp.pad(g1, ((0, 0), (0, glp - gl))).astype(BF16),
            jnp.pad(g2, ((0, glp - gl), (0, 0))).astype(BF16), vec)


def _rwkv_mixer(hl, hc, batch, wts, k_a, tm, tb):
    d = hl.shape[1]
    t_lat, t_ctx = hl.shape[0] // batch, hc.shape[0] // batch
    k_a = k_a.reshape(1, d)
    s0 = jnp.zeros((batch, d // LANES, LANES, LANES), F32)
    three = lambda arrs, t: [a.reshape(batch, t, d) for a in arrs]
    kc, vc, kkc, rc, lw0c, lw1c, a0c, a1c, _, _ = _rwkv_prep(hc, t_ctx, None, wts, min(tm, t_ctx))
    kc, vc, kkc, rc, lw0c, lw1c, a0c, a1c = three((kc, vc, kkc, rc, lw0c, lw1c, a0c, a1c), t_ctx)
    tbc = min(tb, t_ctx)
    _, s_f = _rwkv_scan(kc, vc, kkc, rc, lw0c, a0c, k_a, s0, False, tbc)
    _, s_b = _rwkv_scan(kc, vc, kkc, rc, lw1c, a1c, k_a, s0, True, tbc)
    k, v, kk, r, lw0, lw1, a0, a1, bonus, g = _rwkv_prep(hl, t_lat, GRID_W, wts, tm)
    k, v, kk, r, lw0, lw1, a0, a1 = three((k, v, kk, r, lw0, lw1, a0, a1), t_lat)
    yf, _ = _rwkv_scan(k, v, kk, r, lw0, a0, k_a, s_f, False, tb)
    yb, _ = _rwkv_scan(k, v, kk, r, lw1, a1, k_a, s_b, True, tb)
    return yf.reshape(-1, d), yb.reshape(-1, d), bonus, g


TOKEN_TILE = 512
PREP_TILE = 256
SCAN_BLOCK = 512


def kernel(x, c, ctx, c_ctx, mod_w, mod_b, norm_g, ffn_w_gu, ffn_w_down, rw_mix, rw_w_rkv, rw_w_o, rw_w0, rw_w1, rw_w2, rw_a0, rw_a1, rw_a2, rw_g1, rw_g2, rw_k_k, rw_k_a, rw_r_k, rw_ln_g, rw_ln_b, hy_w_in, hy_conv_w, hy_conv_b, hy_f_w1, hy_f_b1, hy_f_freq, hy_f_w2, hy_f_b2, hy_f_w3, hy_deltas, hy_bias, hy_w_out, final_g):
    batch, seq, d = x.shape
    t_ctx = ctx.shape[1]
    depth = mod_w.shape[0]
    assert depth == 2 and rw_mix.shape[0] == 1 and hy_w_in.shape[0] == 1
    assert seq % TOKEN_TILE == 0 and seq % GRID_W == 0 and t_ctx % CHUNK == 0 and batch + 1 <= 8

    cc = jnp.concatenate([c, c_ctx[None], jnp.zeros((8 - batch - 1, d), F32)], axis=0)
    mods0 = _modulation(cc, mod_w, mod_b, 0)
    mods1 = _modulation(cc, mod_w, mod_b, 1)
    wgu = ffn_w_gu.astype(BF16)
    wd = ffn_w_down.astype(BF16)
    xl = x.reshape(batch * seq, d)
    xc = ctx.reshape(batch * t_ctx, d)
    tm = TOKEN_TILE
    tmc = min(tm, t_ctx)

    xl, hl = _ffn(xl, mods0, 0, seq, norm_g[0, 0:2], wgu, wd, 0, 0, 0, 3, tm)
    _, hc = _ffn(xc, mods0, batch, batch * t_ctx, norm_g[0, 0:2], wgu, wd, 0, 0, 0, 3, tmc)
    wts = _rwkv_weights(rw_mix[0], rw_w_rkv[0], rw_w0[0], rw_w1[0], rw_w2[0], rw_a0[0], rw_a1[0], rw_a2[0],
                        rw_g1[0], rw_g2[0], rw_k_k[0], rw_k_a[0], rw_r_k[0])
    yf, yb, bonus, g = _rwkv_mixer(hl, hc, batch, wts, rw_k_a[0], PREP_TILE, SCAN_BLOCK)
    rwkv_res = ("rwkv", yf, yb, bonus, g, jnp.stack([rw_ln_g[0], rw_ln_b[0]]), rw_w_o[0].astype(BF16))
    xl, _ = _ffn(xl, mods0, 0, seq, jnp.stack([norm_g[0, 2], final_g]), wgu, wd, 0, 1, 6, None, tm,
                 pre=rwkv_res)

    xl, hl = _ffn(xl, mods1, 0, seq, norm_g[1, 0:2], wgu, wd, 1, 0, 0, 3, tm)
    hp = (hy_w_in[0], hy_conv_w[0], hy_conv_b[0], hy_f_w1[0], hy_f_b1[0], hy_f_freq[0], hy_f_w2[0],
          hy_f_b2[0], hy_f_w3[0], hy_deltas[0], hy_bias[0])
    z = _hyena_mixer(hl, batch, hp, tm)
    out, _ = _ffn(xl, mods1, 0, seq, jnp.stack([norm_g[1, 2], final_g]), wgu, wd, 1, 1, 6, "final", tm,
                  pre=("proj", z, hy_w_out[0].astype(BF16)))
    return out.reshape(batch, seq, d)
```
